```python
import jax, jax.numpy as jnp
from jax import lax
import numpy as np

D_MODEL = 2048
BATCH = 4
SEQ = 2048
DEPTH = 1
DEC_BATCH = 128
DEC_SEQ = 8
PAST_LEN = 16384
PAGE_SIZE = 128

N_META = 16
W_A = D_MODEL // 2
HEAD_A = 64
H_A = W_A // HEAD_A
LORA_W = 64
LORA_A = 64
W_B = D_MODEL - W_A
H_B = 8
DK_B = W_B // H_B
CHUNK = 128
N_A_COLS = 4 * W_A + LORA_W + LORA_A
N_B_COLS = 4 * W_B
N_IN = N_A_COLS + N_B_COLS
ALPHA = (2.0 * DEPTH) ** 0.25
BETA = (8.0 * DEPTH) ** -0.25
ROPE_BASE = 10000.0
LN_EPS = 1e-5
GN_EPS_A = 64e-5
GN_EPS_B = 1e-5

kernel_name = "hymba_rwkv7_retention_deepnorm_step"


def _layer_norm(x, g, b):
    xf = x.astype(jnp.float32)
    mu = jnp.mean(xf, -1, keepdims=True)
    var = jnp.mean(jnp.square(xf - mu), -1, keepdims=True)
    return ((xf - mu) * lax.rsqrt(var + LN_EPS) * g.astype(jnp.float32) + b.astype(jnp.float32)).astype(x.dtype)


def _group_norm(x, n_heads, g, b, eps):
    shp = x.shape
    xh = x.astype(jnp.float32).reshape(shp[:-1] + (n_heads, shp[-1] // n_heads))
    mu = jnp.mean(xh, -1, keepdims=True)
    var = jnp.mean(jnp.square(xh - mu), -1, keepdims=True)
    xh = ((xh - mu) * lax.rsqrt(var + eps)).reshape(shp)
    return xh * g.astype(jnp.float32) + b.astype(jnp.float32)


def _rotary(x, pos):
    half = x.shape[-1] // 2
    inv = ROPE_BASE ** (-jnp.arange(half, dtype=jnp.float32) / half)
    ang = pos.astype(jnp.float32)[:, None] * inv[None, :]
    cos = jnp.cos(ang)[None, :, None, :]
    sin = jnp.sin(ang)[None, :, None, :]
    x1, x2 = x[..., :half], x[..., half:]
    return jnp.concatenate([x1 * cos - x2 * sin, x1 * sin + x2 * cos], axis=-1)


def _rwkv7_mixer(za, shift_in, s0, shift_mix, w0, w_up, a0, a_up, k_k, k_a, r_k, gn_g, gn_b):
    B, T, _ = za.shape
    f32 = jnp.float32
    za = za.astype(f32)
    prev = jnp.concatenate([shift_in.astype(f32)[:, None, :], za[:, :-1]], axis=1)
    zs = za + (prev - za) * shift_mix.astype(f32)
    r, k, v, g, zw, zx = jnp.split(zs, [W_A, 2 * W_A, 3 * W_A, 4 * W_A, 4 * W_A + LORA_W], axis=-1)
    w_logit = w0.astype(f32) + jnp.tanh(zw) @ w_up.astype(f32)
    log_w = -jnp.exp(-jax.nn.softplus(-w_logit) - 0.5)
    a = jax.nn.sigmoid(a0.astype(f32) + zx @ a_up.astype(f32))
    hd = lambda t: t.reshape(B, T, H_A, HEAD_A)
    kk = hd(k * k_k.astype(f32))
    kk = kk / jnp.maximum(jnp.sqrt(jnp.sum(kk * kk, -1, keepdims=True)), 1e-12)
    k = k * (1.0 + (a - 1.0) * k_a.astype(f32))
    rh, kh, vh, ah, wh = hd(r), hd(k), hd(v), hd(a), jnp.exp(hd(log_w))

    def step(S, inp):
        r_t, w_t, k_t, v_t, kk_t, b_t = inp
        sa = jnp.einsum('bhvk,bhk->bhv', S, -kk_t)
        S = S * w_t[:, :, None, :] + sa[..., None] * b_t[:, :, None, :] + v_t[..., None] * k_t[:, :, None, :]
        return S, jnp.einsum('bhvk,bhk->bhv', S, r_t)

    xs = tuple(jnp.swapaxes(t, 0, 1) for t in (rh, wh, kh, vh, kk, kk * ah))
    s_T, o = lax.scan(step, s0.astype(f32), xs)
    o = _group_norm(jnp.swapaxes(o, 0, 1).reshape(B, T, W_A), H_A, gn_g, gn_b, GN_EPS_A)
    bonus = (jnp.sum(rh * kh * r_k.astype(f32), -1, keepdims=True) * vh).reshape(B, T, W_A)
    return (o + bonus) * jax.nn.silu(g), s_T, za[:, -1]


def _retention_block(S, q, k, v, log_g):
    C = q.shape[1]
    i = jnp.arange(C, dtype=jnp.float32)
    diff = i[:, None] - i[None, :]
    dmask = jnp.where(diff >= 0, jnp.exp(jnp.maximum(diff, 0.0)[None] * log_g[:, None, None]), 0.0)
    scores = jnp.einsum('bihd,bjhd->bhij', q, k) * dmask[None]
    inner = jnp.einsum('bhij,bjhe->bihe', scores, v)
    q_dec = q * jnp.exp((i + 1.0)[:, None] * log_g[None, :])[None, :, :, None]
    cross = jnp.einsum('bihd,bhde->bihe', q_dec, S)
    k_dec = k * jnp.exp((C - 1.0 - i)[:, None] * log_g[None, :])[None, :, :, None]
    S = jnp.exp(C * log_g)[None, :, None, None] * S + jnp.einsum('bjhd,bjhe->bhde', k_dec, v)
    return S, inner + cross


def _retention_blocks(S, q, k, v, log_g, block):
    B, T, H, _ = q.shape
    n = T // block
    xs = tuple(jnp.swapaxes(t.reshape(B, n, block, H, t.shape[-1]), 0, 1) for t in (q, k, v))
    S, o = lax.scan(lambda s, qkv: _retention_block(s, qkv[0], qkv[1], qkv[2], log_g), S, xs)
    return S, jnp.swapaxes(o, 0, 1).reshape(B, T, H, -1)


def _retention_mixer(zb, pos, s0, lead, block, gn_g, gn_b):
    B, T, _ = zb.shape
    f32 = jnp.float32
    q, k, v, g = jnp.split(zb.astype(f32), 4, axis=-1)
    hd = lambda t: t.reshape(B, T, H_B, DK_B)
    q = _rotary(hd(q), pos)
    k = _rotary(hd(k), pos) * (DK_B ** -0.5)
    v = hd(v)
    log_g = jnp.log1p(-jnp.exp2(-5.0 - jnp.arange(H_B, dtype=f32)))
    S = s0.astype(f32)
    outs = []
    if lead > 0:
        S, o_lead = _retention_block(S, q[:, :lead], k[:, :lead], v[:, :lead], log_g)
        outs.append(o_lead)
    S, o_rest = _retention_blocks(S, q[:, lead:], k[:, lead:], v[:, lead:], log_g, block)
    outs.append(o_rest)
    o = jnp.concatenate(outs, axis=1).reshape(B, T, W_B)
    o = _group_norm(o, H_B, gn_g, gn_b, GN_EPS_B)
    return o * jax.nn.silu(g), S


def _layer(x, pos, shift_in, s_a, s_b, lead, block, w_in, w_out, shift_mix, w0, w_up, a0, a_up,
           k_k, k_a, r_k, gn_a_g, gn_a_b, gn_b_g, gn_b_b, ln_g, ln_b):
    z = jnp.einsum('btd,dn->btn', x, w_in)
    o_a, s_a, shift = _rwkv7_mixer(z[..., :N_A_COLS], shift_in, s_a, shift_mix, w0, w_up, a0, a_up,
                                   k_k, k_a, r_k, gn_a_g, gn_a_b)
    o_b, s_b = _retention_mixer(z[..., N_A_COLS:], pos, s_b, lead, block, gn_b_g, gn_b_b)
    h = jnp.einsum('btc,cd->btd', jnp.concatenate([o_a, o_b], axis=-1).astype(x.dtype), w_out)
    y = _layer_norm(ALPHA * x + h, ln_g, ln_b)
    return y, s_a, shift, s_b


def _run_trunk(x, pos, s_rwkv, s_shift, s_ret, lead, block, w_in, w_out, shift_mix, w0, w_up, a0, a_up,
               k_k, k_a, r_k, gn_a_g, gn_a_b, gn_b_g, gn_b_b, ln_g, ln_b):
    new_rwkv, new_shift, new_ret = [], [], []
    for l in range(DEPTH):
        x, sa, sh, sb = _layer(x, pos, s_shift[l], s_rwkv[l], s_ret[l], lead, block, w_in[l], w_out[l],
                               shift_mix[l], w0[l], w_up[l], a0[l], a_up[l], k_k[l], k_a[l], r_k[l],
                               gn_a_g[l], gn_a_b[l], gn_b_g[l], gn_b_b[l], ln_g[l], ln_b[l])
        new_rwkv.append(sa.astype(x.dtype))
        new_shift.append(sh.astype(x.dtype))
        new_ret.append(sb.astype(x.dtype))
    return x, jnp.stack(new_rwkv), jnp.stack(new_shift), jnp.stack(new_ret)


def setup_inputs(seed: int = 0) -> dict:
    key = jax.random.key(seed)
    ks = jax.random.split(key, 24)
    f32 = jnp.float32
    nrm = lambda kk, shape, s: s * jax.random.normal(kk, shape, f32)
    return {
        'x_prompt': nrm(ks[0], (BATCH, SEQ, D_MODEL), 1.0),
        'x_sample': nrm(ks[1], (DEC_BATCH, DEC_SEQ, D_MODEL), 1.0),
        'state_rwkv': nrm(ks[2], (DEPTH, DEC_BATCH, H_A, HEAD_A, HEAD_A), 0.5),
        'state_shift': nrm(ks[3], (DEPTH, DEC_BATCH, N_A_COLS), 1.0),
        'state_ret': nrm(ks[4], (DEPTH, DEC_BATCH, H_B, DK_B, DK_B), 1.0),
        'meta_tokens': nrm(ks[5], (N_META, D_MODEL), 1.0),
        'w_in': nrm(ks[6], (DEPTH, D_MODEL, N_IN), D_MODEL ** -0.5),
        'w_out': nrm(ks[7], (DEPTH, W_A + W_B, D_MODEL), BETA * (W_A + W_B) ** -0.5),
        'shift_mix': jax.random.uniform(ks[8], (DEPTH, N_A_COLS), f32),
        'w0': jax.random.uniform(ks[9], (DEPTH, W_A), f32, -6.5, -1.5),
        'w_up': nrm(ks[10], (DEPTH, LORA_W, W_A), 0.5 * LORA_W ** -0.5),
        'a0': nrm(ks[11], (DEPTH, W_A), 0.1),
        'a_up': nrm(ks[12], (DEPTH, LORA_A, W_A), 0.5 * LORA_A ** -0.5),
        'k_k': 0.85 + nrm(ks[13], (DEPTH, W_A), 0.02),
        'k_a': 1.0 + nrm(ks[14], (DEPTH, W_A), 0.02),
        'r_k': nrm(ks[15], (DEPTH, H_A, HEAD_A), 0.1),
        'gn_a_g': 1.0 + nrm(ks[16], (DEPTH, W_A), 0.01),
        'gn_a_b': nrm(ks[17], (DEPTH, W_A), 0.01),
        'gn_b_g': 1.0 + nrm(ks[18], (DEPTH, W_B), 0.01),
        'gn_b_b': nrm(ks[19], (DEPTH, W_B), 0.01),
        'ln_g': 1.0 + nrm(ks[20], (DEPTH, D_MODEL), 0.01),
        'ln_b': nrm(ks[21], (DEPTH, D_MODEL), 0.01),
    }


def reference(x_prompt, x_sample, state_rwkv, state_shift, state_ret, meta_tokens, w_in, w_out, shift_mix,
              w0, w_up, a0, a_up, k_k, k_a, r_k, gn_a_g, gn_a_b, gn_b_g, gn_b_b, ln_g, ln_b):
    weights = (w_in, w_out, shift_mix, w0, w_up, a0, a_up, k_k, k_a, r_k, gn_a_g, gn_a_b, gn_b_g, gn_b_b, ln_g, ln_b)
    dt = x_prompt.dtype
    bp = x_prompt.shape[0]
    meta = jnp.broadcast_to(meta_tokens.astype(dt)[None], (bp, N_META, D_MODEL))
    xp = jnp.concatenate([meta, x_prompt], axis=1)
    pos_p = jnp.arange(xp.shape[1], dtype=jnp.int32)
    z_rwkv = jnp.zeros((DEPTH, bp, H_A, HEAD_A, HEAD_A), dt)
    z_shift = jnp.zeros((DEPTH, bp, N_A_COLS), dt)
    z_ret = jnp.zeros((DEPTH, bp, H_B, DK_B, DK_B), dt)
    yp, rwkv_p, shift_p, ret_p = _run_trunk(xp, pos_p, z_rwkv, z_shift, z_ret, N_META, CHUNK, *weights)
    y_prompt = yp[:, N_META:]
    pos_s = PAST_LEN + jnp.arange(x_sample.shape[1], dtype=jnp.int32)
    y_sample, rwkv_s, shift_s, ret_s = _run_trunk(x_sample, pos_s, state_rwkv, state_shift, state_ret,
                                                  0, x_sample.shape[1], *weights)
    return (y_prompt, y_sample, rwkv_p, shift_p, ret_p, rwkv_s, shift_s, ret_s)
```

```python
import functools
import math

import jax
import jax.numpy as jnp
from jax import lax
from jax.experimental import pallas as pl
from jax.experimental.pallas import tpu as pltpu

D_MODEL = 2048
N_META = 16
W_A = 1024
HEAD_A = 64
H_A = W_A // HEAD_A
LORA_W = 64
LORA_A = 64
W_B = D_MODEL - W_A
H_B = 8
DK_B = W_B // H_B
N_A_COLS = 4 * W_A + LORA_W + LORA_A
N_B_COLS = 4 * W_B
DEPTH = 1
PAST_LEN = 16384
ALPHA = (2.0 * DEPTH) ** 0.25
ROPE_BASE = 10000.0
LN_EPS = 1e-5
GN_EPS_A = 64e-5
GN_EPS_B = 1e-5
EXP_NEG_HALF = math.exp(-0.5)

LANES = 128
N_PAIRS = W_A // LANES
VMEM_LIMIT_BYTES = 56 * 1024 * 1024

F32 = jnp.float32
BF16 = jnp.bfloat16


def _dot(a, b):
    return jnp.dot(a, b, preferred_element_type=F32)


def _dot_nt(a, b):
    return lax.dot_general(a, b, (((1,), (1,)), ((), ())), preferred_element_type=F32)


def _dot_tn(a, b):
    return lax.dot_general(a, b, (((0,), (0,)), ((), ())), preferred_element_type=F32)


def _split_dot(x, w_bf16):
    hi = x.astype(BF16)
    lo = (x - hi.astype(F32)).astype(BF16)
    return _dot(hi, w_bf16) + _dot(lo, w_bf16)


def _sigmoid(x):
    return 1.0 / (1.0 + jnp.exp(-x))


def _batch_loop(n, body):
    if n == 1:
        body(0)
    else:
        lax.fori_loop(0, n, lambda i, carry: (body(i), carry)[1], 0)


def _matmul_kernel(x_ref, w_ref, o_ref):
    o_ref[...] = _dot(x_ref[...], w_ref[...])


def _matmul(x, w, tm, tn):
    m, k = x.shape
    n = w.shape[1]
    assert m % tm == 0 and n % tn == 0
    return pl.pallas_call(
        _matmul_kernel,
        grid=(m // tm, n // tn),
        in_specs=[pl.BlockSpec((tm, k), lambda i, j: (i, 0)),
                  pl.BlockSpec((k, tn), lambda i, j: (0, j))],
        out_specs=pl.BlockSpec((tm, tn), lambda i, j: (i, j)),
        out_shape=jax.ShapeDtypeStruct((m, n), F32),
        compiler_params=pltpu.CompilerParams(
            dimension_semantics=("arbitrary", "arbitrary"), vmem_limit_bytes=VMEM_LIMIT_BYTES),
        name="in_proj",
    )(x, w)


def _rwkv_kernel(za_ref, shift_ref, s0_ref, mix_ref, wwa_ref, w0_ref, a0_ref, kk_ref, ka_ref,
                 rk_ref, gng_ref, gnb_ref, ones_ref,
                 ya_ref, s_out_ref,
                 zs_scr, lora_scr, prev_scr, state_scr, *, chunk, n_chunks, bblk):
    C = chunk
    c = pl.program_id(1)

    @pl.when(c == 0)
    def _():
        prev_scr[...] = shift_ref[...]
        state_scr[...] = s0_ref[...]

    ones_bd = ones_ref[...]
    lane = lax.broadcasted_iota(jnp.int32, (C, LANES), 1)
    head0 = lane < HEAD_A
    ri = lax.broadcasted_iota(jnp.int32, (C, C), 0)
    ci = lax.broadcasted_iota(jnp.int32, (C, C), 1)
    tri_incl = (ci <= ri).astype(BF16)
    i2 = lax.broadcasted_iota(jnp.int32, (2 * C, 2 * C), 0)
    j2 = lax.broadcasted_iota(jnp.int32, (2 * C, 2 * C), 1)
    same_head = (i2 >= C) == (j2 >= C)
    t_i = i2 & (C - 1)
    t_j = j2 & (C - 1)
    strict = same_head & (t_j < t_i)
    incl = same_head & (t_j <= t_i)
    eye = (i2 == j2).astype(F32)
    zero = jnp.zeros((2 * C, 2 * C), F32)

    def stack(x):
        z = jnp.zeros_like(x)
        return jnp.concatenate([jnp.where(head0, x, z), jnp.where(head0, z, x)], axis=0).astype(BF16)

    def one_sequence(bb):
        za = za_ref[bb]
        row = lax.broadcasted_iota(jnp.int32, za.shape, 0)
        prev = jnp.where(row == 0, prev_scr[bb], pltpu.roll(za, 1, axis=0))
        prev_scr[bb] = za_ref[bb, C - 1:C, :]
        zs_scr[...] = za + (prev - za) * mix_ref[...]

        zwx = zs_scr[:, 4 * W_A:4 * W_A + LANES]
        lora_in = jnp.where(head0, jnp.tanh(zwx), zwx)
        lora_scr[...] = _dot(lora_in.astype(BF16), wwa_ref[...])

        for p in range(N_PAIRS):
            o = p * LANES
            sl = slice(o, o + LANES)
            r = zs_scr[:, sl]
            k = zs_scr[:, W_A + o:W_A + o + LANES]
            v = zs_scr[:, 2 * W_A + o:2 * W_A + o + LANES]
            g = zs_scr[:, 3 * W_A + o:3 * W_A + o + LANES]
            w_logit = w0_ref[:, sl] + lora_scr[:, sl]
            logw = -EXP_NEG_HALF * _sigmoid(w_logit)
            a = _sigmoid(a0_ref[:, sl] + lora_scr[:, W_A + o:W_A + o + LANES])

            l_hi = logw.astype(BF16)
            rem = logw - l_hi.astype(F32)
            l_mid = rem.astype(BF16)
            l_lo = (rem - l_mid.astype(F32)).astype(BF16)
            cum = _dot(tri_incl, l_hi) + _dot(tri_incl, l_mid) + _dot(tri_incl, l_lo)
            cum_end = cum[C - 1:C, :]

            kkr = k * kk_ref[:, sl]
            norm = jnp.sqrt(_split_dot(kkr * kkr, ones_bd))
            kk = kkr / jnp.maximum(norm, 1e-12)
            kp = k * (1.0 + (a - 1.0) * ka_ref[:, sl])
            b = kk * a

            inv_p = jnp.exp(-cum)
            p_end = jnp.exp(cum_end - cum)
            a_s = stack(-kk * jnp.exp(cum - logw))
            r_s = stack(r * jnp.exp(cum))
            b_s = stack(b * inv_p)
            k_s = stack(kp * inv_p)
            bh_s = stack(b * p_end)
            kh_s = stack(kp * p_end)
            v_s = stack(v)

            n_ab = jnp.where(strict, _dot_nt(a_s, b_s), zero)
            l_ak = jnp.where(strict, _dot_nt(a_s, k_s), zero).astype(BF16)
            m_rb = jnp.where(incl, _dot_nt(r_s, b_s), zero).astype(BF16)
            m_rk = jnp.where(incl, _dot_nt(r_s, k_s), zero).astype(BF16)

            t_inv = eye + n_ab
            n_pow = n_ab
            for _ in range(int(math.log2(C)) - 1):
                n_pow_b = n_pow.astype(BF16)
                n_pow = _dot(n_pow_b, n_pow_b)
                t_inv = t_inv + _dot(t_inv.astype(BF16), n_pow.astype(BF16))

            s_old = state_scr[bb, p]
            s_old_b = s_old.astype(BF16)
            y = _dot_nt(a_s, s_old_b) + _dot(l_ak, v_s)
            u_b = _dot(t_inv.astype(BF16), y.astype(BF16)).astype(BF16)
            o_s = _dot_nt(r_s, s_old_b) + _dot(m_rb, u_b) + _dot(m_rk, v_s)
            out = o_s[:C] + o_s[C:]
            state_scr[bb, p] = s_old * jnp.exp(cum_end) + _dot_tn(u_b, bh_s) + _dot_tn(v_s, kh_s)

            inv_n = 1.0 / HEAD_A
            mu = _split_dot(out, ones_bd) * inv_n
            d = out - mu
            var = _split_dot(d * d, ones_bd) * inv_n
            o_n = d * lax.rsqrt(var + GN_EPS_A) * gng_ref[:, sl] + gnb_ref[:, sl]
            bonus = _split_dot(r * kp * rk_ref[:, sl], ones_bd) * v
            ya_ref[bb, :, sl] = ((o_n + bonus) * (g * _sigmoid(g))).astype(ya_ref.dtype)

    _batch_loop(bblk, one_sequence)

    @pl.when(c == n_chunks - 1)
    def _():
        s_out_ref[...] = state_scr[...]


def _rwkv(za, shift, s_pair, wts, batch, seq, chunk, bblk):
    n_chunks = seq // chunk
    assert seq % chunk == 0 and batch % bblk == 0
    const2 = lambda b, c: (0, 0)
    kern = functools.partial(_rwkv_kernel, chunk=chunk, n_chunks=n_chunks, bblk=bblk)
    vec = pl.BlockSpec((1, W_A), const2)
    state_spec = pl.BlockSpec((bblk, N_PAIRS, LANES, LANES), lambda b, c: (b, 0, 0, 0))
    return pl.pallas_call(
        kern,
        grid=(batch // bblk, n_chunks),
        in_specs=[
            pl.BlockSpec((bblk, chunk, N_A_COLS), lambda b, c: (b, c, 0)),
            pl.BlockSpec((bblk, 1, N_A_COLS), lambda b, c: (b, 0, 0)),
            state_spec,
            pl.BlockSpec((1, N_A_COLS), const2),
            pl.BlockSpec((LANES, 2 * W_A), const2),
            vec, vec, vec, vec, vec, vec, vec,
            pl.BlockSpec((LANES, LANES), const2),
        ],
        out_specs=[
            pl.BlockSpec((bblk, chunk, W_A), lambda b, c: (b, c, 0)),
            state_spec,
        ],
        out_shape=[
            jax.ShapeDtypeStruct((batch, seq, W_A), BF16),
            jax.ShapeDtypeStruct((batch, N_PAIRS, LANES, LANES), F32),
        ],
        scratch_shapes=[
            pltpu.VMEM((chunk, N_A_COLS), F32),
            pltpu.VMEM((chunk, 2 * W_A), F32),
            pltpu.VMEM((bblk, 1, N_A_COLS), F32),
            pltpu.VMEM((bblk, N_PAIRS, LANES, LANES), F32),
        ],
        compiler_params=pltpu.CompilerParams(
            dimension_semantics=("arbitrary", "arbitrary"), vmem_limit_bytes=VMEM_LIMIT_BYTES),
        name="rwkv7_mix",
    )(za.reshape(batch, seq, N_A_COLS), shift, s_pair, wts["shift_mix"], wts["wwa"], wts["w0"], wts["a0"],
      wts["k_k"], wts["k_a"], wts["r_k"], wts["gn_a_g"], wts["gn_a_b"], wts["ones_bd"])


def _ret_kernel(lg_ref, zb_ref, cos_ref, sin_ref, s0_ref, gng_ref, gnb_ref,
                yb_ref, s_out_ref, state_scr, *, chunk, n_chunks, bblk):
    C = chunk
    c = pl.program_id(1)

    @pl.when(c == 0)
    def _():
        state_scr[...] = s0_ref[...]

    cos = cos_ref[...]
    sin = sin_ref[...]
    rowf = lax.broadcasted_iota(jnp.int32, (C, LANES), 0).astype(F32)
    diff = (lax.broadcasted_iota(jnp.int32, (C, C), 0) - lax.broadcasted_iota(jnp.int32, (C, C), 1))
    causal = diff >= 0
    diff_f = jnp.maximum(diff, 0).astype(F32)
    ones_row = jnp.ones((1, LANES), F32)

    def one_sequence(bb):
        for h in range(H_B):
            lg = lg_ref[h]
            sl = slice(h * DK_B, (h + 1) * DK_B)
            q = zb_ref[bb, :, sl]
            k = zb_ref[bb, :, W_B + h * DK_B:W_B + (h + 1) * DK_B]
            v = zb_ref[bb, :, 2 * W_B + h * DK_B:2 * W_B + (h + 1) * DK_B].astype(BF16)
            g = zb_ref[bb, :, 3 * W_B + h * DK_B:3 * W_B + (h + 1) * DK_B]
            qr = q * cos + pltpu.roll(q, DK_B // 2, axis=1) * sin
            kr = (k * cos + pltpu.roll(k, DK_B // 2, axis=1) * sin) * (DK_B ** -0.5)
            dmask = jnp.where(causal, jnp.exp(diff_f * lg), 0.0)
            scores = (_dot_nt(qr.astype(BF16), kr.astype(BF16)) * dmask).astype(BF16)
            s_old = state_scr[bb, h]
            q_dec = (qr * jnp.exp((rowf + 1.0) * lg)).astype(BF16)
            k_dec = (kr * jnp.exp((C - 1.0 - rowf) * lg)).astype(BF16)
            out = _dot(scores, v) + _dot(q_dec, s_old.astype(BF16))
            state_scr[bb, h] = s_old * jnp.exp(ones_row * (C * lg)) + _dot_tn(k_dec, v)
            mu = jnp.mean(out, axis=-1, keepdims=True)
            d = out - mu
            var = jnp.mean(d * d, axis=-1, keepdims=True)
            o_n = d * lax.rsqrt(var + GN_EPS_B) * gng_ref[:, sl] + gnb_ref[:, sl]
            yb_ref[bb, :, sl] = (o_n * (g * _sigmoid(g))).astype(yb_ref.dtype)

    _batch_loop(bblk, one_sequence)

    @pl.when(c == n_chunks - 1)
    def _():
        s_out_ref[...] = state_scr[...]


def _retention(zb, cos, sin, s0, log_g, gn_g, gn_b, batch, seq, chunk, bblk):
    n_chunks = seq // chunk
    assert seq % chunk == 0 and batch % bblk == 0
    kern = functools.partial(_ret_kernel, chunk=chunk, n_chunks=n_chunks, bblk=bblk)
    state_spec = pl.BlockSpec((bblk, H_B, DK_B, DK_B), lambda b, c: (b, 0, 0, 0))
    return pl.pallas_call(
        kern,
        grid=(batch // bblk, n_chunks),
        in_specs=[
            pl.BlockSpec(memory_space=pltpu.SMEM),
            pl.BlockSpec((bblk, chunk, N_B_COLS), lambda b, c: (b, c, 0)),
            pl.BlockSpec((chunk, DK_B), lambda b, c: (c, 0)),
            pl.BlockSpec((chunk, DK_B), lambda b, c: (c, 0)),
            state_spec,
            pl.BlockSpec((1, W_B), lambda b, c: (0, 0)),
            pl.BlockSpec((1, W_B), lambda b, c: (0, 0)),
        ],
        out_specs=[
            pl.BlockSpec((bblk, chunk, W_B), lambda b, c: (b, c, 0)),
            state_spec,
        ],
        out_shape=[
            jax.ShapeDtypeStruct((batch, seq, W_B), BF16),
            jax.ShapeDtypeStruct((batch, H_B, DK_B, DK_B), F32),
        ],
        scratch_shapes=[pltpu.VMEM((bblk, H_B, DK_B, DK_B), F32)],
        compiler_params=pltpu.CompilerParams(
            dimension_semantics=("arbitrary", "arbitrary"), vmem_limit_bytes=VMEM_LIMIT_BYTES),
        name="retention",
    )(log_g, zb.reshape(batch, seq, N_B_COLS), cos, sin, s0, gn_g, gn_b)


def _out_kernel(ya_ref, yb_ref, w_ref, x_ref, g_ref, b_ref, y_ref):
    h = _dot(ya_ref[...], w_ref[0:W_A, :]) + _dot(yb_ref[...], w_ref[W_A:W_A + W_B, :])
    t = ALPHA * x_ref[...] + h
    mu = jnp.mean(t, axis=-1, keepdims=True)
    d = t - mu
    var = jnp.mean(d * d, axis=-1, keepdims=True)
    y_ref[...] = d * lax.rsqrt(var + LN_EPS) * g_ref[...] + b_ref[...]


def _out_proj(ya, yb, w_out, x, ln_g, ln_b, tm):
    m = x.shape[0]
    assert m % tm == 0
    rows = lambda i: (i, 0)
    const = lambda i: (0, 0)
    return pl.pallas_call(
        _out_kernel,
        grid=(m // tm,),
        in_specs=[
            pl.BlockSpec((tm, W_A), rows),
            pl.BlockSpec((tm, W_B), rows),
            pl.BlockSpec((W_A + W_B, D_MODEL), const),
            pl.BlockSpec((tm, D_MODEL), rows),
            pl.BlockSpec((1, D_MODEL), const),
            pl.BlockSpec((1, D_MODEL), const),
        ],
        out_specs=pl.BlockSpec((tm, D_MODEL), rows),
        out_shape=jax.ShapeDtypeStruct((m, D_MODEL), F32),
        compiler_params=pltpu.CompilerParams(
            dimension_semantics=("arbitrary",), vmem_limit_bytes=VMEM_LIMIT_BYTES),
        name="out_proj_ln",
    )(ya, yb, w_out, x, ln_g, ln_b)


def _to_pairs(s):
    b = s.shape[0]
    s = s.reshape(b, N_PAIRS, 2, HEAD_A, HEAD_A)
    z = jnp.zeros_like(s[:, :, 0])
    top = jnp.concatenate([s[:, :, 0], z], axis=-1)
    bot = jnp.concatenate([z, s[:, :, 1]], axis=-1)
    return jnp.concatenate([top, bot], axis=-2)


def _from_pairs(sp):
    b = sp.shape[0]
    return jnp.stack([sp[:, :, :HEAD_A, :HEAD_A], sp[:, :, HEAD_A:, HEAD_A:]], axis=2).reshape(
        b, H_A, HEAD_A, HEAD_A)


def _rope_tables(pos):
    half = DK_B // 2
    inv = ROPE_BASE ** (-jnp.arange(half, dtype=F32) / half)
    ang = pos.astype(F32)[:, None] * inv[None, :]
    cos = jnp.cos(ang)
    sin = jnp.sin(ang)
    return jnp.concatenate([cos, cos], axis=-1), jnp.concatenate([-sin, sin], axis=-1)


def _tile(m, candidates):
    for t in candidates:
        if m % t == 0:
            return t
    return m


def _trunk(x2d, batch, seq, pos0, s_rwkv, s_shift, s_ret, wts, c_rwkv, c_ret, bblk, need_y):
    m = batch * seq
    xb = x2d.astype(BF16)
    tm = _tile(m, (1024, 512, 256))
    za = _matmul(xb, wts["w_in_a"], tm, 1408)
    zb = _matmul(xb, wts["w_in_b"], tm, 1024)
    ya, sa_pair = _rwkv(za, s_shift.reshape(batch, 1, N_A_COLS), _to_pairs(s_rwkv), wts,
                        batch, seq, c_rwkv, bblk)
    cos, sin = _rope_tables(pos0 + jnp.arange(seq, dtype=jnp.int32))
    yb, sb = _retention(zb, cos, sin, s_ret, wts["log_g"], wts["gn_b_g"], wts["gn_b_b"],
                        batch, seq, c_ret, bblk)
    shift = za.reshape(batch, seq, N_A_COLS)[:, -1]
    y = None
    if need_y:
        y = _out_proj(ya.reshape(m, W_A), yb.reshape(m, W_B), wts["w_out"], x2d, wts["ln_g"], wts["ln_b"],
                      _tile(m, (512, 256)))
    return y, _from_pairs(sa_pair), shift, sb


def _weights(w_in, w_out, shift_mix, w0, w_up, a0, a_up, k_k, k_a, r_k, gn_a_g, gn_a_b, gn_b_g, gn_b_b,
             ln_g, ln_b):
    zw = jnp.zeros((LORA_W, W_A), F32)
    wwa = jnp.concatenate([jnp.concatenate([w_up[0], zw], axis=1),
                           jnp.concatenate([zw, a_up[0]], axis=1)], axis=0).astype(BF16)
    lane = jnp.arange(LANES)
    return {
        "w_in_a": w_in[0, :, :N_A_COLS].astype(BF16),
        "w_in_b": w_in[0, :, N_A_COLS:].astype(BF16),
        "w_out": w_out[0].astype(BF16),
        "shift_mix": shift_mix,
        "wwa": wwa,
        "w0": w0, "a0": a0, "k_k": k_k, "k_a": k_a, "r_k": r_k.reshape(1, W_A),
        "gn_a_g": gn_a_g, "gn_a_b": gn_a_b, "gn_b_g": gn_b_g, "gn_b_b": gn_b_b,
        "ln_g": ln_g, "ln_b": ln_b,
        "ones_bd": (lane[:, None] // HEAD_A == lane[None, :] // HEAD_A).astype(BF16),
        "log_g": jnp.log1p(-jnp.exp2(-5.0 - jnp.arange(H_B, dtype=F32))),
    }


def kernel(x_prompt, x_sample, state_rwkv, state_shift, state_ret, meta_tokens, w_in, w_out, shift_mix,
           w0, w_up, a0, a_up, k_k, k_a, r_k, gn_a_g, gn_a_b, gn_b_g, gn_b_b, ln_g, ln_b):
    bp, tp, _ = x_prompt.shape
    bs, ts, _ = x_sample.shape
    wts = _weights(w_in, w_out, shift_mix, w0, w_up, a0, a_up, k_k, k_a, r_k, gn_a_g, gn_a_b, gn_b_g,
                   gn_b_b, ln_g, ln_b)
    _, ra_m, sh_m, rb_m = _trunk(
        meta_tokens.astype(F32), 1, N_META, 0,
        jnp.zeros((1, H_A, HEAD_A, HEAD_A), F32), jnp.zeros((1, N_A_COLS), F32),
        jnp.zeros((1, H_B, DK_B, DK_B), F32), wts, N_META, N_META, 1, False)
    rep = lambda s: jnp.broadcast_to(s, (bp,) + s.shape[1:])
    y_p, ra_p, sh_p, rb_p = _trunk(
        x_prompt.reshape(bp * tp, D_MODEL), bp, tp, N_META, rep(ra_m), rep(sh_m), rep(rb_m),
        wts, 64, 128, 1, True)
    y_s, ra_s, sh_s, rb_s = _trunk(
        x_sample.reshape(bs * ts, D_MODEL), bs, ts, PAST_LEN, state_rwkv[0], state_shift[0], state_ret[0],
        wts, ts, ts, 8, True)
    return (y_p.reshape(bp, tp, D_MODEL), y_s.reshape(bs, ts, D_MODEL),
            ra_p[None], sh_p[None], rb_p[None], ra_s[None], sh_s[None], rb_s[None])
```

```python
import functools
import math

import jax
import jax.numpy as jnp
from jax import lax
from jax.experimental import pallas as pl
from jax.experimental.pallas import tpu as pltpu

D_MODEL = 2048
N_META = 16
W_A = 1024
HEAD_A = 64
H_A = W_A // HEAD_A
LORA_W = 64
LORA_A = 64
W_B = D_MODEL - W_A
H_B = 8
DK_B = W_B // H_B
N_A_COLS = 4 * W_A + LORA_W + LORA_A
N_B_COLS = 4 * W_B
DEPTH = 1
PAST_LEN = 16384
ALPHA = (2.0 * DEPTH) ** 0.25
ROPE_BASE = 10000.0
LN_EPS = 1e-5
GN_EPS_A = 64e-5
GN_EPS_B = 1e-5
EXP_NEG_HALF = math.exp(-0.5)

LANES = 128
N_PAIRS = W_A // LANES
VMEM_LIMIT_BYTES = 56 * 1024 * 1024

F32 = jnp.float32
BF16 = jnp.bfloat16


def _dot(a, b):
    return jnp.dot(a, b, preferred_element_type=F32)


def _dot_nt(a, b):
    return lax.dot_general(a, b, (((1,), (1,)), ((), ())), preferred_element_type=F32)


def _dot_tn(a, b):
    return lax.dot_general(a, b, (((0,), (0,)), ((), ())), preferred_element_type=F32)


def _split_dot(x, w_bf16):
    hi = x.astype(BF16)
    lo = (x - hi.astype(F32)).astype(BF16)
    return _dot(hi, w_bf16) + _dot(lo, w_bf16)


def _sigmoid(x):
    return 1.0 / (1.0 + jnp.exp(-x))


def _batch_loop(n, body):
    if n == 1:
        body(0)
    else:
        lax.fori_loop(0, n, lambda i, carry: (body(i), carry)[1], 0)


def _matmul_kernel(x_ref, w_ref, o_ref):
    o_ref[...] = _dot(x_ref[...], w_ref[...])


def _matmul(x, w, tm, tn):
    m, k = x.shape
    n = w.shape[1]
    assert m % tm == 0 and n % tn == 0
    return pl.pallas_call(
        _matmul_kernel,
        grid=(m // tm, n // tn),
        in_specs=[pl.BlockSpec((tm, k), lambda i, j: (i, 0)),
                  pl.BlockSpec((k, tn), lambda i, j: (0, j))],
        out_specs=pl.BlockSpec((tm, tn), lambda i, j: (i, j)),
        out_shape=jax.ShapeDtypeStruct((m, n), F32),
        compiler_params=pltpu.CompilerParams(
            dimension_semantics=("arbitrary", "arbitrary"), vmem_limit_bytes=VMEM_LIMIT_BYTES),
        name="in_proj",
    )(x, w)


def _rwkv_kernel(za_ref, shift_ref, s0_ref, mix_ref, wwa_ref, w0_ref, a0_ref, kk_ref, ka_ref,
                 rk_ref, gng_ref, gnb_ref, ones_ref,
                 ya_ref, s_out_ref,
                 zs_scr, prev_scr, state_scr, *, chunk, n_chunks, bblk):
    C = chunk
    c = pl.program_id(1)

    @pl.when(c == 0)
    def _():
        prev_scr[...] = shift_ref[...]
        state_scr[...] = s0_ref[...]

    ones_bd = ones_ref[...]
    lane = lax.broadcasted_iota(jnp.int32, (C, LANES), 1)
    head0 = lane < HEAD_A
    ri = lax.broadcasted_iota(jnp.int32, (C, C), 0)
    ci = lax.broadcasted_iota(jnp.int32, (C, C), 1)
    tri_incl = (ci <= ri).astype(BF16)
    i2 = lax.broadcasted_iota(jnp.int32, (2 * C, 2 * C), 0)
    j2 = lax.broadcasted_iota(jnp.int32, (2 * C, 2 * C), 1)
    same_head = (i2 >= C) == (j2 >= C)
    t_i = i2 & (C - 1)
    t_j = j2 & (C - 1)
    strict = same_head & (t_j < t_i)
    incl = same_head & (t_j <= t_i)
    eye = (i2 == j2).astype(F32)
    zero = jnp.zeros((2 * C, 2 * C), F32)

    pairs = range(N_PAIRS)
    pair_slices = [slice(p * LANES, (p + 1) * LANES) for p in pairs]

    def stack(x):
        z = jnp.zeros_like(x)
        return jnp.concatenate([jnp.where(head0, x, z), jnp.where(head0, z, x)], axis=0).astype(BF16)

    def seg_sum(x):
        return jnp.concatenate([_split_dot(x[:, s], ones_bd) for s in pair_slices], axis=1)

    def one_sequence(bb):
        za = za_ref[bb]
        row = lax.broadcasted_iota(jnp.int32, za.shape, 0)
        prev = jnp.where(row == 0, prev_scr[bb], pltpu.roll(za, 1, axis=0))
        prev_scr[bb] = za_ref[bb, C - 1:C, :]
        zs_scr[...] = za + (prev - za) * mix_ref[...]

        zwx = zs_scr[:, 4 * W_A:4 * W_A + LANES]
        lora_in = jnp.where(head0, jnp.tanh(zwx), zwx)
        lora = _dot(lora_in.astype(BF16), wwa_ref[...])

        r = zs_scr[:, 0:W_A]
        k = zs_scr[:, W_A:2 * W_A]
        v = zs_scr[:, 2 * W_A:3 * W_A]
        g = zs_scr[:, 3 * W_A:4 * W_A]
        logw = -EXP_NEG_HALF * _sigmoid(w0_ref[...] + lora[:, 0:W_A])
        a = _sigmoid(a0_ref[...] + lora[:, W_A:2 * W_A])
        l_hi = logw.astype(BF16)
        rem = logw - l_hi.astype(F32)
        l_mid = rem.astype(BF16)
        l_lo = (rem - l_mid.astype(F32)).astype(BF16)
        cum = _dot(tri_incl, l_hi) + _dot(tri_incl, l_mid) + _dot(tri_incl, l_lo)
        cum_end = cum[C - 1:C, :]
        kkr = k * kk_ref[...]
        kk = kkr / jnp.maximum(jnp.sqrt(seg_sum(kkr * kkr)), 1e-12)
        kp = k * (1.0 + (a - 1.0) * ka_ref[...])
        b = kk * a
        inv_p = jnp.exp(-cum)
        p_end = jnp.exp(cum_end - cum)
        decay_end = jnp.exp(cum_end)
        at = -kk * jnp.exp(cum - logw)
        rt = r * jnp.exp(cum)
        bt = b * inv_p
        kt = kp * inv_p
        bh = b * p_end
        kh = kp * p_end

        a_s = [stack(at[:, s]) for s in pair_slices]
        r_s = [stack(rt[:, s]) for s in pair_slices]
        b_s = [stack(bt[:, s]) for s in pair_slices]
        k_s = [stack(kt[:, s]) for s in pair_slices]
        bh_s = [stack(bh[:, s]) for s in pair_slices]
        kh_s = [stack(kh[:, s]) for s in pair_slices]
        v_s = [stack(v[:, s]) for s in pair_slices]
        n_pow = [jnp.where(strict, _dot_nt(a_s[p], b_s[p]), zero) for p in pairs]
        l_ak = [jnp.where(strict, _dot_nt(a_s[p], k_s[p]), zero).astype(BF16) for p in pairs]
        m_rb = [jnp.where(incl, _dot_nt(r_s[p], b_s[p]), zero).astype(BF16) for p in pairs]
        m_rk = [jnp.where(incl, _dot_nt(r_s[p], k_s[p]), zero).astype(BF16) for p in pairs]
        t_inv = [eye + n_pow[p] for p in pairs]
        for _ in range(int(math.log2(C)) - 1):
            n_pow_b = [n.astype(BF16) for n in n_pow]
            n_pow = [_dot(n, n) for n in n_pow_b]
            t_inv = [t_inv[p] + _dot(t_inv[p].astype(BF16), n_pow[p].astype(BF16)) for p in pairs]
        s_old = [state_scr[bb, p] for p in pairs]
        s_old_b = [s.astype(BF16) for s in s_old]
        y = [_dot_nt(a_s[p], s_old_b[p]) + _dot(l_ak[p], v_s[p]) for p in pairs]
        u_b = [_dot(t_inv[p].astype(BF16), y[p].astype(BF16)).astype(BF16) for p in pairs]
        o_s = [_dot_nt(r_s[p], s_old_b[p]) + _dot(m_rb[p], u_b[p]) + _dot(m_rk[p], v_s[p]) for p in pairs]
        for p in pairs:
            state_scr[bb, p] = (s_old[p] * decay_end[:, pair_slices[p]]
                                + _dot_tn(u_b[p], bh_s[p]) + _dot_tn(v_s[p], kh_s[p]))
        out = jnp.concatenate([o[:C] + o[C:] for o in o_s], axis=1)

        inv_n = 1.0 / HEAD_A
        d = out - seg_sum(out) * inv_n
        var = seg_sum(d * d) * inv_n
        o_n = d * lax.rsqrt(var + GN_EPS_A) * gng_ref[...] + gnb_ref[...]
        bonus = seg_sum(r * kp * rk_ref[...]) * v
        ya_ref[bb] = ((o_n + bonus) * (g * _sigmoid(g))).astype(ya_ref.dtype)

    _batch_loop(bblk, one_sequence)

    @pl.when(c == n_chunks - 1)
    def _():
        s_out_ref[...] = state_scr[...]


def _rwkv(za, shift, s_pair, wts, batch, seq, chunk, bblk):
    n_chunks = seq // chunk
    assert seq % chunk == 0 and batch % bblk == 0
    const2 = lambda b, c: (0, 0)
    kern = functools.partial(_rwkv_kernel, chunk=chunk, n_chunks=n_chunks, bblk=bblk)
    vec = pl.BlockSpec((1, W_A), const2)
    state_spec = pl.BlockSpec((bblk, N_PAIRS, LANES, LANES), lambda b, c: (b, 0, 0, 0))
    return pl.pallas_call(
        kern,
        grid=(batch // bblk, n_chunks),
        in_specs=[
            pl.BlockSpec((bblk, chunk, N_A_COLS), lambda b, c: (b, c, 0)),
            pl.BlockSpec((bblk, 1, N_A_COLS), lambda b, c: (b, 0, 0)),
            state_spec,
            pl.BlockSpec((1, N_A_COLS), const2),
            pl.BlockSpec((LANES, 2 * W_A), const2),
            vec, vec, vec, vec, vec, vec, vec,
            pl.BlockSpec((LANES, LANES), const2),
        ],
        out_specs=[
            pl.BlockSpec((bblk, chunk, W_A), lambda b, c: (b, c, 0)),
            state_spec,
        ],
        out_shape=[
            jax.ShapeDtypeStruct((batch, seq, W_A), BF16),
            jax.ShapeDtypeStruct((batch, N_PAIRS, LANES, LANES), F32),
        ],
        scratch_shapes=[
            pltpu.VMEM((chunk, N_A_COLS), F32),
            pltpu.VMEM((bblk, 1, N_A_COLS), F32),
            pltpu.VMEM((bblk, N_PAIRS, LANES, LANES), F32),
        ],
        compiler_params=pltpu.CompilerParams(
            dimension_semantics=("arbitrary", "arbitrary"), vmem_limit_bytes=VMEM_LIMIT_BYTES),
        name="rwkv7_mix",
    )(za.reshape(batch, seq, N_A_COLS), shift, s_pair, wts["shift_mix"], wts["wwa"], wts["w0"], wts["a0"],
      wts["k_k"], wts["k_a"], wts["r_k"], wts["gn_a_g"], wts["gn_a_b"], wts["ones_bd"])


def _ret_kernel(lg_ref, zb_ref, cos_ref, sin_ref, s0_ref, gng_ref, gnb_ref,
                yb_ref, s_out_ref, state_scr, *, chunk, n_chunks, bblk):
    C = chunk
    c = pl.program_id(1)

    @pl.when(c == 0)
    def _():
        state_scr[...] = s0_ref[...]

    cos = cos_ref[...]
    sin = sin_ref[...]
    rowf = lax.broadcasted_iota(jnp.int32, (C, LANES), 0).astype(F32)
    diff = (lax.broadcasted_iota(jnp.int32, (C, C), 0) - lax.broadcasted_iota(jnp.int32, (C, C), 1))
    causal = diff >= 0
    diff_f = jnp.maximum(diff, 0).astype(F32)
    ones_row = jnp.ones((1, LANES), F32)

    def one_sequence(bb):
        for h in range(H_B):
            lg = lg_ref[h]
            sl = slice(h * DK_B, (h + 1) * DK_B)
            q = zb_ref[bb, :, sl]
            k = zb_ref[bb, :, W_B + h * DK_B:W_B + (h + 1) * DK_B]
            v = zb_ref[bb, :, 2 * W_B + h * DK_B:2 * W_B + (h + 1) * DK_B].astype(BF16)
            g = zb_ref[bb, :, 3 * W_B + h * DK_B:3 * W_B + (h + 1) * DK_B]
            qr = q * cos + pltpu.roll(q, DK_B // 2, axis=1) * sin
            kr = (k * cos + pltpu.roll(k, DK_B // 2, axis=1) * sin) * (DK_B ** -0.5)
            dmask = jnp.where(causal, jnp.exp(diff_f * lg), 0.0)
            scores = (_dot_nt(qr.astype(BF16), kr.astype(BF16)) * dmask).astype(BF16)
            s_old = state_scr[bb, h]
            q_dec = (qr * jnp.exp((rowf + 1.0) * lg)).astype(BF16)
            k_dec = (kr * jnp.exp((C - 1.0 - rowf) * lg)).astype(BF16)
            out = _dot(scores, v) + _dot(q_dec, s_old.astype(BF16))
            state_scr[bb, h] = s_old * jnp.exp(ones_row * (C * lg)) + _dot_tn(k_dec, v)
            mu = jnp.mean(out, axis=-1, keepdims=True)
            d = out - mu
            var = jnp.mean(d * d, axis=-1, keepdims=True)
            o_n = d * lax.rsqrt(var + GN_EPS_B) * gng_ref[:, sl] + gnb_ref[:, sl]
            yb_ref[bb, :, sl] = (o_n * (g * _sigmoid(g))).astype(yb_ref.dtype)

    _batch_loop(bblk, one_sequence)

    @pl.when(c == n_chunks - 1)
    def _():
        s_out_ref[...] = state_scr[...]


def _retention(zb, cos, sin, s0, log_g, gn_g, gn_b, batch, seq, chunk, bblk):
    n_chunks = seq // chunk
    assert seq % chunk == 0 and batch % bblk == 0
    kern = functools.partial(_ret_kernel, chunk=chunk, n_chunks=n_chunks, bblk=bblk)
    state_spec = pl.BlockSpec((bblk, H_B, DK_B, DK_B), lambda b, c: (b, 0, 0, 0))
    return pl.pallas_call(
        kern,
        grid=(batch // bblk, n_chunks),
        in_specs=[
            pl.BlockSpec(memory_space=pltpu.SMEM),
            pl.BlockSpec((bblk, chunk, N_B_COLS), lambda b, c: (b, c, 0)),
            pl.BlockSpec((chunk, DK_B), lambda b, c: (c, 0)),
            pl.BlockSpec((chunk, DK_B), lambda b, c: (c, 0)),
            state_spec,
            pl.BlockSpec((1, W_B), lambda b, c: (0, 0)),
            pl.BlockSpec((1, W_B), lambda b, c: (0, 0)),
        ],
        out_specs=[
            pl.BlockSpec((bblk, chunk, W_B), lambda b, c: (b, c, 0)),
            state_spec,
        ],
        out_shape=[
            jax.ShapeDtypeStruct((batch, seq, W_B), BF16),
            jax.ShapeDtypeStruct((batch, H_B, DK_B, DK_B), F32),
        ],
        scratch_shapes=[pltpu.VMEM((bblk, H_B, DK_B, DK_B), F32)],
        compiler_params=pltpu.CompilerParams(
            dimension_semantics=("arbitrary", "arbitrary"), vmem_limit_bytes=VMEM_LIMIT_BYTES),
        name="retention",
    )(log_g, zb.reshape(batch, seq, N_B_COLS), cos, sin, s0, gn_g, gn_b)


def _out_kernel(ya_ref, yb_ref, w_ref, x_ref, g_ref, b_ref, y_ref):
    h = _dot(ya_ref[...], w_ref[0:W_A, :]) + _dot(yb_ref[...], w_ref[W_A:W_A + W_B, :])
    t = ALPHA * x_ref[...] + h
    mu = jnp.mean(t, axis=-1, keepdims=True)
    d = t - mu
    var = jnp.mean(d * d, axis=-1, keepdims=True)
    y_ref[...] = d * lax.rsqrt(var + LN_EPS) * g_ref[...] + b_ref[...]


def _out_proj(ya, yb, w_out, x, ln_g, ln_b, tm):
    m = x.shape[0]
    assert m % tm == 0
    rows = lambda i: (i, 0)
    const = lambda i: (0, 0)
    return pl.pallas_call(
        _out_kernel,
        grid=(m // tm,),
        in_specs=[
            pl.BlockSpec((tm, W_A), rows),
            pl.BlockSpec((tm, W_B), rows),
            pl.BlockSpec((W_A + W_B, D_MODEL), const),
            pl.BlockSpec((tm, D_MODEL), rows),
            pl.BlockSpec((1, D_MODEL), const),
            pl.BlockSpec((1, D_MODEL), const),
        ],
        out_specs=pl.BlockSpec((tm, D_MODEL), rows),
        out_shape=jax.ShapeDtypeStruct((m, D_MODEL), F32),
        compiler_params=pltpu.CompilerParams(
            dimension_semantics=("arbitrary",), vmem_limit_bytes=VMEM_LIMIT_BYTES),
        name="out_proj_ln",
    )(ya, yb, w_out, x, ln_g, ln_b)


def _to_pairs(s):
    b = s.shape[0]
    s = s.reshape(b, N_PAIRS, 2, HEAD_A, HEAD_A)
    z = jnp.zeros_like(s[:, :, 0])
    top = jnp.concatenate([s[:, :, 0], z], axis=-1)
    bot = jnp.concatenate([z, s[:, :, 1]], axis=-1)
    return jnp.concatenate([top, bot], axis=-2)


def _from_pairs(sp):
    b = sp.shape[0]
    return jnp.stack([sp[:, :, :HEAD_A, :HEAD_A], sp[:, :, HEAD_A:, HEAD_A:]], axis=2).reshape(
        b, H_A, HEAD_A, HEAD_A)


def _rope_tables(pos):
    half = DK_B // 2
    inv = ROPE_BASE ** (-jnp.arange(half, dtype=F32) / half)
    ang = pos.astype(F32)[:, None] * inv[None, :]
    cos = jnp.cos(ang)
    sin = jnp.sin(ang)
    return jnp.concatenate([cos, cos], axis=-1), jnp.concatenate([-sin, sin], axis=-1)


def _tile(m, candidates):
    for t in candidates:
        if m % t == 0:
            return t
    return m


def _trunk(x2d, batch, seq, pos0, s_rwkv, s_shift, s_ret, wts, c_rwkv, c_ret, bblk, need_y):
    m = batch * seq
    xb = x2d.astype(BF16)
    tm = _tile(m, (1024, 512, 256))
    za = _matmul(xb, wts["w_in_a"], tm, 1408)
    zb = _matmul(xb, wts["w_in_b"], tm, 1024)
    ya, sa_pair = _rwkv(za, s_shift.reshape(batch, 1, N_A_COLS), _to_pairs(s_rwkv), wts,
                        batch, seq, c_rwkv, bblk)
    cos, sin = _rope_tables(pos0 + jnp.arange(seq, dtype=jnp.int32))
    yb, sb = _retention(zb, cos, sin, s_ret, wts["log_g"], wts["gn_b_g"], wts["gn_b_b"],
                        batch, seq, c_ret, bblk)
    shift = za.reshape(batch, seq, N_A_COLS)[:, -1]
    y = None
    if need_y:
        y = _out_proj(ya.reshape(m, W_A), yb.reshape(m, W_B), wts["w_out"], x2d, wts["ln_g"], wts["ln_b"],
                      _tile(m, (512, 256)))
    return y, _from_pairs(sa_pair), shift, sb


def _weights(w_in, w_out, shift_mix, w0, w_up, a0, a_up, k_k, k_a, r_k, gn_a_g, gn_a_b, gn_b_g, gn_b_b,
             ln_g, ln_b):
    zw = jnp.zeros((LORA_W, W_A), F32)
    wwa = jnp.concatenate([jnp.concatenate([w_up[0], zw], axis=1),
                           jnp.concatenate([zw, a_up[0]], axis=1)], axis=0).astype(BF16)
    lane = jnp.arange(LANES)
    return {
        "w_in_a": w_in[0, :, :N_A_COLS].astype(BF16),
        "w_in_b": w_in[0, :, N_A_COLS:].astype(BF16),
        "w_out": w_out[0].astype(BF16),
        "shift_mix": shift_mix,
        "wwa": wwa,
        "w0": w0, "a0": a0, "k_k": k_k, "k_a": k_a, "r_k": r_k.reshape(1, W_A),
        "gn_a_g": gn_a_g, "gn_a_b": gn_a_b, "gn_b_g": gn_b_g, "gn_b_b": gn_b_b,
        "ln_g": ln_g, "ln_b": ln_b,
        "ones_bd": (lane[:, None] // HEAD_A == lane[None, :] // HEAD_A).astype(BF16),
        "log_g": jnp.log1p(-jnp.exp2(-5.0 - jnp.arange(H_B, dtype=F32))),
    }


def kernel(x_prompt, x_sample, state_rwkv, state_shift, state_ret, meta_tokens, w_in, w_out, shift_mix,
           w0, w_up, a0, a_up, k_k, k_a, r_k, gn_a_g, gn_a_b, gn_b_g, gn_b_b, ln_g, ln_b):
    bp, tp, _ = x_prompt.shape
    bs, ts, _ = x_sample.shape
    wts = _weights(w_in, w_out, shift_mix, w0, w_up, a0, a_up, k_k, k_a, r_k, gn_a_g, gn_a_b, gn_b_g,
                   gn_b_b, ln_g, ln_b)
    _, ra_m, sh_m, rb_m = _trunk(
        meta_tokens.astype(F32), 1, N_META, 0,
        jnp.zeros((1, H_A, HEAD_A, HEAD_A), F32), jnp.zeros((1, N_A_COLS), F32),
        jnp.zeros((1, H_B, DK_B, DK_B), F32), wts, N_META, N_META, 1, False)
    rep = lambda s: jnp.broadcast_to(s, (bp,) + s.shape[1:])
    y_p, ra_p, sh_p, rb_p = _trunk(
        x_prompt.reshape(bp * tp, D_MODEL), bp, tp, N_META, rep(ra_m), rep(sh_m), rep(rb_m),
        wts, 64, 128, 1, True)
    y_s, ra_s, sh_s, rb_s = _trunk(
        x_sample.reshape(bs * ts, D_MODEL), bs, ts, PAST_LEN, state_rwkv[0], state_shift[0], state_ret[0],
        wts, ts, ts, 8, True)
    return (y_p.reshape(bp, tp, D_MODEL), y_s.reshape(bs, ts, D_MODEL),
            ra_p[None], sh_p[None], rb_p[None], ra_s[None], sh_s[None], rb_s[None])
```

```python
import functools
import math

import jax
import jax.numpy as jnp
from jax import lax
from jax.experimental import pallas as pl
from jax.experimental.pallas import tpu as pltpu

D_MODEL = 2048
N_META = 16
W_A = 1024
HEAD_A = 64
H_A = W_A // HEAD_A
LORA_W = 64
LORA_A = 64
W_B = D_MODEL - W_A
H_B = 8
DK_B = W_B // H_B
N_A_COLS = 4 * W_A + LORA_W + LORA_A
N_B_COLS = 4 * W_B
N_IN = N_A_COLS + N_B_COLS
IN_PROJ_TN = N_IN // 5
DEPTH = 1
PAST_LEN = 16384
ALPHA = (2.0 * DEPTH) ** 0.25
ROPE_BASE = 10000.0
LN_EPS = 1e-5
GN_EPS_A = 64e-5
GN_EPS_B = 1e-5
EXP_NEG_HALF = math.exp(-0.5)

LANES = 128
N_PAIRS = W_A // LANES
VMEM_LIMIT_BYTES = 56 * 1024 * 1024

F32 = jnp.float32
BF16 = jnp.bfloat16


def _dot(a, b):
    return jnp.dot(a, b, preferred_element_type=F32)


def _dot_nt(a, b):
    return lax.dot_general(a, b, (((1,), (1,)), ((), ())), preferred_element_type=F32)


def _dot_tn(a, b):
    return lax.dot_general(a, b, (((0,), (0,)), ((), ())), preferred_element_type=F32)


def _split_dot(x, w_bf16):
    hi = x.astype(BF16)
    lo = (x - hi.astype(F32)).astype(BF16)
    return _dot(hi, w_bf16) + _dot(lo, w_bf16)


def _sigmoid(x):
    return 1.0 / (1.0 + jnp.exp(-x))


def _batch_loop(n, body):
    if n == 1:
        body(0)
    else:
        lax.fori_loop(0, n, lambda i, carry: (body(i), carry)[1], 0)


def _matmul_kernel(x_ref, w_ref, o_ref):
    o_ref[...] = _dot(x_ref[...].astype(BF16), w_ref[...])


def _matmul(x, w, tm, tn):
    m, k = x.shape
    n = w.shape[1]
    assert m % tm == 0 and n % tn == 0
    return pl.pallas_call(
        _matmul_kernel,
        grid=(m // tm, n // tn),
        in_specs=[pl.BlockSpec((tm, k), lambda i, j: (i, 0)),
                  pl.BlockSpec((k, tn), lambda i, j: (0, j))],
        out_specs=pl.BlockSpec((tm, tn), lambda i, j: (i, j)),
        out_shape=jax.ShapeDtypeStruct((m, n), F32),
        compiler_params=pltpu.CompilerParams(
            dimension_semantics=("arbitrary", "arbitrary"), vmem_limit_bytes=VMEM_LIMIT_BYTES),
        name="in_proj",
    )(x, w)


def _rwkv_kernel(za_ref, shift_ref, s0_ref, mix_ref, wwa_ref, w0_ref, a0_ref, kk_ref, ka_ref,
                 rk_ref, gng_ref, gnb_ref, ones_ref,
                 ya_ref, s_out_ref,
                 zs_scr, prev_scr, state_scr, *, chunk, n_chunks, bblk):
    C = chunk
    c = pl.program_id(1)

    @pl.when(c == 0)
    def _():
        prev_scr[...] = shift_ref[...]
        state_scr[...] = jnp.zeros(state_scr.shape, F32)
        for p in range(N_PAIRS):
            state_scr[:, p, 0:HEAD_A, 0:HEAD_A] = s0_ref[:, 2 * p]
            state_scr[:, p, HEAD_A:LANES, HEAD_A:LANES] = s0_ref[:, 2 * p + 1]

    ones_bd = ones_ref[...]
    lane = lax.broadcasted_iota(jnp.int32, (C, LANES), 1)
    head0 = lane < HEAD_A
    ri = lax.broadcasted_iota(jnp.int32, (C, C), 0)
    ci = lax.broadcasted_iota(jnp.int32, (C, C), 1)
    tri_incl = (ci <= ri).astype(BF16)
    i2 = lax.broadcasted_iota(jnp.int32, (2 * C, 2 * C), 0)
    j2 = lax.broadcasted_iota(jnp.int32, (2 * C, 2 * C), 1)
    same_head = (i2 >= C) == (j2 >= C)
    t_i = i2 & (C - 1)
    t_j = j2 & (C - 1)
    strict = same_head & (t_j < t_i)
    incl = same_head & (t_j <= t_i)
    eye = (i2 == j2).astype(F32)
    zero = jnp.zeros((2 * C, 2 * C), F32)

    pairs = range(N_PAIRS)
    pair_slices = [slice(p * LANES, (p + 1) * LANES) for p in pairs]

    def stack(x):
        z = jnp.zeros_like(x)
        return jnp.concatenate([jnp.where(head0, x, z), jnp.where(head0, z, x)], axis=0).astype(BF16)

    def seg_sum(x):
        return jnp.concatenate([_split_dot(x[:, s], ones_bd) for s in pair_slices], axis=1)

    def one_sequence(bb):
        za = za_ref[bb]
        row = lax.broadcasted_iota(jnp.int32, za.shape, 0)
        prev = jnp.where(row == 0, prev_scr[bb], pltpu.roll(za, 1, axis=0))
        prev_scr[bb] = za_ref[bb, C - 1:C, :]
        zs_scr[...] = za + (prev - za) * mix_ref[...]

        zwx = zs_scr[:, 4 * W_A:4 * W_A + LANES]
        lora_in = jnp.where(head0, jnp.tanh(zwx), zwx)
        lora = _dot(lora_in.astype(BF16), wwa_ref[...])

        r = zs_scr[:, 0:W_A]
        k = zs_scr[:, W_A:2 * W_A]
        v = zs_scr[:, 2 * W_A:3 * W_A]
        g = zs_scr[:, 3 * W_A:4 * W_A]
        logw = -EXP_NEG_HALF * _sigmoid(w0_ref[...] + lora[:, 0:W_A])
        a = _sigmoid(a0_ref[...] + lora[:, W_A:2 * W_A])
        l_hi = logw.astype(BF16)
        rem = logw - l_hi.astype(F32)
        l_mid = rem.astype(BF16)
        l_lo = (rem - l_mid.astype(F32)).astype(BF16)
        cum = _dot(tri_incl, l_hi) + _dot(tri_incl, l_mid) + _dot(tri_incl, l_lo)
        cum_end = cum[C - 1:C, :]
        kkr = k * kk_ref[...]
        kk = kkr / jnp.maximum(jnp.sqrt(seg_sum(kkr * kkr)), 1e-12)
        kp = k * (1.0 + (a - 1.0) * ka_ref[...])
        b = kk * a
        inv_p = jnp.exp(-cum)
        p_end = jnp.exp(cum_end - cum)
        decay_end = jnp.exp(cum_end)
        at = -kk * jnp.exp(cum - logw)
        rt = r * jnp.exp(cum)
        bt = b * inv_p
        kt = kp * inv_p
        bh = b * p_end
        kh = kp * p_end

        a_s = [stack(at[:, s]) for s in pair_slices]
        r_s = [stack(rt[:, s]) for s in pair_slices]
        b_s = [stack(bt[:, s]) for s in pair_slices]
        k_s = [stack(kt[:, s]) for s in pair_slices]
        bh_s = [stack(bh[:, s]) for s in pair_slices]
        kh_s = [stack(kh[:, s]) for s in pair_slices]
        v_s = [stack(v[:, s]) for s in pair_slices]
        n_pow = [jnp.where(strict, _dot_nt(a_s[p], b_s[p]), zero) for p in pairs]
        l_ak = [jnp.where(strict, _dot_nt(a_s[p], k_s[p]), zero).astype(BF16) for p in pairs]
        m_rb = [jnp.where(incl, _dot_nt(r_s[p], b_s[p]), zero).astype(BF16) for p in pairs]
        m_rk = [jnp.where(incl, _dot_nt(r_s[p], k_s[p]), zero).astype(BF16) for p in pairs]
        t_inv = [eye + n_pow[p] for p in pairs]
        for _ in range(int(math.log2(C)) - 1):
            n_pow_b = [n.astype(BF16) for n in n_pow]
            n_pow = [_dot(n, n) for n in n_pow_b]
            t_inv = [t_inv[p] + _dot(t_inv[p].astype(BF16), n_pow[p].astype(BF16)) for p in pairs]
        s_old = [state_scr[bb, p] for p in pairs]
        s_old_b = [s.astype(BF16) for s in s_old]
        y = [_dot_nt(a_s[p], s_old_b[p]) + _dot(l_ak[p], v_s[p]) for p in pairs]
        u_b = [_dot(t_inv[p].astype(BF16), y[p].astype(BF16)).astype(BF16) for p in pairs]
        o_s = [_dot_nt(r_s[p], s_old_b[p]) + _dot(m_rb[p], u_b[p]) + _dot(m_rk[p], v_s[p]) for p in pairs]
        for p in pairs:
            state_scr[bb, p] = (s_old[p] * decay_end[:, pair_slices[p]]
                                + _dot_tn(u_b[p], bh_s[p]) + _dot_tn(v_s[p], kh_s[p]))
        out = jnp.concatenate([o[:C] + o[C:] for o in o_s], axis=1)

        inv_n = 1.0 / HEAD_A
        d = out - seg_sum(out) * inv_n
        var = seg_sum(d * d) * inv_n
        o_n = d * lax.rsqrt(var + GN_EPS_A) * gng_ref[...] + gnb_ref[...]
        bonus = seg_sum(r * kp * rk_ref[...]) * v
        ya_ref[bb] = ((o_n + bonus) * (g * _sigmoid(g))).astype(ya_ref.dtype)

    _batch_loop(bblk, one_sequence)

    @pl.when(c == n_chunks - 1)
    def _():
        for p in range(N_PAIRS):
            s_out_ref[:, 2 * p] = state_scr[:, p, 0:HEAD_A, 0:HEAD_A]
            s_out_ref[:, 2 * p + 1] = state_scr[:, p, HEAD_A:LANES, HEAD_A:LANES]


def _rwkv(z, shift, s0, wts, batch, seq, chunk, bblk):
    n_chunks = seq // chunk
    assert seq % chunk == 0 and batch % bblk == 0
    const2 = lambda b, c: (0, 0)
    kern = functools.partial(_rwkv_kernel, chunk=chunk, n_chunks=n_chunks, bblk=bblk)
    vec = pl.BlockSpec((1, W_A), const2)
    state_spec = pl.BlockSpec((bblk, H_A, HEAD_A, HEAD_A), lambda b, c: (b, 0, 0, 0))
    return pl.pallas_call(
        kern,
        grid=(batch // bblk, n_chunks),
        in_specs=[
            pl.BlockSpec((bblk, chunk, N_A_COLS), lambda b, c: (b, c, 0)),
            pl.BlockSpec((bblk, 1, N_A_COLS), lambda b, c: (b, 0, 0)),
            state_spec,
            pl.BlockSpec((1, N_A_COLS), const2),
            pl.BlockSpec((LANES, 2 * W_A), const2),
            vec, vec, vec, vec, vec, vec, vec,
            pl.BlockSpec((LANES, LANES), const2),
        ],
        out_specs=[
            pl.BlockSpec((bblk, chunk, W_A), lambda b, c: (b, c, 0)),
            state_spec,
        ],
        out_shape=[
            jax.ShapeDtypeStruct((batch, seq, W_A), BF16),
            jax.ShapeDtypeStruct((batch, H_A, HEAD_A, HEAD_A), F32),
        ],
        scratch_shapes=[
            pltpu.VMEM((chunk, N_A_COLS), F32),
            pltpu.VMEM((bblk, 1, N_A_COLS), F32),
            pltpu.VMEM((bblk, N_PAIRS, LANES, LANES), F32),
        ],
        compiler_params=pltpu.CompilerParams(
            dimension_semantics=("arbitrary", "arbitrary"), vmem_limit_bytes=VMEM_LIMIT_BYTES),
        name="rwkv7_mix",
    )(z, shift, s0, wts["shift_mix"], wts["wwa"], wts["w0"], wts["a0"],
      wts["k_k"], wts["k_a"], wts["r_k"], wts["gn_a_g"], wts["gn_a_b"], wts["ones_bd"])


def _ret_kernel(lg_ref, lgv_ref, zb_ref, cos_ref, sin_ref, s0_ref, gng_ref, gnb_ref,
                yb_ref, s_out_ref, state_scr, *, chunk, n_chunks, bblk):
    C = chunk
    c = pl.program_id(1)

    @pl.when(c == 0)
    def _():
        state_scr[...] = s0_ref[...]

    heads = range(H_B)
    head_slices = [slice(h * DK_B, (h + 1) * DK_B) for h in heads]
    cos = jnp.concatenate([cos_ref[...]] * H_B, axis=1)
    sin = jnp.concatenate([sin_ref[...]] * H_B, axis=1)
    lgv = lgv_ref[...]
    rowf = lax.broadcasted_iota(jnp.int32, (C, W_B), 0).astype(F32)
    q_decay = jnp.exp((rowf + 1.0) * lgv)
    k_decay = jnp.exp((C - 1.0 - rowf) * lgv) * (DK_B ** -0.5)
    diff = (lax.broadcasted_iota(jnp.int32, (C, C), 0) - lax.broadcasted_iota(jnp.int32, (C, C), 1))
    causal = diff >= 0
    diff_f = jnp.maximum(diff, 0).astype(F32)
    dmask = [jnp.where(causal, jnp.exp(diff_f * lg_ref[h]), 0.0) for h in heads]
    s_decay = jnp.exp(C * lgv)

    def rot_half(x):
        return jnp.concatenate([pltpu.roll(x[:, s], DK_B // 2, axis=1) for s in head_slices], axis=1)

    def one_sequence(bb):
        q = zb_ref[bb, :, 0:W_B]
        k = zb_ref[bb, :, W_B:2 * W_B]
        v = zb_ref[bb, :, 2 * W_B:3 * W_B].astype(BF16)
        g = zb_ref[bb, :, 3 * W_B:4 * W_B]
        qr = q * cos + rot_half(q) * sin
        kr = k * cos + rot_half(k) * sin
        qr_b = qr.astype(BF16)
        kr_b = (kr * (DK_B ** -0.5)).astype(BF16)
        q_dec = (qr * q_decay).astype(BF16)
        k_dec = (kr * k_decay).astype(BF16)
        scores = [(_dot_nt(qr_b[:, s], kr_b[:, s]) * dmask[h]).astype(BF16)
                  for h, s in enumerate(head_slices)]
        s_old = [state_scr[bb, h] for h in heads]
        out = [_dot(scores[h], v[:, s]) + _dot(q_dec[:, s], s_old[h].astype(BF16))
               for h, s in enumerate(head_slices)]
        for h, s in enumerate(head_slices):
            state_scr[bb, h] = s_old[h] * s_decay[:, s] + _dot_tn(k_dec[:, s], v[:, s])
        d = [o - jnp.mean(o, axis=-1, keepdims=True) for o in out]
        rstd = [lax.rsqrt(jnp.mean(x * x, axis=-1, keepdims=True) + GN_EPS_B) for x in d]
        o_n = jnp.concatenate([d[h] * rstd[h] for h in heads], axis=1) * gng_ref[...] + gnb_ref[...]
        yb_ref[bb] = (o_n * (g * _sigmoid(g))).astype(yb_ref.dtype)

    _batch_loop(bblk, one_sequence)

    @pl.when(c == n_chunks - 1)
    def _():
        s_out_ref[...] = state_scr[...]


def _retention(z, cos, sin, s0, log_g, gn_g, gn_b, batch, seq, chunk, bblk):
    n_chunks = seq // chunk
    assert seq % chunk == 0 and batch % bblk == 0
    kern = functools.partial(_ret_kernel, chunk=chunk, n_chunks=n_chunks, bblk=bblk)
    state_spec = pl.BlockSpec((bblk, H_B, DK_B, DK_B), lambda b, c: (b, 0, 0, 0))
    return pl.pallas_call(
        kern,
        grid=(batch // bblk, n_chunks),
        in_specs=[
            pl.BlockSpec(memory_space=pltpu.SMEM),
            pl.BlockSpec((1, W_B), lambda b, c: (0, 0)),
            pl.BlockSpec((bblk, chunk, N_A_COLS), lambda b, c: (b, c, 1)),
            pl.BlockSpec((chunk, DK_B), lambda b, c: (c, 0)),
            pl.BlockSpec((chunk, DK_B), lambda b, c: (c, 0)),
            state_spec,
            pl.BlockSpec((1, W_B), lambda b, c: (0, 0)),
            pl.BlockSpec((1, W_B), lambda b, c: (0, 0)),
        ],
        out_specs=[
            pl.BlockSpec((bblk, chunk, W_B), lambda b, c: (b, c, 0)),
            state_spec,
        ],
        out_shape=[
            jax.ShapeDtypeStruct((batch, seq, W_B), BF16),
            jax.ShapeDtypeStruct((batch, H_B, DK_B, DK_B), F32),
        ],
        scratch_shapes=[pltpu.VMEM((bblk, H_B, DK_B, DK_B), F32)],
        compiler_params=pltpu.CompilerParams(
            dimension_semantics=("arbitrary", "arbitrary"), vmem_limit_bytes=VMEM_LIMIT_BYTES),
        name="retention",
    )(log_g, jnp.repeat(log_g, DK_B).reshape(1, W_B), z, cos, sin, s0, gn_g, gn_b)


def _out_kernel(ya_ref, yb_ref, w_ref, x_ref, g_ref, b_ref, y_ref):
    h = _dot(ya_ref[...], w_ref[0:W_A, :]) + _dot(yb_ref[...], w_ref[W_A:W_A + W_B, :])
    t = ALPHA * x_ref[...] + h
    mu = jnp.mean(t, axis=-1, keepdims=True)
    d = t - mu
    var = jnp.mean(d * d, axis=-1, keepdims=True)
    y_ref[...] = d * lax.rsqrt(var + LN_EPS) * g_ref[...] + b_ref[...]


def _out_proj(ya, yb, w_out, x, ln_g, ln_b, tm):
    m = x.shape[0]
    assert m % tm == 0
    rows = lambda i: (i, 0)
    const = lambda i: (0, 0)
    return pl.pallas_call(
        _out_kernel,
        grid=(m // tm,),
        in_specs=[
            pl.BlockSpec((tm, W_A), rows),
            pl.BlockSpec((tm, W_B), rows),
            pl.BlockSpec((W_A + W_B, D_MODEL), const),
            pl.BlockSpec((tm, D_MODEL), rows),
            pl.BlockSpec((1, D_MODEL), const),
            pl.BlockSpec((1, D_MODEL), const),
        ],
        out_specs=pl.BlockSpec((tm, D_MODEL), rows),
        out_shape=jax.ShapeDtypeStruct((m, D_MODEL), F32),
        compiler_params=pltpu.CompilerParams(
            dimension_semantics=("arbitrary",), vmem_limit_bytes=VMEM_LIMIT_BYTES),
        name="out_proj_ln",
    )(ya, yb, w_out, x, ln_g, ln_b)


def _rope_tables(pos):
    half = DK_B // 2
    inv = ROPE_BASE ** (-jnp.arange(half, dtype=F32) / half)
    ang = pos.astype(F32)[:, None] * inv[None, :]
    cos = jnp.cos(ang)
    sin = jnp.sin(ang)
    return jnp.concatenate([cos, cos], axis=-1), jnp.concatenate([-sin, sin], axis=-1)


def _tile(m, candidates):
    for t in candidates:
        if m % t == 0:
            return t
    return m


def _trunk(x2d, batch, seq, pos0, s_rwkv, s_shift, s_ret, wts, c_rwkv, c_ret, bblk, need_y):
    m = batch * seq
    z = _matmul(x2d, wts["w_in"], _tile(m, (1024, 512, 256)), IN_PROJ_TN).reshape(batch, seq, N_IN)
    ya, sa = _rwkv(z, s_shift.reshape(batch, 1, N_A_COLS), s_rwkv, wts,
                        batch, seq, c_rwkv, bblk)
    cos, sin = _rope_tables(pos0 + jnp.arange(seq, dtype=jnp.int32))
    yb, sb = _retention(z, cos, sin, s_ret, wts["log_g"], wts["gn_b_g"], wts["gn_b_b"],
                        batch, seq, c_ret, bblk)
    shift = z[:, -1, :N_A_COLS]
    y = None
    if need_y:
        y = _out_proj(ya.reshape(m, W_A), yb.reshape(m, W_B), wts["w_out"], x2d, wts["ln_g"], wts["ln_b"],
                      _tile(m, (512, 256)))
    return y, sa, shift, sb


def _weights(w_in, w_out, shift_mix, w0, w_up, a0, a_up, k_k, k_a, r_k, gn_a_g, gn_a_b, gn_b_g, gn_b_b,
             ln_g, ln_b):
    zw = jnp.zeros((LORA_W, W_A), F32)
    wwa = jnp.concatenate([jnp.concatenate([w_up[0], zw], axis=1),
                           jnp.concatenate([zw, a_up[0]], axis=1)], axis=0).astype(BF16)
    lane = jnp.arange(LANES)
    return {
        "w_in": w_in[0].astype(BF16),
        "w_out": w_out[0].astype(BF16),
        "shift_mix": shift_mix,
        "wwa": wwa,
        "w0": w0, "a0": a0, "k_k": k_k, "k_a": k_a, "r_k": r_k.reshape(1, W_A),
        "gn_a_g": gn_a_g, "gn_a_b": gn_a_b, "gn_b_g": gn_b_g, "gn_b_b": gn_b_b,
        "ln_g": ln_g, "ln_b": ln_b,
        "ones_bd": (lane[:, None] // HEAD_A == lane[None, :] // HEAD_A).astype(BF16),
        "log_g": jnp.log1p(-jnp.exp2(-5.0 - jnp.arange(H_B, dtype=F32))),
    }


def kernel(x_prompt, x_sample, state_rwkv, state_shift, state_ret, meta_tokens, w_in, w_out, shift_mix,
           w0, w_up, a0, a_up, k_k, k_a, r_k, gn_a_g, gn_a_b, gn_b_g, gn_b_b, ln_g, ln_b):
    bp, tp, _ = x_prompt.shape
    bs, ts, _ = x_sample.shape
    wts = _weights(w_in, w_out, shift_mix, w0, w_up, a0, a_up, k_k, k_a, r_k, gn_a_g, gn_a_b, gn_b_g,
                   gn_b_b, ln_g, ln_b)
    _, ra_m, sh_m, rb_m = _trunk(
        meta_tokens.astype(F32), 1, N_META, 0,
        jnp.zeros((1, H_A, HEAD_A, HEAD_A), F32), jnp.zeros((1, N_A_COLS), F32),
        jnp.zeros((1, H_B, DK_B, DK_B), F32), wts, N_META, N_META, 1, False)
    rep = lambda s: jnp.broadcast_to(s, (bp,) + s.shape[1:])
    y_p, ra_p, sh_p, rb_p = _trunk(
        x_prompt.reshape(bp * tp, D_MODEL), bp, tp, N_META, rep(ra_m), rep(sh_m), rep(rb_m),
        wts, 64, 128, 1, True)
    y_s, ra_s, sh_s, rb_s = _trunk(
        x_sample.reshape(bs * ts, D_MODEL), bs, ts, PAST_LEN, state_rwkv[0], state_shift[0], state_ret[0],
        wts, ts, ts, 8, True)
    return (y_p.reshape(bp, tp, D_MODEL), y_s.reshape(bs, ts, D_MODEL),
            ra_p[None], sh_p[None], rb_p[None], ra_s[None], sh_s[None], rb_s[None])
```

```python
import functools
import math

import jax
import jax.numpy as jnp
from jax import lax
from jax.experimental import pallas as pl
from jax.experimental.pallas import tpu as pltpu

D_MODEL = 2048
N_META = 16
W_A = 1024
HEAD_A = 64
H_A = W_A // HEAD_A
LORA_W = 64
LORA_A = 64
W_B = D_MODEL - W_A
H_B = 8
DK_B = W_B // H_B
N_A_COLS = 4 * W_A + LORA_W + LORA_A
N_B_COLS = 4 * W_B
N_IN = N_A_COLS + N_B_COLS
IN_PROJ_TN = N_IN // 5
DEPTH = 1
PAST_LEN = 16384
ALPHA = (2.0 * DEPTH) ** 0.25
ROPE_BASE = 10000.0
LN_EPS = 1e-5
GN_EPS_A = 64e-5
GN_EPS_B = 1e-5
EXP_NEG_HALF = math.exp(-0.5)

LANES = 128
N_PAIRS = W_A // LANES
VMEM_LIMIT_BYTES = 56 * 1024 * 1024

F32 = jnp.float32
BF16 = jnp.bfloat16


def _dot(a, b):
    return jnp.dot(a, b, preferred_element_type=F32)


def _dot_nt(a, b):
    return lax.dot_general(a, b, (((1,), (1,)), ((), ())), preferred_element_type=F32)


def _dot_tn(a, b):
    return lax.dot_general(a, b, (((0,), (0,)), ((), ())), preferred_element_type=F32)


def _split_dot(x, w_bf16):
    hi = x.astype(BF16)
    lo = (x - hi.astype(F32)).astype(BF16)
    return _dot(hi, w_bf16) + _dot(lo, w_bf16)


def _sigmoid(x):
    return 1.0 / (1.0 + jnp.exp(-x))


def _batch_loop(n, body):
    if n == 1:
        body(0)
    else:
        lax.fori_loop(0, n, lambda i, carry: (body(i), carry)[1], 0)


def _matmul_kernel(x_ref, w_ref, o_ref):
    o_ref[...] = _dot(x_ref[...].astype(BF16), w_ref[...])


def _matmul(x, w, tm, tn):
    m, k = x.shape
    n = w.shape[1]
    assert m % tm == 0 and n % tn == 0
    return pl.pallas_call(
        _matmul_kernel,
        grid=(m // tm, n // tn),
        in_specs=[pl.BlockSpec((tm, k), lambda i, j: (i, 0)),
                  pl.BlockSpec((k, tn), lambda i, j: (0, j))],
        out_specs=pl.BlockSpec((tm, tn), lambda i, j: (i, j)),
        out_shape=jax.ShapeDtypeStruct((m, n), F32),
        compiler_params=pltpu.CompilerParams(
            dimension_semantics=("arbitrary", "arbitrary"), vmem_limit_bytes=VMEM_LIMIT_BYTES),
        name="in_proj",
    )(x, w)


def _rwkv_kernel(za_ref, shift_ref, s0_ref, mix_ref, wwa_ref, w0_ref, a0_ref, kk_ref, ka_ref,
                 rk_ref, gng_ref, gnb_ref, ones_ref,
                 ya_ref, s_out_ref,
                 zs_scr, prev_scr, state_scr, *, chunk, n_chunks, nseq):
    C = chunk
    G = nseq
    R = G * C
    R2 = 2 * R
    log2c = int(math.log2(C))
    c = pl.program_id(1)

    @pl.when(c == 0)
    def _():
        prev_scr[...] = shift_ref[...]
        state_scr[...] = jnp.zeros(state_scr.shape, F32)
        for p in range(N_PAIRS):
            state_scr[:, p, 0:HEAD_A, 0:HEAD_A] = s0_ref[:, 2 * p]
            state_scr[:, p, HEAD_A:LANES, HEAD_A:LANES] = s0_ref[:, 2 * p + 1]

    ones_bd = ones_ref[...]
    head0 = lax.broadcasted_iota(jnp.int32, (C, LANES), 1) < HEAD_A
    ri = lax.broadcasted_iota(jnp.int32, (R, R), 0)
    ci = lax.broadcasted_iota(jnp.int32, (R, R), 1)
    tri_incl = (((ri >> log2c) == (ci >> log2c)) & (ci <= ri)).astype(BF16)
    i2 = lax.broadcasted_iota(jnp.int32, (R2, R2), 0)
    j2 = lax.broadcasted_iota(jnp.int32, (R2, R2), 1)
    same_block = (i2 >> log2c) == (j2 >> log2c)
    strict = same_block & (j2 < i2)
    incl = same_block & (j2 <= i2)
    eye = (i2 == j2).astype(F32)
    zero = jnp.zeros((R2, R2), F32)

    pairs = range(N_PAIRS)
    seqs = range(G)
    pair_slices = [slice(p * LANES, (p + 1) * LANES) for p in pairs]
    seq_rows = [slice(g * C, (g + 1) * C) for g in seqs]
    seq_rows2 = [slice(2 * g * C, 2 * (g + 1) * C) for g in seqs]

    def stack_parts(x):
        z = jnp.zeros((C, LANES), F32)
        return [jnp.concatenate([jnp.where(head0, x[rows], z), jnp.where(head0, z, x[rows])],
                                axis=0).astype(BF16) for rows in seq_rows]

    def join(parts):
        return parts[0] if G == 1 else jnp.concatenate(parts, axis=0)

    def unstack(o):
        return jnp.concatenate([o[2 * g * C:(2 * g + 1) * C] + o[(2 * g + 1) * C:(2 * g + 2) * C]
                                for g in seqs], axis=0)

    def seg_sum(x):
        return jnp.concatenate([_split_dot(x[:, s], ones_bd) for s in pair_slices], axis=1)

    def last_rows(x):
        return [x[(g + 1) * C - 1:(g + 1) * C] for g in seqs]

    za = za_ref[...]
    first = (lax.broadcasted_iota(jnp.int32, za.shape, 0) & (C - 1)) == 0
    carried = jnp.concatenate([jnp.broadcast_to(prev_scr[g], (C, N_A_COLS)) for g in seqs], axis=0)
    prev = jnp.where(first, carried, pltpu.roll(za, 1, axis=0))
    for g in seqs:
        prev_scr[g] = za_ref[(g + 1) * C - 1:(g + 1) * C, :]
    zs_scr[...] = za + (prev - za) * mix_ref[...]

    zwx = zs_scr[:, 4 * W_A:4 * W_A + LANES]
    lora_in = jnp.where(lax.broadcasted_iota(jnp.int32, (R, LANES), 1) < LORA_W, jnp.tanh(zwx), zwx)
    lora = _dot(lora_in.astype(BF16), wwa_ref[...])

    r = zs_scr[:, 0:W_A]
    k = zs_scr[:, W_A:2 * W_A]
    v = zs_scr[:, 2 * W_A:3 * W_A]
    g_gate = zs_scr[:, 3 * W_A:4 * W_A]
    logw = -EXP_NEG_HALF * _sigmoid(w0_ref[...] + lora[:, 0:W_A])
    a = _sigmoid(a0_ref[...] + lora[:, W_A:2 * W_A])
    l_hi = logw.astype(BF16)
    rem = logw - l_hi.astype(F32)
    l_mid = rem.astype(BF16)
    l_lo = (rem - l_mid.astype(F32)).astype(BF16)
    cum = _dot(tri_incl, l_hi) + _dot(tri_incl, l_mid) + _dot(tri_incl, l_lo)
    cum_last = last_rows(cum)
    cum_end = cum_last[0] if G == 1 else jnp.concatenate(
        [jnp.broadcast_to(x, (C, W_A)) for x in cum_last], axis=0)
    kkr = k * kk_ref[...]
    kk = kkr / jnp.maximum(jnp.sqrt(seg_sum(kkr * kkr)), 1e-12)
    kp = k * (1.0 + (a - 1.0) * ka_ref[...])
    b = kk * a
    inv_p = jnp.exp(-cum)
    p_end = jnp.exp(cum_end - cum)
    decay_end = [jnp.exp(x) for x in cum_last]
    at =-kk * jnp.exp(cum - logw)
    rt = r * jnp.exp(cum)
    bt = b * inv_p
    kt = kp * inv_p
    bh = b * p_end
    kh = kp * p_end

    a_g = [stack_parts(at[:, s]) for s in pair_slices]
    r_g = [stack_parts(rt[:, s]) for s in pair_slices]
    bh_g = [stack_parts(bh[:, s]) for s in pair_slices]
    kh_g = [stack_parts(kh[:, s]) for s in pair_slices]
    v_g = [stack_parts(v[:, s]) for s in pair_slices]
    a_s = [join(x) for x in a_g]
    r_s = [join(x) for x in r_g]
    v_s = [join(x) for x in v_g]
    b_s = [join(stack_parts(bt[:, s])) for s in pair_slices]
    k_s = [join(stack_parts(kt[:, s])) for s in pair_slices]
    n_pow = [jnp.where(strict, _dot_nt(a_s[p], b_s[p]), zero) for p in pairs]
    l_ak = [jnp.where(strict, _dot_nt(a_s[p], k_s[p]), zero).astype(BF16) for p in pairs]
    m_rb = [jnp.where(incl, _dot_nt(r_s[p], b_s[p]), zero).astype(BF16) for p in pairs]
    m_rk = [jnp.where(incl, _dot_nt(r_s[p], k_s[p]), zero).astype(BF16) for p in pairs]
    t_inv = [eye + n_pow[p] for p in pairs]
    for _ in range(log2c - 1):
        n_pow_b = [n.astype(BF16) for n in n_pow]
        n_pow = [_dot(n, n) for n in n_pow_b]
        t_inv = [t_inv[p] + _dot(t_inv[p].astype(BF16), n_pow[p].astype(BF16)) for p in pairs]
    s_old = [[state_scr[g, p] for p in pairs] for g in seqs]
    s_old_b = [[s.astype(BF16) for s in row] for row in s_old]
    y_state = [join([_dot_nt(a_g[p][g], s_old_b[g][p]) for g in seqs]) for p in pairs]
    o_state = [join([_dot_nt(r_g[p][g], s_old_b[g][p]) for g in seqs]) for p in pairs]
    y = [y_state[p] + _dot(l_ak[p], v_s[p]) for p in pairs]
    u = [_dot(t_inv[p].astype(BF16), y[p].astype(BF16)) for p in pairs]
    u_b = [x.astype(BF16) for x in u]
    o_s = [o_state[p] + _dot(m_rb[p], u_b[p]) + _dot(m_rk[p], v_s[p]) for p in pairs]
    for g in seqs:
        for p in pairs:
            u_g = u_b[p] if G == 1 else u[p][seq_rows2[g]].astype(BF16)
            state_scr[g, p] = (s_old[g][p] * decay_end[g][:, pair_slices[p]]
                               + _dot_tn(u_g, bh_g[p][g]) + _dot_tn(v_g[p][g], kh_g[p][g]))
    out = jnp.concatenate([unstack(o) for o in o_s], axis=1)

    inv_n = 1.0 / HEAD_A
    d = out - seg_sum(out) * inv_n
    var = seg_sum(d * d) * inv_n
    o_n = d * lax.rsqrt(var + GN_EPS_A) * gng_ref[...] + gnb_ref[...]
    bonus = seg_sum(r * kp * rk_ref[...]) * v
    ya_ref[...] = ((o_n + bonus) * (g_gate * _sigmoid(g_gate))).astype(ya_ref.dtype)

    @pl.when(c == n_chunks - 1)
    def _():
        for p in range(N_PAIRS):
            s_out_ref[:, 2 * p] = state_scr[:, p, 0:HEAD_A, 0:HEAD_A]
            s_out_ref[:, 2 * p + 1] = state_scr[:, p, HEAD_A:LANES, HEAD_A:LANES]


def _rwkv(z, shift, s0, wts, batch, seq, chunk, nseq):
    n_chunks = seq // chunk
    assert seq % chunk == 0 and batch % nseq == 0 and (nseq == 1 or n_chunks == 1)
    rows = nseq * chunk
    const2 = lambda b, c: (0, 0)
    kern = functools.partial(_rwkv_kernel, chunk=chunk, n_chunks=n_chunks, nseq=nseq)
    vec = pl.BlockSpec((1, W_A), const2)
    state_spec = pl.BlockSpec((nseq, H_A, HEAD_A, HEAD_A), lambda b, c: (b, 0, 0, 0))
    return pl.pallas_call(
        kern,
        grid=(batch // nseq, n_chunks),
        in_specs=[
            pl.BlockSpec((rows, N_A_COLS), lambda b, c: (b * n_chunks + c, 0)),
            pl.BlockSpec((nseq, 1, N_A_COLS), lambda b, c: (b, 0, 0)),
            state_spec,
            pl.BlockSpec((1, N_A_COLS), const2),
            pl.BlockSpec((LANES, 2 * W_A), const2),
            vec, vec, vec, vec, vec, vec, vec,
            pl.BlockSpec((LANES, LANES), const2),
        ],
        out_specs=[
            pl.BlockSpec((rows, W_A), lambda b, c: (b * n_chunks + c, 0)),
            state_spec,
        ],
        out_shape=[
            jax.ShapeDtypeStruct((batch * seq, W_A), BF16),
            jax.ShapeDtypeStruct((batch, H_A, HEAD_A, HEAD_A), F32),
        ],
        scratch_shapes=[
            pltpu.VMEM((rows, N_A_COLS), F32),
            pltpu.VMEM((nseq, 1, N_A_COLS), F32),
            pltpu.VMEM((nseq, N_PAIRS, LANES, LANES), F32),
        ],
        compiler_params=pltpu.CompilerParams(
            dimension_semantics=("arbitrary", "arbitrary"), vmem_limit_bytes=VMEM_LIMIT_BYTES),
        name="rwkv7_mix",
    )(z, shift, s0, wts["shift_mix"], wts["wwa"], wts["w0"], wts["a0"],
      wts["k_k"], wts["k_a"], wts["r_k"], wts["gn_a_g"], wts["gn_a_b"], wts["ones_bd"])


def _ret_kernel(lg_ref, lgv_ref, zb_ref, cos_ref, sin_ref, s0_ref, gng_ref, gnb_ref,
                yb_ref, s_out_ref, state_scr, *, chunk, n_chunks, bblk):
    C = chunk
    c = pl.program_id(1)

    @pl.when(c == 0)
    def _():
        state_scr[...] = s0_ref[...]

    heads = range(H_B)
    head_slices = [slice(h * DK_B, (h + 1) * DK_B) for h in heads]
    cos = jnp.concatenate([cos_ref[...]] * H_B, axis=1)
    sin = jnp.concatenate([sin_ref[...]] * H_B, axis=1)
    lgv = lgv_ref[...]
    rowf = lax.broadcasted_iota(jnp.int32, (C, W_B), 0).astype(F32)
    q_decay = jnp.exp((rowf + 1.0) * lgv)
    k_decay = jnp.exp((C - 1.0 - rowf) * lgv) * (DK_B ** -0.5)
    diff = (lax.broadcasted_iota(jnp.int32, (C, C), 0) - lax.broadcasted_iota(jnp.int32, (C, C), 1))
    causal = diff >= 0
    diff_f = jnp.maximum(diff, 0).astype(F32)
    dmask = [jnp.where(causal, jnp.exp(diff_f * lg_ref[h]), 0.0) for h in heads]
    s_decay = jnp.exp(C * lgv)

    def rot_half(x):
        return jnp.concatenate([pltpu.roll(x[:, s], DK_B // 2, axis=1) for s in head_slices], axis=1)

    def one_sequence(bb):
        q = zb_ref[bb, :, 0:W_B]
        k = zb_ref[bb, :, W_B:2 * W_B]
        v = zb_ref[bb, :, 2 * W_B:3 * W_B].astype(BF16)
        g = zb_ref[bb, :, 3 * W_B:4 * W_B]
        qr = q * cos + rot_half(q) * sin
        kr = k * cos + rot_half(k) * sin
        qr_b = qr.astype(BF16)
        kr_b = (kr * (DK_B ** -0.5)).astype(BF16)
        q_dec = (qr * q_decay).astype(BF16)
        k_dec = (kr * k_decay).astype(BF16)
        scores = [(_dot_nt(qr_b[:, s], kr_b[:, s]) * dmask[h]).astype(BF16)
                  for h, s in enumerate(head_slices)]
        s_old = [state_scr[bb, h] for h in heads]
        out = [_dot(scores[h], v[:, s]) + _dot(q_dec[:, s], s_old[h].astype(BF16))
               for h, s in enumerate(head_slices)]
        for h, s in enumerate(head_slices):
            state_scr[bb, h] = s_old[h] * s_decay[:, s] + _dot_tn(k_dec[:, s], v[:, s])
        d = [o - jnp.mean(o, axis=-1, keepdims=True) for o in out]
        rstd = [lax.rsqrt(jnp.mean(x * x, axis=-1, keepdims=True) + GN_EPS_B) for x in d]
        o_n = jnp.concatenate([d[h] * rstd[h] for h in heads], axis=1) * gng_ref[...] + gnb_ref[...]
        yb_ref[bb] = (o_n * (g * _sigmoid(g))).astype(yb_ref.dtype)

    _batch_loop(bblk, one_sequence)

    @pl.when(c == n_chunks - 1)
    def _():
        s_out_ref[...] = state_scr[...]


def _retention(z, cos, sin, s0, log_g, gn_g, gn_b, batch, seq, chunk, bblk):
    n_chunks = seq // chunk
    assert seq % chunk == 0 and batch % bblk == 0
    kern = functools.partial(_ret_kernel, chunk=chunk, n_chunks=n_chunks, bblk=bblk)
    state_spec = pl.BlockSpec((bblk, H_B, DK_B, DK_B), lambda b, c: (b, 0, 0, 0))
    return pl.pallas_call(
        kern,
        grid=(batch // bblk, n_chunks),
        in_specs=[
            pl.BlockSpec(memory_space=pltpu.SMEM),
            pl.BlockSpec((1, W_B), lambda b, c: (0, 0)),
            pl.BlockSpec((bblk, chunk, N_A_COLS), lambda b, c: (b, c, 1)),
            pl.BlockSpec((chunk, DK_B), lambda b, c: (c, 0)),
            pl.BlockSpec((chunk, DK_B), lambda b, c: (c, 0)),
            state_spec,
            pl.BlockSpec((1, W_B), lambda b, c: (0, 0)),
            pl.BlockSpec((1, W_B), lambda b, c: (0, 0)),
        ],
        out_specs=[
            pl.BlockSpec((bblk, chunk, W_B), lambda b, c: (b, c, 0)),
            state_spec,
        ],
        out_shape=[
            jax.ShapeDtypeStruct((batch, seq, W_B), BF16),
            jax.ShapeDtypeStruct((batch, H_B, DK_B, DK_B), F32),
        ],
        scratch_shapes=[pltpu.VMEM((bblk, H_B, DK_B, DK_B), F32)],
        compiler_params=pltpu.CompilerParams(
            dimension_semantics=("arbitrary", "arbitrary"), vmem_limit_bytes=VMEM_LIMIT_BYTES),
        name="retention",
    )(log_g, jnp.repeat(log_g, DK_B).reshape(1, W_B), z, cos, sin, s0, gn_g, gn_b)


def _out_kernel(ya_ref, yb_ref, w_ref, x_ref, g_ref, b_ref, y_ref):
    h = _dot(ya_ref[...], w_ref[0:W_A, :]) + _dot(yb_ref[...], w_ref[W_A:W_A + W_B, :])
    t = ALPHA * x_ref[...] + h
    mu = jnp.mean(t, axis=-1, keepdims=True)
    d = t - mu
    var = jnp.mean(d * d, axis=-1, keepdims=True)
    y_ref[...] = d * lax.rsqrt(var + LN_EPS) * g_ref[...] + b_ref[...]


def _out_proj(ya, yb, w_out, x, ln_g, ln_b, tm):
    m = x.shape[0]
    assert m % tm == 0
    rows = lambda i: (i, 0)
    const = lambda i: (0, 0)
    return pl.pallas_call(
        _out_kernel,
        grid=(m // tm,),
        in_specs=[
            pl.BlockSpec((tm, W_A), rows),
            pl.BlockSpec((tm, W_B), rows),
            pl.BlockSpec((W_A + W_B, D_MODEL), const),
            pl.BlockSpec((tm, D_MODEL), rows),
            pl.BlockSpec((1, D_MODEL), const),
            pl.BlockSpec((1, D_MODEL), const),
        ],
        out_specs=pl.BlockSpec((tm, D_MODEL), rows),
        out_shape=jax.ShapeDtypeStruct((m, D_MODEL), F32),
        compiler_params=pltpu.CompilerParams(
            dimension_semantics=("arbitrary",), vmem_limit_bytes=VMEM_LIMIT_BYTES),
        name="out_proj_ln",
    )(ya, yb, w_out, x, ln_g, ln_b)


def _rope_tables(pos):
    half = DK_B // 2
    inv = ROPE_BASE ** (-jnp.arange(half, dtype=F32) / half)
    ang = pos.astype(F32)[:, None] * inv[None, :]
    cos = jnp.cos(ang)
    sin = jnp.sin(ang)
    return jnp.concatenate([cos, cos], axis=-1), jnp.concatenate([-sin, sin], axis=-1)


def _tile(m, candidates):
    for t in candidates:
        if m % t == 0:
            return t
    return m


def _trunk(x2d, batch, seq, pos0, s_rwkv, s_shift, s_ret, wts, c_rwkv, c_ret, bblk, need_y):
    m = batch * seq
    z = _matmul(x2d, wts["w_in"], _tile(m, (1024, 512, 256)), IN_PROJ_TN)
    ya, sa = _rwkv(z, s_shift.reshape(batch, 1, N_A_COLS), s_rwkv, wts, batch, seq, c_rwkv, bblk)
    z3 = z.reshape(batch, seq, N_IN)
    cos, sin = _rope_tables(pos0 + jnp.arange(seq, dtype=jnp.int32))
    yb, sb = _retention(z3, cos, sin, s_ret, wts["log_g"], wts["gn_b_g"], wts["gn_b_b"],
                        batch, seq, c_ret, bblk)
    shift = z3[:, -1, :N_A_COLS]
    y = None
    if need_y:
        y = _out_proj(ya, yb.reshape(m, W_B), wts["w_out"], x2d, wts["ln_g"], wts["ln_b"],
                      _tile(m, (512, 256)))
    return y, sa, shift, sb


def _weights(w_in, w_out, shift_mix, w0, w_up, a0, a_up, k_k, k_a, r_k, gn_a_g, gn_a_b, gn_b_g, gn_b_b,
             ln_g, ln_b):
    zw = jnp.zeros((LORA_W, W_A), F32)
    wwa = jnp.concatenate([jnp.concatenate([w_up[0], zw], axis=1),
                           jnp.concatenate([zw, a_up[0]], axis=1)], axis=0).astype(BF16)
    lane = jnp.arange(LANES)
    return {
        "w_in": w_in[0].astype(BF16),
        "w_out": w_out[0].astype(BF16),
        "shift_mix": shift_mix,
        "wwa": wwa,
        "w0": w0, "a0": a0, "k_k": k_k, "k_a": k_a, "r_k": r_k.reshape(1, W_A),
        "gn_a_g": gn_a_g, "gn_a_b": gn_a_b, "gn_b_g": gn_b_g, "gn_b_b": gn_b_b,
        "ln_g": ln_g, "ln_b": ln_b,
        "ones_bd": (lane[:, None] // HEAD_A == lane[None, :] // HEAD_A).astype(BF16),
        "log_g": jnp.log1p(-jnp.exp2(-5.0 - jnp.arange(H_B, dtype=F32))),
    }


def kernel(x_prompt, x_sample, state_rwkv, state_shift, state_ret, meta_tokens, w_in, w_out, shift_mix,
           w0, w_up, a0, a_up, k_k, k_a, r_k, gn_a_g, gn_a_b, gn_b_g, gn_b_b, ln_g, ln_b):
    bp, tp, _ = x_prompt.shape
    bs, ts, _ = x_sample.shape
    wts = _weights(w_in, w_out, shift_mix, w0, w_up, a0, a_up, k_k, k_a, r_k, gn_a_g, gn_a_b, gn_b_g,
                   gn_b_b, ln_g, ln_b)
    _, ra_m, sh_m, rb_m = _trunk(
        meta_tokens.astype(F32), 1, N_META, 0,
        jnp.zeros((1, H_A, HEAD_A, HEAD_A), F32), jnp.zeros((1, N_A_COLS), F32),
        jnp.zeros((1, H_B, DK_B, DK_B), F32), wts, N_META, N_META, 1, False)
    rep = lambda s: jnp.broadcast_to(s, (bp,) + s.shape[1:])
    y_p, ra_p, sh_p, rb_p = _trunk(
        x_prompt.reshape(bp * tp, D_MODEL), bp, tp, N_META, rep(ra_m), rep(sh_m), rep(rb_m),
        wts, 64, 128, 1, True)
    y_s, ra_s, sh_s, rb_s = _trunk(
        x_sample.reshape(bs * ts, D_MODEL), bs, ts, PAST_LEN, state_rwkv[0], state_shift[0], state_ret[0],
        wts, ts, ts, 8, True)
    return (y_p.reshape(bp, tp, D_MODEL), y_s.reshape(bs, ts, D_MODEL),
            ra_p[None], sh_p[None], rb_p[None], ra_s[None], sh_s[None], rb_s[None])
```

```python
import functools
import math

import jax
import jax.numpy as jnp
from jax import lax
from jax.experimental import pallas as pl
from jax.experimental.pallas import tpu as pltpu

D_MODEL = 2048
N_META = 16
W_A = 1024
HEAD_A = 64
H_A = W_A // HEAD_A
LORA_W = 64
LORA_A = 64
W_B = D_MODEL - W_A
H_B = 8
DK_B = W_B // H_B
N_A_COLS = 4 * W_A + LORA_W + LORA_A
N_B_COLS = 4 * W_B
N_IN = N_A_COLS + N_B_COLS
IN_PROJ_TN = N_IN // 5
DEPTH = 1
PAST_LEN = 16384
ALPHA = (2.0 * DEPTH) ** 0.25
ROPE_BASE = 10000.0
LN_EPS = 1e-5
GN_EPS_A = 64e-5
GN_EPS_B = 1e-5
EXP_NEG_HALF = math.exp(-0.5)

LANES = 128
N_PAIRS = W_A // LANES
VMEM_LIMIT_BYTES = 56 * 1024 * 1024

F32 = jnp.float32
BF16 = jnp.bfloat16


def _dot(a, b):
    return jnp.dot(a, b, preferred_element_type=F32)


def _dot_nt(a, b):
    return lax.dot_general(a, b, (((1,), (1,)), ((), ())), preferred_element_type=F32)


def _dot_tn(a, b):
    return lax.dot_general(a, b, (((0,), (0,)), ((), ())), preferred_element_type=F32)


def _split_dot(x, w_bf16):
    hi = x.astype(BF16)
    lo = (x - hi.astype(F32)).astype(BF16)
    return _dot(hi, w_bf16) + _dot(lo, w_bf16)


def _sigmoid(x):
    return 1.0 / (1.0 + jnp.exp(-x))


def _batch_loop(n, body):
    if n == 1:
        body(0)
    else:
        lax.fori_loop(0, n, lambda i, carry: (body(i), carry)[1], 0)


def _matmul_kernel(x_ref, w_ref, o_ref):
    o_ref[...] = _dot(x_ref[...].astype(BF16), w_ref[...])


def _matmul(x, w, tm, tn):
    m, k = x.shape
    n = w.shape[1]
    assert m % tm == 0 and n % tn == 0
    return pl.pallas_call(
        _matmul_kernel,
        grid=(m // tm, n // tn),
        in_specs=[pl.BlockSpec((tm, k), lambda i, j: (i, 0)),
                  pl.BlockSpec((k, tn), lambda i, j: (0, j))],
        out_specs=pl.BlockSpec((tm, tn), lambda i, j: (i, j)),
        out_shape=jax.ShapeDtypeStruct((m, n), F32),
        compiler_params=pltpu.CompilerParams(
            dimension_semantics=("arbitrary", "arbitrary"), vmem_limit_bytes=VMEM_LIMIT_BYTES),
        name="in_proj",
    )(x, w)


def _rwkv_kernel(za_ref, shift_ref, s0_ref, mix_ref, wwa_ref, w0_ref, a0_ref, kk_ref, ka_ref,
                 rk_ref, gng_ref, gnb_ref, ones_ref,
                 ya_ref, s_out_ref,
                 zs_scr, prev_scr, state_scr, *, chunk, n_chunks, nseq):
    C = chunk
    G = nseq
    R = G * C
    R2 = 2 * R
    log2c = int(math.log2(C))
    c = pl.program_id(1)

    @pl.when(c == 0)
    def _():
        prev_scr[...] = shift_ref[...]
        state_scr[...] = jnp.zeros(state_scr.shape, F32)
        for p in range(N_PAIRS):
            state_scr[:, p, 0:HEAD_A, 0:HEAD_A] = s0_ref[:, 2 * p]
            state_scr[:, p, HEAD_A:LANES, HEAD_A:LANES] = s0_ref[:, 2 * p + 1]

    ones_bd = ones_ref[...]
    head0 = lax.broadcasted_iota(jnp.int32, (R, LANES), 1) < HEAD_A
    ri = lax.broadcasted_iota(jnp.int32, (R, R), 0)
    ci = lax.broadcasted_iota(jnp.int32, (R, R), 1)
    tri_incl = (((ri >> log2c) == (ci >> log2c)) & (ci <= ri)).astype(BF16)
    i2 = lax.broadcasted_iota(jnp.int32, (R, R2), 0)
    j2 = lax.broadcasted_iota(jnp.int32, (R, R2), 1) & (R - 1)
    same_seq = (i2 >> log2c) == (j2 >> log2c)
    strict = same_seq & (j2 < i2)
    incl = same_seq & (j2 <= i2)
    eye = (i2 == j2).astype(F32)
    zero = jnp.zeros((R, R2), F32)
    col_head0 = lax.broadcasted_iota(jnp.int32, (R, R2), 1) < R

    pairs = range(N_PAIRS)
    seqs = range(G)
    pair_slices = [slice(p * LANES, (p + 1) * LANES) for p in pairs]
    seq_rows = [slice(g * C, (g + 1) * C) for g in seqs]
    rows2 = lambda xs: jnp.concatenate(xs, axis=0)

    def stack(x, first_head=None):
        m = head0[:x.shape[0]] if first_head is None else first_head
        z = jnp.zeros_like(x)
        return rows2([jnp.where(m, x, z), jnp.where(m, z, x)]).astype(BF16)

    def join(parts):
        return parts[0] if G == 1 else jnp.concatenate(parts, axis=0)

    def seg_sum(x):
        return jnp.concatenate([_split_dot(x[:, s], ones_bd) for s in pair_slices], axis=1)

    def last_rows(x):
        return [x[(g + 1) * C - 1:(g + 1) * C] for g in seqs]

    za = za_ref[...]
    first = (lax.broadcasted_iota(jnp.int32, za.shape, 0) & (C - 1)) == 0
    carried = jnp.concatenate([jnp.broadcast_to(prev_scr[g], (C, N_A_COLS)) for g in seqs], axis=0)
    prev = jnp.where(first, carried, pltpu.roll(za, 1, axis=0))
    for g in seqs:
        prev_scr[g] = za_ref[(g + 1) * C - 1:(g + 1) * C, :]
    zs_scr[...] = za + (prev - za) * mix_ref[...]

    zwx = zs_scr[:, 4 * W_A:4 * W_A + LANES]
    lora_in = jnp.where(lax.broadcasted_iota(jnp.int32, (R, LANES), 1) < LORA_W, jnp.tanh(zwx), zwx)
    lora = _dot(lora_in.astype(BF16), wwa_ref[...])

    r = zs_scr[:, 0:W_A]
    k = zs_scr[:, W_A:2 * W_A]
    v = zs_scr[:, 2 * W_A:3 * W_A]
    g_gate = zs_scr[:, 3 * W_A:4 * W_A]
    logw = -EXP_NEG_HALF * _sigmoid(w0_ref[...] + lora[:, 0:W_A])
    a = _sigmoid(a0_ref[...] + lora[:, W_A:2 * W_A])
    l_hi = logw.astype(BF16)
    rem = logw - l_hi.astype(F32)
    l_mid = rem.astype(BF16)
    l_lo = (rem - l_mid.astype(F32)).astype(BF16)
    cum = _dot(tri_incl, l_hi) + _dot(tri_incl, l_mid) + _dot(tri_incl, l_lo)
    cum_last = last_rows(cum)
    cum_end = cum_last[0] if G == 1 else jnp.concatenate(
        [jnp.broadcast_to(x, (C, W_A)) for x in cum_last], axis=0)
    kkr = k * kk_ref[...]
    kk = kkr / jnp.maximum(jnp.sqrt(seg_sum(kkr * kkr)), 1e-12)
    kp = k * (1.0 + (a - 1.0) * ka_ref[...])
    b = kk * a
    inv_p = jnp.exp(-cum)
    p_end = jnp.exp(cum_end - cum)
    decay_end = [jnp.exp(x) for x in cum_last]
    at =-kk * jnp.exp(cum - logw)
    rt = r * jnp.exp(cum)
    bt = b * inv_p
    kt = kp * inv_p
    bh = b * p_end
    kh = kp * p_end

    at_p = [at[:, s] for s in pair_slices]
    rt_p = [rt[:, s] for s in pair_slices]
    v_p = [v[:, s] for s in pair_slices]
    ar = [rows2([at_p[p], rt_p[p]]).astype(BF16) for p in pairs]
    b_s = [stack(bt[:, s]) for s in pair_slices]
    k_s = [stack(kt[:, s]) for s in pair_slices]
    v_s = [stack(x) for x in v_p]
    if R2 % LANES == 0:
        sc = [_dot_nt(ar[p], rows2([b_s[p], k_s[p]])) for p in pairs]
        ab, ak = [x[:R, :R2] for x in sc], [x[:R, R2:] for x in sc]
        rb, rk = [x[R:, :R2] for x in sc], [x[R:, R2:] for x in sc]
    else:
        ab_rb = [_dot_nt(ar[p], b_s[p]) for p in pairs]
        ak_rk = [_dot_nt(ar[p], k_s[p]) for p in pairs]
        ab, rb = [x[:R] for x in ab_rb], [x[R:] for x in ab_rb]
        ak, rk = [x[:R] for x in ak_rk], [x[R:] for x in ak_rk]
    n_pow = [jnp.where(strict, x, zero) for x in ab]
    l_ak = [jnp.where(strict, x, zero).astype(BF16) for x in ak]
    m_rb = [jnp.where(incl, x, zero).astype(BF16) for x in rb]
    m_rk = [jnp.where(incl, x, zero).astype(BF16) for x in rk]
    t_inv = [eye + n_pow[p] for p in pairs]
    n_pow = [_dot(n.astype(BF16), stack(n, col_head0)) for n in n_pow]
    for level in range(log2c - 1):
        n_st = [stack(n, col_head0) for n in n_pow]
        if level < log2c - 2:
            both = [_dot(rows2([t_inv[p], n_pow[p]]).astype(BF16), n_st[p]) for p in pairs]
            t_inv = [t_inv[p] + both[p][:R] for p in pairs]
            n_pow = [x[R:] for x in both]
        else:
            t_inv = [t_inv[p] + _dot(t_inv[p].astype(BF16), n_st[p]) for p in pairs]
    s_old = [[state_scr[g, p] for p in pairs] for g in seqs]
    s_old_b = [[s.astype(BF16) for s in row] for row in s_old]
    yo = [[_dot_nt(rows2([at_p[p][seq_rows[g]], rt_p[p][seq_rows[g]]]).astype(BF16), s_old_b[g][p])
           for g in seqs] for p in pairs]
    y_state = [join([x[:C] for x in yo[p]]) for p in pairs]
    o_state = [join([x[C:] for x in yo[p]]) for p in pairs]
    lv = [_dot(rows2([l_ak[p], m_rk[p]]), v_s[p]) for p in pairs]
    y = [y_state[p] + lv[p][:R] for p in pairs]
    u = [_dot(t_inv[p].astype(BF16), stack(y[p])) for p in pairs]
    u_s = [stack(x) for x in u]
    o_p = [o_state[p] + _dot(m_rb[p], u_s[p]) + lv[p][R:] for p in pairs]
    for g in seqs:
        rows = seq_rows[g]
        for p in pairs:
            sl = pair_slices[p]
            if G == 1:
                uv = rows2([u_s[p], v_s[p]])
            else:
                uv = rows2([stack(u[p][rows]), stack(v_p[p][rows])])
            bk = rows2([stack(bh[rows, sl]), stack(kh[rows, sl])])
            state_scr[g, p] = s_old[g][p] * decay_end[g][:, sl] + _dot_tn(uv, bk)
    out = jnp.concatenate(o_p, axis=1)

    inv_n = 1.0 / HEAD_A
    d = out - seg_sum(out) * inv_n
    var = seg_sum(d * d) * inv_n
    o_n = d * lax.rsqrt(var + GN_EPS_A) * gng_ref[...] + gnb_ref[...]
    bonus = seg_sum(r * kp * rk_ref[...]) * v
    ya_ref[...] = ((o_n + bonus) * (g_gate * _sigmoid(g_gate))).astype(ya_ref.dtype)

    @pl.when(c == n_chunks - 1)
    def _():
        for p in range(N_PAIRS):
            s_out_ref[:, 2 * p] = state_scr[:, p, 0:HEAD_A, 0:HEAD_A]
            s_out_ref[:, 2 * p + 1] = state_scr[:, p, HEAD_A:LANES, HEAD_A:LANES]


def _rwkv(z, shift, s0, wts, batch, seq, chunk, nseq):
    n_chunks = seq // chunk
    assert seq % chunk == 0 and batch % nseq == 0 and (nseq == 1 or n_chunks == 1)
    rows = nseq * chunk
    const2 = lambda b, c: (0, 0)
    kern = functools.partial(_rwkv_kernel, chunk=chunk, n_chunks=n_chunks, nseq=nseq)
    vec = pl.BlockSpec((1, W_A), const2)
    state_spec = pl.BlockSpec((nseq, H_A, HEAD_A, HEAD_A), lambda b, c: (b, 0, 0, 0))
    return pl.pallas_call(
        kern,
        grid=(batch // nseq, n_chunks),
        in_specs=[
            pl.BlockSpec((rows, N_A_COLS), lambda b, c: (b * n_chunks + c, 0)),
            pl.BlockSpec((nseq, 1, N_A_COLS), lambda b, c: (b, 0, 0)),
            state_spec,
            pl.BlockSpec((1, N_A_COLS), const2),
            pl.BlockSpec((LANES, 2 * W_A), const2),
            vec, vec, vec, vec, vec, vec, vec,
            pl.BlockSpec((LANES, LANES), const2),
        ],
        out_specs=[
            pl.BlockSpec((rows, W_A), lambda b, c: (b * n_chunks + c, 0)),
            state_spec,
        ],
        out_shape=[
            jax.ShapeDtypeStruct((batch * seq, W_A), BF16),
            jax.ShapeDtypeStruct((batch, H_A, HEAD_A, HEAD_A), F32),
        ],
        scratch_shapes=[
            pltpu.VMEM((rows, N_A_COLS), F32),
            pltpu.VMEM((nseq, 1, N_A_COLS), F32),
            pltpu.VMEM((nseq, N_PAIRS, LANES, LANES), F32),
        ],
        compiler_params=pltpu.CompilerParams(
            dimension_semantics=("arbitrary", "arbitrary"), vmem_limit_bytes=VMEM_LIMIT_BYTES),
        name="rwkv7_mix",
    )(z, shift, s0, wts["shift_mix"], wts["wwa"], wts["w0"], wts["a0"],
      wts["k_k"], wts["k_a"], wts["r_k"], wts["gn_a_g"], wts["gn_a_b"], wts["ones_bd"])


def _ret_kernel(lg_ref, lgv_ref, zb_ref, cos_ref, sin_ref, s0_ref, gng_ref, gnb_ref,
                yb_ref, s_out_ref, state_scr, *, chunk, n_chunks, bblk):
    C = chunk
    c = pl.program_id(1)

    @pl.when(c == 0)
    def _():
        state_scr[...] = s0_ref[...]

    heads = range(H_B)
    head_slices = [slice(h * DK_B, (h + 1) * DK_B) for h in heads]
    cos = jnp.concatenate([cos_ref[...]] * H_B, axis=1)
    sin = jnp.concatenate([sin_ref[...]] * H_B, axis=1)
    lgv = lgv_ref[...]
    rowf = lax.broadcasted_iota(jnp.int32, (C, W_B), 0).astype(F32)
    q_decay = jnp.exp((rowf + 1.0) * lgv)
    k_decay = jnp.exp((C - 1.0 - rowf) * lgv) * (DK_B ** -0.5)
    diff = (lax.broadcasted_iota(jnp.int32, (C, C), 0) - lax.broadcasted_iota(jnp.int32, (C, C), 1))
    causal = diff >= 0
    diff_f = jnp.maximum(diff, 0).astype(F32)
    dmask = [jnp.where(causal, jnp.exp(diff_f * lg_ref[h]), 0.0) for h in heads]
    s_decay = jnp.exp(C * lgv)

    def rot_half(x):
        return jnp.concatenate([pltpu.roll(x[:, s], DK_B // 2, axis=1) for s in head_slices], axis=1)

    def one_sequence(bb):
        q = zb_ref[bb, :, 0:W_B]
        k = zb_ref[bb, :, W_B:2 * W_B]
        v = zb_ref[bb, :, 2 * W_B:3 * W_B].astype(BF16)
        g = zb_ref[bb, :, 3 * W_B:4 * W_B]
        qr = q * cos + rot_half(q) * sin
        kr = k * cos + rot_half(k) * sin
        qr_b = qr.astype(BF16)
        kr_b = (kr * (DK_B ** -0.5)).astype(BF16)
        q_dec = (qr * q_decay).astype(BF16)
        k_dec = (kr * k_decay).astype(BF16)
        scores = [(_dot_nt(qr_b[:, s], kr_b[:, s]) * dmask[h]).astype(BF16)
                  for h, s in enumerate(head_slices)]
        s_old = [state_scr[bb, h] for h in heads]
        out = [_dot(scores[h], v[:, s]) + _dot(q_dec[:, s], s_old[h].astype(BF16))
               for h, s in enumerate(head_slices)]
        for h, s in enumerate(head_slices):
            state_scr[bb, h] = s_old[h] * s_decay[:, s] + _dot_tn(k_dec[:, s], v[:, s])
        d = [o - jnp.mean(o, axis=-1, keepdims=True) for o in out]
        rstd = [lax.rsqrt(jnp.mean(x * x, axis=-1, keepdims=True) + GN_EPS_B) for x in d]
        o_n = jnp.concatenate([d[h] * rstd[h] for h in heads], axis=1) * gng_ref[...] + gnb_ref[...]
        yb_ref[bb] = (o_n * (g * _sigmoid(g))).astype(yb_ref.dtype)

    _batch_loop(bblk, one_sequence)

    @pl.when(c == n_chunks - 1)
    def _():
        s_out_ref[...] = state_scr[...]


def _retention(z, cos, sin, s0, log_g, gn_g, gn_b, batch, seq, chunk, bblk):
    n_chunks = seq // chunk
    assert seq % chunk == 0 and batch % bblk == 0
    kern = functools.partial(_ret_kernel, chunk=chunk, n_chunks=n_chunks, bblk=bblk)
    state_spec = pl.BlockSpec((bblk, H_B, DK_B, DK_B), lambda b, c: (b, 0, 0, 0))
    return pl.pallas_call(
        kern,
        grid=(batch // bblk, n_chunks),
        in_specs=[
            pl.BlockSpec(memory_space=pltpu.SMEM),
            pl.BlockSpec((1, W_B), lambda b, c: (0, 0)),
            pl.BlockSpec((bblk, chunk, N_A_COLS), lambda b, c: (b, c, 1)),
            pl.BlockSpec((chunk, DK_B), lambda b, c: (c, 0)),
            pl.BlockSpec((chunk, DK_B), lambda b, c: (c, 0)),
            state_spec,
            pl.BlockSpec((1, W_B), lambda b, c: (0, 0)),
            pl.BlockSpec((1, W_B), lambda b, c: (0, 0)),
        ],
        out_specs=[
            pl.BlockSpec((bblk, chunk, W_B), lambda b, c: (b, c, 0)),
            state_spec,
        ],
        out_shape=[
            jax.ShapeDtypeStruct((batch, seq, W_B), BF16),
            jax.ShapeDtypeStruct((batch, H_B, DK_B, DK_B), F32),
        ],
        scratch_shapes=[pltpu.VMEM((bblk, H_B, DK_B, DK_B), F32)],
        compiler_params=pltpu.CompilerParams(
            dimension_semantics=("arbitrary", "arbitrary"), vmem_limit_bytes=VMEM_LIMIT_BYTES),
        name="retention",
    )(log_g, jnp.repeat(log_g, DK_B).reshape(1, W_B), z, cos, sin, s0, gn_g, gn_b)


def _out_kernel(ya_ref, yb_ref, w_ref, x_ref, g_ref, b_ref, y_ref):
    h = _dot(ya_ref[...], w_ref[0:W_A, :]) + _dot(yb_ref[...], w_ref[W_A:W_A + W_B, :])
    t = ALPHA * x_ref[...] + h
    mu = jnp.mean(t, axis=-1, keepdims=True)
    d = t - mu
    var = jnp.mean(d * d, axis=-1, keepdims=True)
    y_ref[...] = d * lax.rsqrt(var + LN_EPS) * g_ref[...] + b_ref[...]


def _out_proj(ya, yb, w_out, x, ln_g, ln_b, tm):
    m = x.shape[0]
    assert m % tm == 0
    rows = lambda i: (i, 0)
    const = lambda i: (0, 0)
    return pl.pallas_call(
        _out_kernel,
        grid=(m // tm,),
        in_specs=[
            pl.BlockSpec((tm, W_A), rows),
            pl.BlockSpec((tm, W_B), rows),
            pl.BlockSpec((W_A + W_B, D_MODEL), const),
            pl.BlockSpec((tm, D_MODEL), rows),
            pl.BlockSpec((1, D_MODEL), const),
            pl.BlockSpec((1, D_MODEL), const),
        ],
        out_specs=pl.BlockSpec((tm, D_MODEL), rows),
        out_shape=jax.ShapeDtypeStruct((m, D_MODEL), F32),
        compiler_params=pltpu.CompilerParams(
            dimension_semantics=("arbitrary",), vmem_limit_bytes=VMEM_LIMIT_BYTES),
        name="out_proj_ln",
    )(ya, yb, w_out, x, ln_g, ln_b)


def _rope_tables(pos):
    half = DK_B // 2
    inv = ROPE_BASE ** (-jnp.arange(half, dtype=F32) / half)
    ang = pos.astype(F32)[:, None] * inv[None, :]
    cos = jnp.cos(ang)
    sin = jnp.sin(ang)
    return jnp.concatenate([cos, cos], axis=-1), jnp.concatenate([-sin, sin], axis=-1)


def _tile(m, candidates):
    for t in candidates:
        if m % t == 0:
            return t
    return m


def _trunk(x2d, batch, seq, pos0, s_rwkv, s_shift, s_ret, wts, c_rwkv, c_ret, bblk, need_y):
    m = batch * seq
    z = _matmul(x2d, wts["w_in"], _tile(m, (1024, 512, 256)), IN_PROJ_TN)
    ya, sa = _rwkv(z, s_shift.reshape(batch, 1, N_A_COLS), s_rwkv, wts, batch, seq, c_rwkv, bblk)
    z3 = z.reshape(batch, seq, N_IN)
    cos, sin = _rope_tables(pos0 + jnp.arange(seq, dtype=jnp.int32))
    yb, sb = _retention(z3, cos, sin, s_ret, wts["log_g"], wts["gn_b_g"], wts["gn_b_b"],
                        batch, seq, c_ret, bblk)
    shift = z3[:, -1, :N_A_COLS]
    y = None
    if need_y:
        y = _out_proj(ya, yb.reshape(m, W_B), wts["w_out"], x2d, wts["ln_g"], wts["ln_b"],
                      _tile(m, (512, 256)))
    return y, sa, shift, sb


def _weights(w_in, w_out, shift_mix, w0, w_up, a0, a_up, k_k, k_a, r_k, gn_a_g, gn_a_b, gn_b_g, gn_b_b,
             ln_g, ln_b):
    zw = jnp.zeros((LORA_W, W_A), F32)
    wwa = jnp.concatenate([jnp.concatenate([w_up[0], zw], axis=1),
                           jnp.concatenate([zw, a_up[0]], axis=1)], axis=0).astype(BF16)
    lane = jnp.arange(LANES)
    return {
        "w_in": w_in[0].astype(BF16),
        "w_out": w_out[0].astype(BF16),
        "shift_mix": shift_mix,
        "wwa": wwa,
        "w0": w0, "a0": a0, "k_k": k_k, "k_a": k_a, "r_k": r_k.reshape(1, W_A),
        "gn_a_g": gn_a_g, "gn_a_b": gn_a_b, "gn_b_g": gn_b_g, "gn_b_b": gn_b_b,
        "ln_g": ln_g, "ln_b": ln_b,
        "ones_bd": (lane[:, None] // HEAD_A == lane[None, :] // HEAD_A).astype(BF16),
        "log_g": jnp.log1p(-jnp.exp2(-5.0 - jnp.arange(H_B, dtype=F32))),
    }


def kernel(x_prompt, x_sample, state_rwkv, state_shift, state_ret, meta_tokens, w_in, w_out, shift_mix,
           w0, w_up, a0, a_up, k_k, k_a, r_k, gn_a_g, gn_a_b, gn_b_g, gn_b_b, ln_g, ln_b):
    bp, tp, _ = x_prompt.shape
    bs, ts, _ = x_sample.shape
    wts = _weights(w_in, w_out, shift_mix, w0, w_up, a0, a_up, k_k, k_a, r_k, gn_a_g, gn_a_b, gn_b_g,
                   gn_b_b, ln_g, ln_b)
    _, ra_m, sh_m, rb_m = _trunk(
        meta_tokens.astype(F32), 1, N_META, 0,
        jnp.zeros((1, H_A, HEAD_A, HEAD_A), F32), jnp.zeros((1, N_A_COLS), F32),
        jnp.zeros((1, H_B, DK_B, DK_B), F32), wts, N_META, N_META, 1, False)
    rep = lambda s: jnp.broadcast_to(s, (bp,) + s.shape[1:])
    y_p, ra_p, sh_p, rb_p = _trunk(
        x_prompt.reshape(bp * tp, D_MODEL), bp, tp, N_META, rep(ra_m), rep(sh_m), rep(rb_m),
        wts, 64, 128, 1, True)
    y_s, ra_s, sh_s, rb_s = _trunk(
        x_sample.reshape(bs * ts, D_MODEL), bs, ts, PAST_LEN, state_rwkv[0], state_shift[0], state_ret[0],
        wts, ts, ts, 8, True)
    return (y_p.reshape(bp, tp, D_MODEL), y_s.reshape(bs, ts, D_MODEL),
            ra_p[None], sh_p[None], rb_p[None], ra_s[None], sh_s[None], rb_s[None])
```

```python
import functools
import math

import jax
import jax.numpy as jnp
from jax import lax
from jax.experimental import pallas as pl
from jax.experimental.pallas import tpu as pltpu

D_MODEL = 2048
N_META = 16
W_A = 1024
HEAD_A = 64
H_A = W_A // HEAD_A
LORA_W = 64
LORA_A = 64
W_B = D_MODEL - W_A
H_B = 8
DK_B = W_B // H_B
N_A_COLS = 4 * W_A + LORA_W + LORA_A
N_B_COLS = 4 * W_B
N_IN = N_A_COLS + N_B_COLS
IN_PROJ_TN = N_IN // 5
DEPTH = 1
PAST_LEN = 16384
ALPHA = (2.0 * DEPTH) ** 0.25
ROPE_BASE = 10000.0
LN_EPS = 1e-5
GN_EPS_A = 64e-5
GN_EPS_B = 1e-5
EXP_NEG_HALF = math.exp(-0.5)

LANES = 128
N_PAIRS = W_A // LANES
VMEM_LIMIT_BYTES = 56 * 1024 * 1024

F32 = jnp.float32
BF16 = jnp.bfloat16


def _dot(a, b):
    return jnp.dot(a, b, preferred_element_type=F32)


def _dot_nt(a, b):
    return lax.dot_general(a, b, (((1,), (1,)), ((), ())), preferred_element_type=F32)


def _dot_tn(a, b):
    return lax.dot_general(a, b, (((0,), (0,)), ((), ())), preferred_element_type=F32)


def _split_dot(x, w_bf16):
    hi = x.astype(BF16)
    lo = (x - hi.astype(F32)).astype(BF16)
    return _dot(hi, w_bf16) + _dot(lo, w_bf16)


def _sigmoid(x):
    return 1.0 / (1.0 + jnp.exp(-x))


def _batch_loop(n, body):
    if n == 1:
        body(0)
    else:
        lax.fori_loop(0, n, lambda i, carry: (body(i), carry)[1], 0)


def _matmul_kernel(x_ref, w_ref, o_ref):
    o_ref[...] = _dot(x_ref[...].astype(BF16), w_ref[...])


def _matmul(x, w, tm, tn):
    m, k = x.shape
    n = w.shape[1]
    assert m % tm == 0 and n % tn == 0
    return pl.pallas_call(
        _matmul_kernel,
        grid=(m // tm, n // tn),
        in_specs=[pl.BlockSpec((tm, k), lambda i, j: (i, 0)),
                  pl.BlockSpec((k, tn), lambda i, j: (0, j))],
        out_specs=pl.BlockSpec((tm, tn), lambda i, j: (i, j)),
        out_shape=jax.ShapeDtypeStruct((m, n), F32),
        compiler_params=pltpu.CompilerParams(
            dimension_semantics=("arbitrary", "arbitrary"), vmem_limit_bytes=VMEM_LIMIT_BYTES),
        name="in_proj",
    )(x, w)


def _rwkv_kernel(za_ref, shift_ref, s0_ref, mix_ref, wwa_ref, w0_ref, a0_ref, kk_ref, ka_ref,
                 rk_ref, gng_ref, gnb_ref, ones_ref,
                 ya_ref, s_out_ref, shift_out_ref,
                 zs_scr, prev_scr, state_scr, *, chunk, n_chunks, nseq, ntile):
    C = chunk
    G = nseq
    NT = ntile
    R = G * C
    R2 = 2 * R
    RT = NT * R
    NS = NT * G
    log2c = int(math.log2(C))
    c = pl.program_id(1)

    @pl.when(c == 0)
    def _():
        prev_scr[...] = shift_ref[...]
        state_scr[...] = jnp.zeros(state_scr.shape, F32)
        for p in range(N_PAIRS):
            state_scr[:, p, 0:HEAD_A, 0:HEAD_A] = s0_ref[:, 2 * p]
            state_scr[:, p, HEAD_A:LANES, HEAD_A:LANES] = s0_ref[:, 2 * p + 1]

    ones_bd = ones_ref[...]
    head0 = {n: lax.broadcasted_iota(jnp.int32, (n, LANES), 1) < HEAD_A for n in {R, C}}
    ri =lax.broadcasted_iota(jnp.int32, (RT, RT), 0)
    ci = lax.broadcasted_iota(jnp.int32, (RT, RT), 1)
    tri_incl = (((ri >> log2c) == (ci >> log2c)) & (ci <= ri)).astype(BF16)
    i2 = lax.broadcasted_iota(jnp.int32, (R, R2), 0)
    j2 = lax.broadcasted_iota(jnp.int32, (R, R2), 1) & (R - 1)
    same_seq = (i2 >> log2c) == (j2 >> log2c)
    strict = same_seq & (j2 < i2)
    incl = same_seq & (j2 <= i2)
    eye = (i2 == j2).astype(F32)
    zero = jnp.zeros((R, R2), F32)
    col_head0 = lax.broadcasted_iota(jnp.int32, (R, R2), 1) < R

    pairs = range(N_PAIRS)
    tiles = range(NT)
    pair_slices = [slice(p * LANES, (p + 1) * LANES) for p in pairs]
    tile_rows = [slice(t * R, (t + 1) * R) for t in tiles]
    seq_rows = [slice(g * C, (g + 1) * C) for g in range(G)]
    rows2 = lambda xs: jnp.concatenate(xs, axis=0)

    def stack(x, first_head=None):
        m = head0[x.shape[0]] if first_head is None else first_head
        x = x.astype(BF16)
        z = jnp.zeros_like(x)
        return rows2([jnp.where(m, x, z), jnp.where(m, z, x)])

    def join(parts):
        return parts[0] if len(parts) == 1 else jnp.concatenate(parts, axis=0)

    def seg_sum(x):
        tall = _split_dot(rows2([x[:, s] for s in pair_slices]), ones_bd)
        return jnp.concatenate([tall[p * RT:(p + 1) * RT] for p in pairs], axis=1)

    za = join([za_ref[t, 0] for t in tiles])
    first = (lax.broadcasted_iota(jnp.int32, za.shape, 0) & (C - 1)) == 0
    carried = join([jnp.broadcast_to(prev_scr[s], (C, N_A_COLS)) for s in range(NS)])
    prev = jnp.where(first, carried, pltpu.roll(za, 1, axis=0))
    for s in range(NS):
        last = (s % G + 1) * C - 1
        prev_scr[s] = za_ref[s // G, 0, last:last + 1, :]
    zs_scr[...] = za + (prev - za) * mix_ref[...]

    zwx = zs_scr[:, 4 * W_A:4 * W_A + LANES]
    lora_in = jnp.where(lax.broadcasted_iota(jnp.int32, (RT, LANES), 1) < LORA_W, jnp.tanh(zwx), zwx)
    lora = _dot(lora_in.astype(BF16), wwa_ref[...])

    r = zs_scr[:, 0:W_A]
    k = zs_scr[:, W_A:2 * W_A]
    v = zs_scr[:, 2 * W_A:3 * W_A]
    g_gate = zs_scr[:, 3 * W_A:4 * W_A]
    logw = -EXP_NEG_HALF * _sigmoid(w0_ref[...] + lora[:, 0:W_A])
    a = _sigmoid(a0_ref[...] + lora[:, W_A:2 * W_A])
    l_hi = logw.astype(BF16)
    rem = logw - l_hi.astype(F32)
    l_mid = rem.astype(BF16)
    l_lo = (rem - l_mid.astype(F32)).astype(BF16)
    cum = _dot(tri_incl, l_hi) + _dot(tri_incl, l_mid) + _dot(tri_incl, l_lo)
    cum_last = [cum[(s + 1) * C - 1:(s + 1) * C] for s in range(NS)]
    cum_end = cum_last[0] if NS == 1 else jnp.concatenate(
        [jnp.broadcast_to(x, (C, W_A)) for x in cum_last], axis=0)
    kkr = k * kk_ref[...]
    kk = kkr / jnp.maximum(jnp.sqrt(seg_sum(kkr * kkr)), 1e-12)
    kp = k * (1.0 + (a - 1.0) * ka_ref[...])
    b = kk * a
    inv_p = jnp.exp(-cum)
    p_end = jnp.exp(cum_end - cum)
    decay_end = [jnp.exp(x) for x in cum_last]
    at = -kk * jnp.exp(cum - logw)
    rt = r * jnp.exp(cum)
    bt = b * inv_p
    kt = kp * inv_p
    bh = b * p_end
    kh = kp * p_end

    def recurrence(ids):
        part = lambda x, key: x[tile_rows[key[0]], pair_slices[key[1]]]
        at_p = {i: part(at, i) for i in ids}
        rt_p = {i: part(rt, i) for i in ids}
        v_p = {i: part(v, i) for i in ids}
        bh_p = {i: part(bh, i) for i in ids}
        kh_p = {i: part(kh, i) for i in ids}
        ar = {i: rows2([at_p[i], rt_p[i]]).astype(BF16) for i in ids}
        b_s = {i: stack(part(bt, i)) for i in ids}
        k_s = {i: stack(part(kt, i)) for i in ids}
        v_s = {i: stack(v_p[i]) for i in ids}
        if R2 % LANES == 0:
            sc = {i: _dot_nt(ar[i], rows2([b_s[i], k_s[i]])) for i in ids}
            ab, ak = {i: sc[i][:R, :R2] for i in ids}, {i: sc[i][:R, R2:] for i in ids}
            rb, rk = {i: sc[i][R:, :R2] for i in ids}, {i: sc[i][R:, R2:] for i in ids}
        else:
            ab_rb = {i: _dot_nt(ar[i], b_s[i]) for i in ids}
            ak_rk = {i: _dot_nt(ar[i], k_s[i]) for i in ids}
            ab, rb = {i: ab_rb[i][:R] for i in ids}, {i: ab_rb[i][R:] for i in ids}
            ak, rk = {i: ak_rk[i][:R] for i in ids}, {i: ak_rk[i][R:] for i in ids}
        n_pow = {i: jnp.where(strict, ab[i], zero) for i in ids}
        l_ak = {i: jnp.where(strict, ak[i], zero).astype(BF16) for i in ids}
        m_rb = {i: jnp.where(incl, rb[i], zero).astype(BF16) for i in ids}
        m_rk = {i: jnp.where(incl, rk[i], zero).astype(BF16) for i in ids}
        t_inv = {i: eye + n_pow[i] for i in ids}
        n_pow = {i: _dot(n_pow[i].astype(BF16), stack(n_pow[i], col_head0)) for i in ids}
        for level in range(log2c - 1):
            n_st = {i: stack(n_pow[i], col_head0) for i in ids}
            if level < log2c - 2:
                both = {i: _dot(rows2([t_inv[i], n_pow[i]]).astype(BF16), n_st[i]) for i in ids}
                t_inv = {i: t_inv[i] + both[i][:R] for i in ids}
                n_pow = {i: both[i][R:] for i in ids}
            else:
                t_inv = {i: t_inv[i] + _dot(t_inv[i].astype(BF16), n_st[i]) for i in ids}
        s_old = {(g, i): state_scr[i[0] * G + g, i[1]] for g in range(G) for i in ids}
        s_old_b = {key: val.astype(BF16) for key, val in s_old.items()}
        yo = {(g, i): _dot_nt(rows2([at_p[i][seq_rows[g]], rt_p[i][seq_rows[g]]]).astype(BF16),
                              s_old_b[g, i]) for g in range(G) for i in ids}
        y_state = {i: join([yo[g, i][:C] for g in range(G)]) for i in ids}
        o_state = {i: join([yo[g, i][C:] for g in range(G)]) for i in ids}
        lv = {i: _dot(rows2([l_ak[i], m_rk[i]]), v_s[i]) for i in ids}
        y = {i: y_state[i] + lv[i][:R] for i in ids}
        u = {i: _dot(t_inv[i].astype(BF16), stack(y[i])) for i in ids}
        u_s = {i: stack(u[i]) for i in ids}
        o_p = {i: o_state[i] + _dot(m_rb[i], u_s[i]) + lv[i][R:] for i in ids}
        for g in range(G):
            rows = seq_rows[g]
            for i in ids:
                seq = i[0] * G + g
                if G == 1:
                    uv = rows2([u_s[i], v_s[i]])
                else:
                    uv = rows2([stack(u[i][rows]), stack(v_p[i][rows])])
                bk = rows2([stack(bh_p[i][rows]), stack(kh_p[i][rows])])
                state_scr[seq, i[1]] = (s_old[g, i] * decay_end[seq][:, pair_slices[i[1]]]
                                        + _dot_tn(uv, bk))
        return o_p

    o_p = recurrence([(t, p) for t in tiles for p in pairs])
    out = join([jnp.concatenate([o_p[t, p] for p in pairs], axis=1) for t in tiles])

    inv_n = 1.0 / HEAD_A
    d = out - seg_sum(out) * inv_n
    var = seg_sum(d * d) * inv_n
    o_n = d * lax.rsqrt(var + GN_EPS_A) * gng_ref[...] + gnb_ref[...]
    bonus = seg_sum(r * kp * rk_ref[...]) * v
    ya = (o_n + bonus) * (g_gate * _sigmoid(g_gate))
    for t in tiles:
        ya_ref[t, 0] = ya[tile_rows[t]].astype(ya_ref.dtype)

    @pl.when(c == n_chunks - 1)
    def _():
        shift_out_ref[...] = prev_scr[...]
        for p in range(N_PAIRS):
            s_out_ref[:, 2 * p] = state_scr[:, p, 0:HEAD_A, 0:HEAD_A]
            s_out_ref[:, 2 * p + 1] = state_scr[:, p, HEAD_A:LANES, HEAD_A:LANES]


def _rwkv(z, shift, s0, wts, batch, seq, chunk, nseq, ntile):
    n_chunks = seq // chunk
    assert seq % chunk == 0 and batch % (nseq * ntile) == 0 and (nseq == 1 or n_chunks == 1)
    rows = nseq * chunk
    nsq = nseq * ntile
    const2 = lambda b, c: (0, 0)
    kern = functools.partial(_rwkv_kernel, chunk=chunk, n_chunks=n_chunks, nseq=nseq, ntile=ntile)
    vec = pl.BlockSpec((1, W_A), const2)
    state_spec = pl.BlockSpec((nsq, H_A, HEAD_A, HEAD_A), lambda b, c: (b, 0, 0, 0))
    shift_spec = pl.BlockSpec((nsq, 1, N_A_COLS), lambda b, c: (b, 0, 0))
    ya, s_new, shift_new = pl.pallas_call(
        kern,
        grid=(batch // nsq, n_chunks),
        in_specs=[
            pl.BlockSpec((ntile, 1, rows, N_A_COLS), lambda b, c: (b, c, 0, 0)),
            shift_spec,
            state_spec,
            pl.BlockSpec((1, N_A_COLS), const2),
            pl.BlockSpec((LANES, 2 * W_A), const2),
            vec, vec, vec, vec, vec, vec, vec,
            pl.BlockSpec((LANES, LANES), const2),
        ],
        out_specs=[
            pl.BlockSpec((ntile, 1, rows, W_A), lambda b, c: (b, c, 0, 0)),
            state_spec,
            shift_spec,
        ],
        out_shape=[
            jax.ShapeDtypeStruct((batch // nseq, n_chunks, rows, W_A), BF16),
            jax.ShapeDtypeStruct((batch, H_A, HEAD_A, HEAD_A), F32),
            jax.ShapeDtypeStruct((batch, 1, N_A_COLS), F32),
        ],
        scratch_shapes=[
            pltpu.VMEM((ntile * rows, N_A_COLS), F32),
            pltpu.VMEM((nsq, 1, N_A_COLS), F32),
            pltpu.VMEM((nsq, N_PAIRS, LANES, LANES), F32),
        ],
        compiler_params=pltpu.CompilerParams(
            dimension_semantics=("arbitrary", "arbitrary"), vmem_limit_bytes=VMEM_LIMIT_BYTES),
        name="rwkv7_mix",
    )(z.reshape(batch // nseq, n_chunks, rows, N_IN), shift, s0, wts["shift_mix"], wts["wwa"], wts["w0"],
      wts["a0"], wts["k_k"], wts["k_a"], wts["r_k"], wts["gn_a_g"], wts["gn_a_b"], wts["ones_bd"])
    return ya.reshape(batch * seq, W_A), s_new, shift_new.reshape(batch, N_A_COLS)


def _ret_kernel(lg_ref, lgv_ref, zb_ref, cos_ref, sin_ref, s0_ref, gng_ref, gnb_ref,
                yb_ref, s_out_ref, state_scr, *, chunk, n_chunks, bblk):
    C = chunk
    c = pl.program_id(1)

    @pl.when(c == 0)
    def _():
        state_scr[...] = s0_ref[...]

    heads = range(H_B)
    head_slices = [slice(h * DK_B, (h + 1) * DK_B) for h in heads]
    cos = jnp.concatenate([cos_ref[...]] * H_B, axis=1)
    sin = jnp.concatenate([sin_ref[...]] * H_B, axis=1)
    lgv = lgv_ref[...]
    rowf = lax.broadcasted_iota(jnp.int32, (C, W_B), 0).astype(F32)
    q_decay = jnp.exp((rowf + 1.0) * lgv)
    k_decay = jnp.exp((C - 1.0 - rowf) * lgv) * (DK_B ** -0.5)
    diff = (lax.broadcasted_iota(jnp.int32, (C, C), 0) - lax.broadcasted_iota(jnp.int32, (C, C), 1))
    causal = diff >= 0
    diff_f = jnp.maximum(diff, 0).astype(F32)
    dmask = [jnp.where(causal, jnp.exp(diff_f * lg_ref[h]), 0.0) for h in heads]
    s_decay = jnp.exp(C * lgv)

    def rot_half(x):
        return jnp.concatenate([pltpu.roll(x[:, s], DK_B // 2, axis=1) for s in head_slices], axis=1)

    def one_sequence(bb):
        q = zb_ref[bb, :, 0:W_B]
        k = zb_ref[bb, :, W_B:2 * W_B]
        v = zb_ref[bb, :, 2 * W_B:3 * W_B].astype(BF16)
        g = zb_ref[bb, :, 3 * W_B:4 * W_B]
        qr = q * cos + rot_half(q) * sin
        kr = k * cos + rot_half(k) * sin
        qr_b = qr.astype(BF16)
        kr_b = (kr * (DK_B ** -0.5)).astype(BF16)
        q_dec = (qr * q_decay).astype(BF16)
        k_dec = (kr * k_decay).astype(BF16)
        scores = [(_dot_nt(qr_b[:, s], kr_b[:, s]) * dmask[h]).astype(BF16)
                  for h, s in enumerate(head_slices)]
        s_old = [state_scr[bb, h] for h in heads]
        out = [_dot(scores[h], v[:, s]) + _dot(q_dec[:, s], s_old[h].astype(BF16))
               for h, s in enumerate(head_slices)]
        for h, s in enumerate(head_slices):
            state_scr[bb, h] = s_old[h] * s_decay[:, s] + _dot_tn(k_dec[:, s], v[:, s])
        d = [o - jnp.mean(o, axis=-1, keepdims=True) for o in out]
        rstd = [lax.rsqrt(jnp.mean(x * x, axis=-1, keepdims=True) + GN_EPS_B) for x in d]
        o_n = jnp.concatenate([d[h] * rstd[h] for h in heads], axis=1) * gng_ref[...] + gnb_ref[...]
        yb_ref[bb] = (o_n * (g * _sigmoid(g))).astype(yb_ref.dtype)

    _batch_loop(bblk, one_sequence)

    @pl.when(c == n_chunks - 1)
    def _():
        s_out_ref[...] = state_scr[...]


def _retention(z, cos, sin, s0, log_g, gn_g, gn_b, batch, seq, chunk, bblk):
    n_chunks = seq // chunk
    assert seq % chunk == 0 and batch % bblk == 0
    kern = functools.partial(_ret_kernel, chunk=chunk, n_chunks=n_chunks, bblk=bblk)
    state_spec = pl.BlockSpec((bblk, H_B, DK_B, DK_B), lambda b, c: (b, 0, 0, 0))
    return pl.pallas_call(
        kern,
        grid=(batch // bblk, n_chunks),
        in_specs=[
            pl.BlockSpec(memory_space=pltpu.SMEM),
            pl.BlockSpec((1, W_B), lambda b, c: (0, 0)),
            pl.BlockSpec((bblk, chunk, N_A_COLS), lambda b, c: (b, c, 1)),
            pl.BlockSpec((chunk, DK_B), lambda b, c: (c, 0)),
            pl.BlockSpec((chunk, DK_B), lambda b, c: (c, 0)),
            state_spec,
            pl.BlockSpec((1, W_B), lambda b, c: (0, 0)),
            pl.BlockSpec((1, W_B), lambda b, c: (0, 0)),
        ],
        out_specs=[
            pl.BlockSpec((bblk, chunk, W_B), lambda b, c: (b, c, 0)),
            state_spec,
        ],
        out_shape=[
            jax.ShapeDtypeStruct((batch, seq, W_B), BF16),
            jax.ShapeDtypeStruct((batch, H_B, DK_B, DK_B), F32),
        ],
        scratch_shapes=[pltpu.VMEM((bblk, H_B, DK_B, DK_B), F32)],
        compiler_params=pltpu.CompilerParams(
            dimension_semantics=("arbitrary", "arbitrary"), vmem_limit_bytes=VMEM_LIMIT_BYTES),
        name="retention",
    )(log_g, jnp.repeat(log_g, DK_B).reshape(1, W_B), z, cos, sin, s0, gn_g, gn_b)


def _out_kernel(ya_ref, yb_ref, w_ref, x_ref, g_ref, b_ref, y_ref):
    h = _dot(ya_ref[...], w_ref[0:W_A, :]) + _dot(yb_ref[...], w_ref[W_A:W_A + W_B, :])
    t = ALPHA * x_ref[...] + h
    mu = jnp.mean(t, axis=-1, keepdims=True)
    d = t - mu
    var = jnp.mean(d * d, axis=-1, keepdims=True)
    y_ref[...] = d * lax.rsqrt(var + LN_EPS) * g_ref[...] + b_ref[...]


def _out_proj(ya, yb, w_out, x, ln_g, ln_b, tm):
    m = x.shape[0]
    assert m % tm == 0
    rows = lambda i: (i, 0)
    const = lambda i: (0, 0)
    return pl.pallas_call(
        _out_kernel,
        grid=(m // tm,),
        in_specs=[
            pl.BlockSpec((tm, W_A), rows),
            pl.BlockSpec((tm, W_B), rows),
            pl.BlockSpec((W_A + W_B, D_MODEL), const),
            pl.BlockSpec((tm, D_MODEL), rows),
            pl.BlockSpec((1, D_MODEL), const),
            pl.BlockSpec((1, D_MODEL), const),
        ],
        out_specs=pl.BlockSpec((tm, D_MODEL), rows),
        out_shape=jax.ShapeDtypeStruct((m, D_MODEL), F32),
        compiler_params=pltpu.CompilerParams(
            dimension_semantics=("arbitrary",), vmem_limit_bytes=VMEM_LIMIT_BYTES),
        name="out_proj_ln",
    )(ya, yb, w_out, x, ln_g, ln_b)


def _rope_tables(pos):
    half = DK_B // 2
    inv = ROPE_BASE ** (-jnp.arange(half, dtype=F32) / half)
    ang = pos.astype(F32)[:, None] * inv[None, :]
    cos = jnp.cos(ang)
    sin = jnp.sin(ang)
    return jnp.concatenate([cos, cos], axis=-1), jnp.concatenate([-sin, sin], axis=-1)


def _tile(m, candidates):
    for t in candidates:
        if m % t == 0:
            return t
    return m


def _trunk(x2d, batch, seq, pos0, s_rwkv, s_shift, s_ret, wts, c_rwkv, c_ret, bblk, need_y, ntile=1):
    m = batch * seq
    z = _matmul(x2d, wts["w_in"], _tile(m, (1024, 512, 256)), IN_PROJ_TN)
    ya, sa, shift = _rwkv(z, s_shift.reshape(batch, 1, N_A_COLS), s_rwkv, wts, batch, seq, c_rwkv,
                          bblk, ntile)
    cos, sin = _rope_tables(pos0 + jnp.arange(seq, dtype=jnp.int32))
    yb, sb = _retention(z.reshape(batch, seq, N_IN), cos, sin, s_ret, wts["log_g"], wts["gn_b_g"],
                        wts["gn_b_b"], batch, seq, c_ret, bblk)
    y = None
    if need_y:
        y = _out_proj(ya, yb.reshape(m, W_B), wts["w_out"], x2d, wts["ln_g"], wts["ln_b"],
                      _tile(m, (512, 256)))
    return y, sa, shift, sb


def _weights(w_in, w_out, shift_mix, w0, w_up, a0, a_up, k_k, k_a, r_k, gn_a_g, gn_a_b, gn_b_g, gn_b_b,
             ln_g, ln_b):
    zw = jnp.zeros((LORA_W, W_A), F32)
    wwa = jnp.concatenate([jnp.concatenate([w_up[0], zw], axis=1),
                           jnp.concatenate([zw, a_up[0]], axis=1)], axis=0).astype(BF16)
    lane = jnp.arange(LANES)
    return {
        "w_in": w_in[0].astype(BF16),
        "w_out": w_out[0].astype(BF16),
        "shift_mix": shift_mix,
        "wwa": wwa,
        "w0": w0, "a0": a0, "k_k": k_k, "k_a": k_a, "r_k": r_k.reshape(1, W_A),
        "gn_a_g": gn_a_g, "gn_a_b": gn_a_b, "gn_b_g": gn_b_g, "gn_b_b": gn_b_b,
        "ln_g": ln_g, "ln_b": ln_b,
        "ones_bd": (lane[:, None] // HEAD_A == lane[None, :] // HEAD_A).astype(BF16),
        "log_g": jnp.log1p(-jnp.exp2(-5.0 - jnp.arange(H_B, dtype=F32))),
    }


def kernel(x_prompt, x_sample, state_rwkv, state_shift, state_ret, meta_tokens, w_in, w_out, shift_mix,
           w0, w_up, a0, a_up, k_k, k_a, r_k, gn_a_g, gn_a_b, gn_b_g, gn_b_b, ln_g, ln_b):
    bp, tp, _ = x_prompt.shape
    bs, ts, _ = x_sample.shape
    wts = _weights(w_in, w_out, shift_mix, w0, w_up, a0, a_up, k_k, k_a, r_k, gn_a_g, gn_a_b, gn_b_g,
                   gn_b_b, ln_g, ln_b)
    _, ra_m, sh_m, rb_m = _trunk(
        meta_tokens.astype(F32), 1, N_META, 0,
        jnp.zeros((1, H_A, HEAD_A, HEAD_A), F32), jnp.zeros((1, N_A_COLS), F32),
        jnp.zeros((1, H_B, DK_B, DK_B), F32), wts, N_META, N_META, 1, False)
    rep = lambda s: jnp.broadcast_to(s, (bp,) + s.shape[1:])
    y_p, ra_p, sh_p, rb_p = _trunk(
        x_prompt.reshape(bp * tp, D_MODEL), bp, tp, N_META, rep(ra_m), rep(sh_m), rep(rb_m),
        wts, 64, 128, 1, True, ntile=4)
    y_s, ra_s, sh_s, rb_s = _trunk(
        x_sample.reshape(bs * ts, D_MODEL), bs, ts, PAST_LEN, state_rwkv[0], state_shift[0], state_ret[0],
        wts, ts, ts, 8, True, ntile=2)
    return (y_p.reshape(bp, tp, D_MODEL), y_s.reshape(bs, ts, D_MODEL),
            ra_p[None], sh_p[None], rb_p[None], ra_s[None], sh_s[None], rb_s[None])
```

```python
import functools
import math

import jax
import jax.numpy as jnp
from jax import lax
from jax.experimental import pallas as pl
from jax.experimental.pallas import tpu as pltpu

D_MODEL = 2048
N_META = 16
W_A = 1024
HEAD_A = 64
H_A = W_A // HEAD_A
LORA_W = 64
LORA_A = 64
W_B = D_MODEL - W_A
H_B = 8
DK_B = W_B // H_B
N_A_COLS = 4 * W_A + LORA_W + LORA_A
N_B_COLS = 4 * W_B
N_IN = N_A_COLS + N_B_COLS
IN_PROJ_TN = N_IN // 5
DEPTH = 1
PAST_LEN = 16384
ALPHA = (2.0 * DEPTH) ** 0.25
ROPE_BASE = 10000.0
LN_EPS = 1e-5
GN_EPS_A = 64e-5
GN_EPS_B = 1e-5
EXP_NEG_HALF = math.exp(-0.5)

LANES = 128
N_PAIRS = W_A // LANES
VMEM_LIMIT_BYTES = 56 * 1024 * 1024

F32 = jnp.float32
BF16 = jnp.bfloat16


def _dot(a, b):
    return jnp.dot(a, b, preferred_element_type=F32)


def _dot_nt(a, b):
    return lax.dot_general(a, b, (((1,), (1,)), ((), ())), preferred_element_type=F32)


def _dot_tn(a, b):
    return lax.dot_general(a, b, (((0,), (0,)), ((), ())), preferred_element_type=F32)


def _split_dot(x, w_bf16):
    hi = x.astype(BF16)
    lo = (x - hi.astype(F32)).astype(BF16)
    return _dot(hi, w_bf16) + _dot(lo, w_bf16)


def _sigmoid(x):
    return 1.0 / (1.0 + jnp.exp(-x))


def _batch_loop(n, body):
    if n == 1:
        body(0)
    else:
        lax.fori_loop(0, n, lambda i, carry: (body(i), carry)[1], 0)


def _matmul_kernel(x_ref, w_ref, o_ref):
    o_ref[...] = _dot(x_ref[...].astype(BF16), w_ref[...])


def _matmul(x, w, tm, tn):
    m, k = x.shape
    n = w.shape[1]
    assert m % tm == 0 and n % tn == 0
    return pl.pallas_call(
        _matmul_kernel,
        grid=(m // tm, n // tn),
        in_specs=[pl.BlockSpec((tm, k), lambda i, j: (i, 0)),
                  pl.BlockSpec((k, tn), lambda i, j: (0, j))],
        out_specs=pl.BlockSpec((tm, tn), lambda i, j: (i, j)),
        out_shape=jax.ShapeDtypeStruct((m, n), F32),
        compiler_params=pltpu.CompilerParams(
            dimension_semantics=("arbitrary", "arbitrary"), vmem_limit_bytes=VMEM_LIMIT_BYTES),
        name="in_proj",
    )(x, w)


def _rwkv_kernel(za_ref, shift_ref, s0_ref, mix_ref, wwa_ref, w0_ref, a0_ref, kk_ref, ka_ref,
                 rk_ref, gng_ref, gnb_ref, ones_ref,
                 ya_ref, s_out_ref, shift_out_ref,
                 zs_scr, prev_scr, state_scr, *, chunk, n_chunks, nseq, ntile):
    C = chunk
    G = nseq
    NT = ntile
    R = G * C
    R2 = 2 * R
    RT = NT * R
    NS = NT * G
    log2c = int(math.log2(C))
    c = pl.program_id(1)

    @pl.when(c == 0)
    def _():
        prev_scr[...] = shift_ref[...]
        state_scr[...] = jnp.zeros(state_scr.shape, F32)
        for p in range(N_PAIRS):
            state_scr[:, p, 0:HEAD_A, 0:HEAD_A] = s0_ref[:, 2 * p]
            state_scr[:, p, HEAD_A:LANES, HEAD_A:LANES] = s0_ref[:, 2 * p + 1]

    ones_bd = ones_ref[...]
    head0 = {n: lax.broadcasted_iota(jnp.int32, (n, LANES), 1) < HEAD_A for n in {R, C}}
    ri =lax.broadcasted_iota(jnp.int32, (RT, RT), 0)
    ci = lax.broadcasted_iota(jnp.int32, (RT, RT), 1)
    tri_incl = (((ri >> log2c) == (ci >> log2c)) & (ci <= ri)).astype(BF16)
    i2 = lax.broadcasted_iota(jnp.int32, (R, R2), 0)
    j2 = lax.broadcasted_iota(jnp.int32, (R, R2), 1) & (R - 1)
    same_seq = (i2 >> log2c) == (j2 >> log2c)
    strict = same_seq & (j2 < i2)
    incl = same_seq & (j2 <= i2)
    eye = (i2 == j2).astype(F32)
    zero = jnp.zeros((R, R2), F32)
    col_head0 = lax.broadcasted_iota(jnp.int32, (R, R2), 1) < R

    pairs = range(N_PAIRS)
    tiles = range(NT)
    pair_slices = [slice(p * LANES, (p + 1) * LANES) for p in pairs]
    tile_rows = [slice(t * R, (t + 1) * R) for t in tiles]
    seq_rows = [slice(g * C, (g + 1) * C) for g in range(G)]
    rows2 = lambda xs: jnp.concatenate(xs, axis=0)

    def stack(x, first_head=None):
        m = head0[x.shape[0]] if first_head is None else first_head
        x = x.astype(BF16)
        z = jnp.zeros_like(x)
        return rows2([jnp.where(m, x, z), jnp.where(m, z, x)])

    def join(parts):
        return parts[0] if len(parts) == 1 else jnp.concatenate(parts, axis=0)

    def seg_sum(x):
        tall = _split_dot(rows2([x[:, s] for s in pair_slices]), ones_bd)
        return jnp.concatenate([tall[p * RT:(p + 1) * RT] for p in pairs], axis=1)

    za = join([za_ref[t, 0] for t in tiles])
    first = (lax.broadcasted_iota(jnp.int32, za.shape, 0) & (C - 1)) == 0
    carried = join([jnp.broadcast_to(prev_scr[s], (C, N_A_COLS)) for s in range(NS)])
    prev = jnp.where(first, carried, pltpu.roll(za, 1, axis=0))
    for s in range(NS):
        last = (s % G + 1) * C - 1
        prev_scr[s] = za_ref[s // G, 0, last:last + 1, :]
    zs_scr[...] = za + (prev - za) * mix_ref[...]

    zwx = zs_scr[:, 4 * W_A:4 * W_A + LANES]
    lora_in = jnp.where(lax.broadcasted_iota(jnp.int32, (RT, LANES), 1) < LORA_W, jnp.tanh(zwx), zwx)
    lora = _dot(lora_in.astype(BF16), wwa_ref[...])

    r = zs_scr[:, 0:W_A]
    k = zs_scr[:, W_A:2 * W_A]
    v = zs_scr[:, 2 * W_A:3 * W_A]
    g_gate = zs_scr[:, 3 * W_A:4 * W_A]
    logw = -EXP_NEG_HALF * _sigmoid(w0_ref[...] + lora[:, 0:W_A])
    a = _sigmoid(a0_ref[...] + lora[:, W_A:2 * W_A])
    l_hi = logw.astype(BF16)
    rem = logw - l_hi.astype(F32)
    l_mid = rem.astype(BF16)
    l_lo = (rem - l_mid.astype(F32)).astype(BF16)
    cum = _dot(tri_incl, l_hi) + _dot(tri_incl, l_mid) + _dot(tri_incl, l_lo)
    cum_last = [cum[(s + 1) * C - 1:(s + 1) * C] for s in range(NS)]
    cum_end = cum_last[0] if NS == 1 else jnp.concatenate(
        [jnp.broadcast_to(x, (C, W_A)) for x in cum_last], axis=0)
    kkr = k * kk_ref[...]
    kk = kkr / jnp.maximum(jnp.sqrt(seg_sum(kkr * kkr)), 1e-12)
    kp = k * (1.0 + (a - 1.0) * ka_ref[...])
    b = kk * a
    inv_p = jnp.exp(-cum)
    p_end = jnp.exp(cum_end - cum)
    decay_end = [jnp.exp(x) for x in cum_last]
    at = -kk * jnp.exp(cum - logw)
    rt = r * jnp.exp(cum)
    bt = b * inv_p
    kt = kp * inv_p
    bh = b * p_end
    kh = kp * p_end

    def recurrence(ids):
        part = lambda x, key: x[tile_rows[key[0]], pair_slices[key[1]]]
        at_p = {i: part(at, i) for i in ids}
        rt_p = {i: part(rt, i) for i in ids}
        v_p = {i: part(v, i) for i in ids}
        bh_p = {i: part(bh, i) for i in ids}
        kh_p = {i: part(kh, i) for i in ids}
        ar = {i: rows2([at_p[i], rt_p[i]]).astype(BF16) for i in ids}
        b_s = {i: stack(part(bt, i)) for i in ids}
        k_s = {i: stack(part(kt, i)) for i in ids}
        v_s = {i: stack(v_p[i]) for i in ids}
        if R2 % LANES == 0:
            sc = {i: _dot_nt(ar[i], rows2([b_s[i], k_s[i]])) for i in ids}
            ab, ak = {i: sc[i][:R, :R2] for i in ids}, {i: sc[i][:R, R2:] for i in ids}
            rb, rk = {i: sc[i][R:, :R2] for i in ids}, {i: sc[i][R:, R2:] for i in ids}
        else:
            ab_rb = {i: _dot_nt(ar[i], b_s[i]) for i in ids}
            ak_rk = {i: _dot_nt(ar[i], k_s[i]) for i in ids}
            ab, rb = {i: ab_rb[i][:R] for i in ids}, {i: ab_rb[i][R:] for i in ids}
            ak, rk = {i: ak_rk[i][:R] for i in ids}, {i: ak_rk[i][R:] for i in ids}
        n_pow = {i: jnp.where(strict, ab[i], zero) for i in ids}
        l_ak = {i: jnp.where(strict, ak[i], zero).astype(BF16) for i in ids}
        m_rb = {i: jnp.where(incl, rb[i], zero).astype(BF16) for i in ids}
        m_rk = {i: jnp.where(incl, rk[i], zero).astype(BF16) for i in ids}
        t_inv = {i: eye + n_pow[i] for i in ids}
        n_pow = {i: _dot(n_pow[i].astype(BF16), stack(n_pow[i], col_head0)) for i in ids}
        for level in range(log2c - 1):
            n_st = {i: stack(n_pow[i], col_head0) for i in ids}
            if level < log2c - 2:
                both = {i: _dot(rows2([t_inv[i], n_pow[i]]).astype(BF16), n_st[i]) for i in ids}
                t_inv = {i: t_inv[i] + both[i][:R] for i in ids}
                n_pow = {i: both[i][R:] for i in ids}
            else:
                t_inv = {i: t_inv[i] + _dot(t_inv[i].astype(BF16), n_st[i]) for i in ids}
        s_old = {(g, i): state_scr[i[0] * G + g, i[1]] for g in range(G) for i in ids}
        s_old_b = {key: val.astype(BF16) for key, val in s_old.items()}
        yo = {(g, i): _dot_nt(rows2([at_p[i][seq_rows[g]], rt_p[i][seq_rows[g]]]).astype(BF16),
                              s_old_b[g, i]) for g in range(G) for i in ids}
        y_state = {i: join([yo[g, i][:C] for g in range(G)]) for i in ids}
        o_state = {i: join([yo[g, i][C:] for g in range(G)]) for i in ids}
        lv = {i: _dot(rows2([l_ak[i], m_rk[i]]), v_s[i]) for i in ids}
        y = {i: y_state[i] + lv[i][:R] for i in ids}
        u = {i: _dot(t_inv[i].astype(BF16), stack(y[i])) for i in ids}
        u_s = {i: stack(u[i]) for i in ids}
        o_p = {i: o_state[i] + _dot(m_rb[i], u_s[i]) + lv[i][R:] for i in ids}
        for g in range(G):
            rows = seq_rows[g]
            for i in ids:
                seq = i[0] * G + g
                if G == 1:
                    uv = rows2([u_s[i], v_s[i]])
                else:
                    uv = rows2([stack(u[i][rows]), stack(v_p[i][rows])])
                bk = rows2([stack(bh_p[i][rows]), stack(kh_p[i][rows])])
                state_scr[seq, i[1]] = (s_old[g, i] * decay_end[seq][:, pair_slices[i[1]]]
                                        + _dot_tn(uv, bk))
        return o_p

    o_p = recurrence([(t, p) for t in tiles for p in pairs])
    out = join([jnp.concatenate([o_p[t, p] for p in pairs], axis=1) for t in tiles])

    inv_n = 1.0 / HEAD_A
    d = out - seg_sum(out) * inv_n
    var = seg_sum(d * d) * inv_n
    o_n = d * lax.rsqrt(var + GN_EPS_A) * gng_ref[...] + gnb_ref[...]
    bonus = seg_sum(r * kp * rk_ref[...]) * v
    ya = (o_n + bonus) * (g_gate * _sigmoid(g_gate))
    for t in tiles:
        ya_ref[t, 0] = ya[tile_rows[t]].astype(ya_ref.dtype)

    @pl.when(c == n_chunks - 1)
    def _():
        shift_out_ref[...] = prev_scr[...]
        for p in range(N_PAIRS):
            s_out_ref[:, 2 * p] = state_scr[:, p, 0:HEAD_A, 0:HEAD_A]
            s_out_ref[:, 2 * p + 1] = state_scr[:, p, HEAD_A:LANES, HEAD_A:LANES]


def _rwkv(z, shift, s0, wts, batch, seq, chunk, nseq, ntile):
    n_chunks = seq // chunk
    assert seq % chunk == 0 and batch % (nseq * ntile) == 0 and (nseq == 1 or n_chunks == 1)
    rows = nseq * chunk
    nsq = nseq * ntile
    const2 = lambda b, c: (0, 0)
    kern = functools.partial(_rwkv_kernel, chunk=chunk, n_chunks=n_chunks, nseq=nseq, ntile=ntile)
    vec = pl.BlockSpec((1, W_A), const2)
    state_spec = pl.BlockSpec((nsq, H_A, HEAD_A, HEAD_A), lambda b, c: (b, 0, 0, 0))
    shift_spec = pl.BlockSpec((nsq, 1, N_A_COLS), lambda b, c: (b, 0, 0))
    ya, s_new, shift_new = pl.pallas_call(
        kern,
        grid=(batch // nsq, n_chunks),
        in_specs=[
            pl.BlockSpec((ntile, 1, rows, N_A_COLS), lambda b, c: (b, c, 0, 0)),
            shift_spec,
            state_spec,
            pl.BlockSpec((1, N_A_COLS), const2),
            pl.BlockSpec((LANES, 2 * W_A), const2),
            vec, vec, vec, vec, vec, vec, vec,
            pl.BlockSpec((LANES, LANES), const2),
        ],
        out_specs=[
            pl.BlockSpec((ntile, 1, rows, W_A), lambda b, c: (b, c, 0, 0)),
            state_spec,
            shift_spec,
        ],
        out_shape=[
            jax.ShapeDtypeStruct((batch // nseq, n_chunks, rows, W_A), BF16),
            jax.ShapeDtypeStruct((batch, H_A, HEAD_A, HEAD_A), F32),
            jax.ShapeDtypeStruct((batch, 1, N_A_COLS), F32),
        ],
        scratch_shapes=[
            pltpu.VMEM((ntile * rows, N_A_COLS), F32),
            pltpu.VMEM((nsq, 1, N_A_COLS), F32),
            pltpu.VMEM((nsq, N_PAIRS, LANES, LANES), F32),
        ],
        compiler_params=pltpu.CompilerParams(
            dimension_semantics=("arbitrary", "arbitrary"), vmem_limit_bytes=VMEM_LIMIT_BYTES),
        name="rwkv7_mix",
    )(z.reshape(batch // nseq, n_chunks, rows, N_IN), shift, s0, wts["shift_mix"], wts["wwa"], wts["w0"],
      wts["a0"], wts["k_k"], wts["k_a"], wts["r_k"], wts["gn_a_g"], wts["gn_a_b"], wts["ones_bd"])
    return ya.reshape(batch * seq, W_A), s_new, shift_new.reshape(batch, N_A_COLS)


def _ret_kernel(lg_ref, lgv_ref, zb_ref, cos_ref, sin_ref, s0_ref, gng_ref, gnb_ref,
                yb_ref, s_out_ref, *scratch, chunk, n_chunks, nseq, ntile):
    C = chunk
    G = nseq
    NT = ntile
    R = G * C
    log2c = int(math.log2(C))
    c = pl.program_id(1)
    if n_chunks == 1:
        s_in, s_new = s0_ref, s_out_ref
    else:
        (state_scr,) = scratch
        s_in = s_new = state_scr

        @pl.when(c == 0)
        def _():
            state_scr[...] = s0_ref[...]

    heads = range(H_B)
    tiles = range(NT)
    head_slices = [slice(h * DK_B, (h + 1) * DK_B) for h in heads]
    seq_rows = [slice(g * C, (g + 1) * C) for g in range(G)]
    join = lambda parts: parts[0] if len(parts) == 1 else jnp.concatenate(parts, axis=0)
    cos = jnp.concatenate([join([cos_ref[...]] * G)] * H_B, axis=1)
    sin = jnp.concatenate([join([sin_ref[...]] * G)] * H_B, axis=1)
    lgv = lgv_ref[...]
    tok = (lax.broadcasted_iota(jnp.int32, (R, W_B), 0) & (C - 1)).astype(F32)
    q_decay = jnp.exp((tok + 1.0) * lgv)
    k_decay = jnp.exp((C - 1.0 - tok) * lgv) * (DK_B ** -0.5)
    ri = lax.broadcasted_iota(jnp.int32, (R, R), 0)
    ci = lax.broadcasted_iota(jnp.int32, (R, R), 1)
    causal = ((ri >> log2c) == (ci >> log2c)) & (ci <= ri)
    diff_f = jnp.maximum(ri - ci, 0).astype(F32)
    dmask = [jnp.where(causal, jnp.exp(diff_f * lg_ref[h]), 0.0) for h in heads]
    s_decay = jnp.exp(C * lgv)

    def rot_half(x):
        return jnp.concatenate([pltpu.roll(x[:, s], DK_B // 2, axis=1) for s in head_slices], axis=1)

    qr_b, kr_b, q_dec, k_dec, v_b, gate = {}, {}, {}, {}, {}, {}
    for t in tiles:
        q = zb_ref[t, 0, :, 0:W_B]
        k = zb_ref[t, 0, :, W_B:2 * W_B]
        g = zb_ref[t, 0, :, 3 * W_B:4 * W_B]
        qr = q * cos + rot_half(q) * sin
        kr = k * cos + rot_half(k) * sin
        qr_b[t] = qr.astype(BF16)
        kr_b[t] = (kr * (DK_B ** -0.5)).astype(BF16)
        q_dec[t] = (qr * q_decay).astype(BF16)
        k_dec[t] = (kr * k_decay).astype(BF16)
        v_b[t] = zb_ref[t, 0, :, 2 * W_B:3 * W_B].astype(BF16)
        gate[t] = g * _sigmoid(g)

    ids = [(t, h) for t in tiles for h in heads]
    hs = {i: head_slices[i[1]] for i in ids}
    scores = {i: (_dot_nt(qr_b[i[0]][:, hs[i]], kr_b[i[0]][:, hs[i]]) * dmask[i[1]]).astype(BF16)
              for i in ids}
    s_old = {(g, i): s_in[i[0] * G + g, i[1]] for g in range(G) for i in ids}
    cross = {i: join([_dot(q_dec[i[0]][seq_rows[g], hs[i]], s_old[g, i].astype(BF16)) for g in range(G)])
             for i in ids}
    out = {i: _dot(scores[i], v_b[i[0]][:, hs[i]]) + cross[i] for i in ids}
    for g in range(G):
        for i in ids:
            s_new[i[0] * G + g, i[1]] = (s_old[g, i] * s_decay[:, hs[i]]
                                         + _dot_tn(k_dec[i[0]][seq_rows[g], hs[i]],
                                                   v_b[i[0]][seq_rows[g], hs[i]]))
    d = {i: out[i] - jnp.mean(out[i], axis=-1, keepdims=True) for i in ids}
    rstd = {i: lax.rsqrt(jnp.mean(d[i] * d[i], axis=-1, keepdims=True) + GN_EPS_B) for i in ids}
    for t in tiles:
        o_n = jnp.concatenate([d[t, h] * rstd[t, h] for h in heads], axis=1) * gng_ref[...] + gnb_ref[...]
        yb_ref[t, 0] = (o_n * gate[t]).astype(yb_ref.dtype)

    if n_chunks > 1:
        @pl.when(c == n_chunks - 1)
        def _():
            s_out_ref[...] = state_scr[...]


def _retention(z, cos, sin, s0, log_g, gn_g, gn_b, batch, seq, chunk, nseq, ntile):
    n_chunks = seq // chunk
    assert seq % chunk == 0 and batch % (nseq * ntile) == 0 and (nseq == 1 or n_chunks == 1)
    rows = nseq * chunk
    nsq = nseq * ntile
    kern = functools.partial(_ret_kernel, chunk=chunk, n_chunks=n_chunks, nseq=nseq, ntile=ntile)
    state_spec = pl.BlockSpec((nsq, H_B, DK_B, DK_B), lambda b, c: (b, 0, 0, 0))
    yb, s_new = pl.pallas_call(
        kern,
        grid=(batch // nsq, n_chunks),
        in_specs=[
            pl.BlockSpec(memory_space=pltpu.SMEM),
            pl.BlockSpec((1, W_B), lambda b, c: (0, 0)),
            pl.BlockSpec((ntile, 1, rows, N_A_COLS), lambda b, c: (b, c, 0, 1)),
            pl.BlockSpec((chunk, DK_B), lambda b, c: (c, 0)),
            pl.BlockSpec((chunk, DK_B), lambda b, c: (c, 0)),
            state_spec,
            pl.BlockSpec((1, W_B), lambda b, c: (0, 0)),
            pl.BlockSpec((1, W_B), lambda b, c: (0, 0)),
        ],
        out_specs=[
            pl.BlockSpec((ntile, 1, rows, W_B), lambda b, c: (b, c, 0, 0)),
            state_spec,
        ],
        out_shape=[
            jax.ShapeDtypeStruct((batch // nseq, n_chunks, rows, W_B), BF16),
            jax.ShapeDtypeStruct((batch, H_B, DK_B, DK_B), F32),
        ],
        scratch_shapes=[pltpu.VMEM((nsq, H_B, DK_B, DK_B), F32)] if n_chunks > 1 else [],
        compiler_params=pltpu.CompilerParams(
            dimension_semantics=("arbitrary", "arbitrary"), vmem_limit_bytes=VMEM_LIMIT_BYTES),
        name="retention",
    )(log_g, jnp.repeat(log_g, DK_B).reshape(1, W_B), z.reshape(batch // nseq, n_chunks, rows, N_IN),
      cos, sin, s0, gn_g, gn_b)
    return yb.reshape(batch * seq, W_B), s_new


def _out_kernel(ya_ref, yb_ref, w_ref, x_ref, g_ref, b_ref, y_ref):
    h = _dot(ya_ref[...], w_ref[0:W_A, :]) + _dot(yb_ref[...], w_ref[W_A:W_A + W_B, :])
    t = ALPHA * x_ref[...] + h
    mu = jnp.mean(t, axis=-1, keepdims=True)
    d = t - mu
    var = jnp.mean(d * d, axis=-1, keepdims=True)
    y_ref[...] = d * lax.rsqrt(var + LN_EPS) * g_ref[...] + b_ref[...]


def _out_proj(ya, yb, w_out, x, ln_g, ln_b, tm):
    m = x.shape[0]
    assert m % tm == 0
    rows = lambda i: (i, 0)
    const = lambda i: (0, 0)
    return pl.pallas_call(
        _out_kernel,
        grid=(m // tm,),
        in_specs=[
            pl.BlockSpec((tm, W_A), rows),
            pl.BlockSpec((tm, W_B), rows),
            pl.BlockSpec((W_A + W_B, D_MODEL), const),
            pl.BlockSpec((tm, D_MODEL), rows),
            pl.BlockSpec((1, D_MODEL), const),
            pl.BlockSpec((1, D_MODEL), const),
        ],
        out_specs=pl.BlockSpec((tm, D_MODEL), rows),
        out_shape=jax.ShapeDtypeStruct((m, D_MODEL), F32),
        compiler_params=pltpu.CompilerParams(
            dimension_semantics=("arbitrary",), vmem_limit_bytes=VMEM_LIMIT_BYTES),
        name="out_proj_ln",
    )(ya, yb, w_out, x, ln_g, ln_b)


def _rope_tables(pos):
    half = DK_B // 2
    inv = ROPE_BASE ** (-jnp.arange(half, dtype=F32) / half)
    ang = pos.astype(F32)[:, None] * inv[None, :]
    cos = jnp.cos(ang)
    sin = jnp.sin(ang)
    return jnp.concatenate([cos, cos], axis=-1), jnp.concatenate([-sin, sin], axis=-1)


def _tile(m, candidates):
    for t in candidates:
        if m % t == 0:
            return t
    return m


def _trunk(x2d, batch, seq, pos0, s_rwkv, s_shift, s_ret, wts, c_rwkv, c_ret, bblk, need_y, ntile=1):
    m = batch * seq
    z = _matmul(x2d, wts["w_in"], _tile(m, (1024, 512, 256)), IN_PROJ_TN)
    ya, sa, shift = _rwkv(z, s_shift.reshape(batch, 1, N_A_COLS), s_rwkv, wts, batch, seq, c_rwkv,
                          bblk, ntile)
    cos, sin = _rope_tables(pos0 + jnp.arange(seq, dtype=jnp.int32))
    yb, sb = _retention(z, cos, sin, s_ret, wts["log_g"], wts["gn_b_g"], wts["gn_b_b"],
                        batch, seq, c_ret, bblk, ntile)
    y = None
    if need_y:
        y = _out_proj(ya, yb, wts["w_out"], x2d, wts["ln_g"], wts["ln_b"],
                      _tile(m, (512, 256)))
    return y, sa, shift, sb


def _weights(w_in, w_out, shift_mix, w0, w_up, a0, a_up, k_k, k_a, r_k, gn_a_g, gn_a_b, gn_b_g, gn_b_b,
             ln_g, ln_b):
    zw = jnp.zeros((LORA_W, W_A), F32)
    wwa = jnp.concatenate([jnp.concatenate([w_up[0], zw], axis=1),
                           jnp.concatenate([zw, a_up[0]], axis=1)], axis=0).astype(BF16)
    lane = jnp.arange(LANES)
    return {
        "w_in": w_in[0].astype(BF16),
        "w_out": w_out[0].astype(BF16),
        "shift_mix": shift_mix,
        "wwa": wwa,
        "w0": w0, "a0": a0, "k_k": k_k, "k_a": k_a, "r_k": r_k.reshape(1, W_A),
        "gn_a_g": gn_a_g, "gn_a_b": gn_a_b, "gn_b_g": gn_b_g, "gn_b_b": gn_b_b,
        "ln_g": ln_g, "ln_b": ln_b,
        "ones_bd": (lane[:, None] // HEAD_A == lane[None, :] // HEAD_A).astype(BF16),
        "log_g": jnp.log1p(-jnp.exp2(-5.0 - jnp.arange(H_B, dtype=F32))),
    }


def kernel(x_prompt, x_sample, state_rwkv, state_shift, state_ret, meta_tokens, w_in, w_out, shift_mix,
           w0, w_up, a0, a_up, k_k, k_a, r_k, gn_a_g, gn_a_b, gn_b_g, gn_b_b, ln_g, ln_b):
    bp, tp, _ = x_prompt.shape
    bs, ts, _ = x_sample.shape
    wts = _weights(w_in, w_out, shift_mix, w0, w_up, a0, a_up, k_k, k_a, r_k, gn_a_g, gn_a_b, gn_b_g,
                   gn_b_b, ln_g, ln_b)
    _, ra_m, sh_m, rb_m = _trunk(
        meta_tokens.astype(F32), 1, N_META, 0,
        jnp.zeros((1, H_A, HEAD_A, HEAD_A), F32), jnp.zeros((1, N_A_COLS), F32),
        jnp.zeros((1, H_B, DK_B, DK_B), F32), wts, N_META, N_META, 1, False)
    rep = lambda s: jnp.broadcast_to(s, (bp,) + s.shape[1:])
    y_p, ra_p, sh_p, rb_p = _trunk(
        x_prompt.reshape(bp * tp, D_MODEL), bp, tp, N_META, rep(ra_m), rep(sh_m), rep(rb_m),
        wts, 64, 128, 1, True, ntile=4)
    y_s, ra_s, sh_s, rb_s = _trunk(
        x_sample.reshape(bs * ts, D_MODEL), bs, ts, PAST_LEN, state_rwkv[0], state_shift[0], state_ret[0],
        wts, ts, ts, 8, True, ntile=2)
    return (y_p.reshape(bp, tp, D_MODEL), y_s.reshape(bs, ts, D_MODEL),
            ra_p[None], sh_p[None], rb_p[None], ra_s[None], sh_s[None], rb_s[None])
```

```python
import functools
import math

import jax
import jax.numpy as jnp
from jax import lax
from jax.experimental import pallas as pl
from jax.experimental.pallas import tpu as pltpu

D_MODEL = 2048
N_META = 16
W_A = 1024
HEAD_A = 64
H_A = W_A // HEAD_A
LORA_W = 64
LORA_A = 64
W_B = D_MODEL - W_A
H_B = 8
DK_B = W_B // H_B
N_A_COLS = 4 * W_A + LORA_W + LORA_A
N_B_COLS = 4 * W_B
N_IN = N_A_COLS + N_B_COLS
IN_PROJ_TN = N_IN // 5
OUT_SUB_ROWS = 128
DEPTH = 1
PAST_LEN = 16384
ALPHA = (2.0 * DEPTH) ** 0.25
ROPE_BASE = 10000.0
LN_EPS = 1e-5
GN_EPS_A = 64e-5
GN_EPS_B = 1e-5
EXP_NEG_HALF = math.exp(-0.5)

LANES = 128
N_PAIRS = W_A // LANES
VMEM_LIMIT_BYTES = 56 * 1024 * 1024

F32 = jnp.float32
BF16 = jnp.bfloat16


def _dot(a, b):
    return jnp.dot(a, b, preferred_element_type=F32)


def _dot_nt(a, b):
    return lax.dot_general(a, b, (((1,), (1,)), ((), ())), preferred_element_type=F32)


def _dot_tn(a, b):
    return lax.dot_general(a, b, (((0,), (0,)), ((), ())), preferred_element_type=F32)


def _split_dot(x, w_bf16):
    hi = x.astype(BF16)
    lo = (x - hi.astype(F32)).astype(BF16)
    return _dot(hi, w_bf16) + _dot(lo, w_bf16)


def _sigmoid(x):
    return 1.0 / (1.0 + jnp.exp(-x))


def _batch_loop(n, body):
    if n == 1:
        body(0)
    else:
        lax.fori_loop(0, n, lambda i, carry: (body(i), carry)[1], 0)


def _matmul_kernel(x_ref, w_ref, o_ref):
    o_ref[...] = _dot(x_ref[...].astype(BF16), w_ref[...])


def _matmul(x, w, tm, tn):
    m, k = x.shape
    n = w.shape[1]
    assert m % tm == 0 and n % tn == 0
    return pl.pallas_call(
        _matmul_kernel,
        grid=(m // tm, n // tn),
        in_specs=[pl.BlockSpec((tm, k), lambda i, j: (i, 0)),
                  pl.BlockSpec((k, tn), lambda i, j: (0, j))],
        out_specs=pl.BlockSpec((tm, tn), lambda i, j: (i, j)),
        out_shape=jax.ShapeDtypeStruct((m, n), F32),
        compiler_params=pltpu.CompilerParams(
            dimension_semantics=("arbitrary", "arbitrary"), vmem_limit_bytes=VMEM_LIMIT_BYTES),
        name="in_proj",
    )(x, w)


def _rwkv_kernel(za_ref, shift_ref, s0_ref, mix_ref, wwa_ref, w0_ref, a0_ref, kk_ref, ka_ref,
                 rk_ref, gng_ref, gnb_ref, ones_ref,
                 ya_ref, s_out_ref, shift_out_ref,
                 zs_scr, prev_scr, state_scr, *, chunk, n_chunks, nseq, ntile):
    C = chunk
    G = nseq
    NT = ntile
    R = G * C
    R2 = 2 * R
    RT = NT * R
    NS = NT * G
    log2c = int(math.log2(C))
    c = pl.program_id(1)

    @pl.when(c == 0)
    def _():
        prev_scr[...] = shift_ref[...]
        state_scr[...] = jnp.zeros(state_scr.shape, F32)
        for p in range(N_PAIRS):
            state_scr[:, p, 0:HEAD_A, 0:HEAD_A] = s0_ref[:, 2 * p]
            state_scr[:, p, HEAD_A:LANES, HEAD_A:LANES] = s0_ref[:, 2 * p + 1]

    ones_bd = ones_ref[...]
    head0 = {n: lax.broadcasted_iota(jnp.int32, (n, LANES), 1) < HEAD_A for n in {R, C}}
    ri =lax.broadcasted_iota(jnp.int32, (RT, RT), 0)
    ci = lax.broadcasted_iota(jnp.int32, (RT, RT), 1)
    tri_incl = (((ri >> log2c) == (ci >> log2c)) & (ci <= ri)).astype(BF16)
    i2 = lax.broadcasted_iota(jnp.int32, (R, R2), 0)
    j2 = lax.broadcasted_iota(jnp.int32, (R, R2), 1) & (R - 1)
    same_seq = (i2 >> log2c) == (j2 >> log2c)
    strict = same_seq & (j2 < i2)
    incl = same_seq & (j2 <= i2)
    eye = (i2 == j2).astype(F32)
    zero = jnp.zeros((R, R2), F32)
    col_head0 = lax.broadcasted_iota(jnp.int32, (R, R2), 1) < R

    pairs = range(N_PAIRS)
    tiles = range(NT)
    pair_slices = [slice(p * LANES, (p + 1) * LANES) for p in pairs]
    tile_rows = [slice(t * R, (t + 1) * R) for t in tiles]
    seq_rows = [slice(g * C, (g + 1) * C) for g in range(G)]
    rows2 = lambda xs: jnp.concatenate(xs, axis=0)

    def stack(x, first_head=None):
        m = head0[x.shape[0]] if first_head is None else first_head
        x = x.astype(BF16)
        z = jnp.zeros_like(x)
        return rows2([jnp.where(m, x, z), jnp.where(m, z, x)])

    def join(parts):
        return parts[0] if len(parts) == 1 else jnp.concatenate(parts, axis=0)

    def seg_sum(x, split):
        tall = rows2([x[:, s] for s in pair_slices])
        tall = _split_dot(tall, ones_bd) if split else _dot(tall.astype(BF16), ones_bd)
        return jnp.concatenate([tall[p * RT:(p + 1) * RT] for p in pairs], axis=1)

    za = join([za_ref[t, 0] for t in tiles])
    first = (lax.broadcasted_iota(jnp.int32, za.shape, 0) & (C - 1)) == 0
    carried = join([jnp.broadcast_to(prev_scr[s], (C, N_A_COLS)) for s in range(NS)])
    prev = jnp.where(first, carried, pltpu.roll(za, 1, axis=0))
    for s in range(NS):
        last = (s % G + 1) * C - 1
        prev_scr[s] = za_ref[s // G, 0, last:last + 1, :]
    zs_scr[...] = za + (prev - za) * mix_ref[...]

    zwx = zs_scr[:, 4 * W_A:4 * W_A + LANES]
    lora_in = jnp.where(lax.broadcasted_iota(jnp.int32, (RT, LANES), 1) < LORA_W, jnp.tanh(zwx), zwx)
    lora = _dot(lora_in.astype(BF16), wwa_ref[...])

    r = zs_scr[:, 0:W_A]
    k = zs_scr[:, W_A:2 * W_A]
    v = zs_scr[:, 2 * W_A:3 * W_A]
    g_gate = zs_scr[:, 3 * W_A:4 * W_A]
    logw = -EXP_NEG_HALF * _sigmoid(w0_ref[...] + lora[:, 0:W_A])
    a = _sigmoid(a0_ref[...] + lora[:, W_A:2 * W_A])
    l_hi = logw.astype(BF16)
    rem = logw - l_hi.astype(F32)
    l_mid = rem.astype(BF16)
    l_lo = (rem - l_mid.astype(F32)).astype(BF16)
    cum = _dot(tri_incl, l_hi) + _dot(tri_incl, l_mid) + _dot(tri_incl, l_lo)
    cum_last = [cum[(s + 1) * C - 1:(s + 1) * C] for s in range(NS)]
    cum_end = cum_last[0] if NS == 1 else jnp.concatenate(
        [jnp.broadcast_to(x, (C, W_A)) for x in cum_last], axis=0)
    kkr = k * kk_ref[...]
    kk = kkr * jnp.minimum(lax.rsqrt(seg_sum(kkr * kkr, False)), 1e12)
    kp = k * (1.0 + (a - 1.0) * ka_ref[...])
    b = kk * a
    inv_p = jnp.exp(-cum)
    p_end = jnp.exp(cum_end - cum)
    decay_end = [jnp.exp(x) for x in cum_last]
    at = -kk * jnp.exp(cum - logw)
    rt = r * jnp.exp(cum)
    bt = b * inv_p
    kt = kp * inv_p
    bh = b * p_end
    kh = kp * p_end

    def recurrence(ids):
        part = lambda x, key: x[tile_rows[key[0]], pair_slices[key[1]]]
        at_p = {i: part(at, i) for i in ids}
        rt_p = {i: part(rt, i) for i in ids}
        v_p = {i: part(v, i) for i in ids}
        bh_p = {i: part(bh, i) for i in ids}
        kh_p = {i: part(kh, i) for i in ids}
        ar = {i: rows2([at_p[i], rt_p[i]]).astype(BF16) for i in ids}
        b_s = {i: stack(part(bt, i)) for i in ids}
        k_s = {i: stack(part(kt, i)) for i in ids}
        v_s = {i: stack(v_p[i]) for i in ids}
        if R2 % LANES == 0:
            sc = {i: _dot_nt(ar[i], rows2([b_s[i], k_s[i]])) for i in ids}
            ab, ak = {i: sc[i][:R, :R2] for i in ids}, {i: sc[i][:R, R2:] for i in ids}
            rb, rk = {i: sc[i][R:, :R2] for i in ids}, {i: sc[i][R:, R2:] for i in ids}
        else:
            ab_rb = {i: _dot_nt(ar[i], b_s[i]) for i in ids}
            ak_rk = {i: _dot_nt(ar[i], k_s[i]) for i in ids}
            ab, rb = {i: ab_rb[i][:R] for i in ids}, {i: ab_rb[i][R:] for i in ids}
            ak, rk = {i: ak_rk[i][:R] for i in ids}, {i: ak_rk[i][R:] for i in ids}
        n_pow = {i: jnp.where(strict, ab[i], zero) for i in ids}
        l_ak = {i: jnp.where(strict, ak[i], zero).astype(BF16) for i in ids}
        m_rb = {i: jnp.where(incl, rb[i], zero).astype(BF16) for i in ids}
        m_rk = {i: jnp.where(incl, rk[i], zero).astype(BF16) for i in ids}
        t_inv = {i: eye + n_pow[i] for i in ids}
        n_pow = {i: _dot(n_pow[i].astype(BF16), stack(n_pow[i], col_head0)) for i in ids}
        for level in range(log2c - 1):
            n_st = {i: stack(n_pow[i], col_head0) for i in ids}
            if level < log2c - 2:
                both = {i: _dot(rows2([t_inv[i], n_pow[i]]).astype(BF16), n_st[i]) for i in ids}
                t_inv = {i: t_inv[i] + both[i][:R] for i in ids}
                n_pow = {i: both[i][R:] for i in ids}
            else:
                t_inv = {i: t_inv[i] + _dot(t_inv[i].astype(BF16), n_st[i]) for i in ids}
        s_old = {(g, i): state_scr[i[0] * G + g, i[1]] for g in range(G) for i in ids}
        s_old_b = {key: val.astype(BF16) for key, val in s_old.items()}
        yo = {(g, i): _dot_nt(rows2([at_p[i][seq_rows[g]], rt_p[i][seq_rows[g]]]).astype(BF16),
                              s_old_b[g, i]) for g in range(G) for i in ids}
        y_state = {i: join([yo[g, i][:C] for g in range(G)]) for i in ids}
        o_state = {i: join([yo[g, i][C:] for g in range(G)]) for i in ids}
        lv = {i: _dot(rows2([l_ak[i], m_rk[i]]), v_s[i]) for i in ids}
        y = {i: y_state[i] + lv[i][:R] for i in ids}
        u = {i: _dot(t_inv[i].astype(BF16), stack(y[i])) for i in ids}
        u_s = {i: stack(u[i]) for i in ids}
        o_p = {i: o_state[i] + _dot(m_rb[i], u_s[i]) + lv[i][R:] for i in ids}
        for g in range(G):
            rows = seq_rows[g]
            for i in ids:
                seq = i[0] * G + g
                if G == 1:
                    uv = rows2([u_s[i], v_s[i]])
                else:
                    uv = rows2([stack(u[i][rows]), stack(v_p[i][rows])])
                bk = rows2([stack(bh_p[i][rows]), stack(kh_p[i][rows])])
                state_scr[seq, i[1]] = (s_old[g, i] * decay_end[seq][:, pair_slices[i[1]]]
                                        + _dot_tn(uv, bk))
        return o_p

    o_p = recurrence([(t, p) for t in tiles for p in pairs])
    out = join([jnp.concatenate([o_p[t, p] for p in pairs], axis=1) for t in tiles])

    inv_n = 1.0 / HEAD_A
    d = out - seg_sum(out, True) * inv_n
    var = seg_sum(d * d, False) * inv_n
    o_n = d * lax.rsqrt(var + GN_EPS_A) * gng_ref[...] + gnb_ref[...]
    bonus = seg_sum(r * kp * rk_ref[...], True) * v
    ya = (o_n + bonus) * (g_gate * _sigmoid(g_gate))
    for t in tiles:
        ya_ref[t, 0] = ya[tile_rows[t]].astype(ya_ref.dtype)

    @pl.when(c == n_chunks - 1)
    def _():
        shift_out_ref[...] = prev_scr[...]
        for p in range(N_PAIRS):
            s_out_ref[:, 2 * p] = state_scr[:, p, 0:HEAD_A, 0:HEAD_A]
            s_out_ref[:, 2 * p + 1] = state_scr[:, p, HEAD_A:LANES, HEAD_A:LANES]


def _rwkv(z, shift, s0, wts, batch, seq, chunk, nseq, ntile):
    n_chunks = seq // chunk
    assert seq % chunk == 0 and batch % (nseq * ntile) == 0 and (nseq == 1 or n_chunks == 1)
    rows = nseq * chunk
    nsq = nseq * ntile
    const2 = lambda b, c: (0, 0)
    kern = functools.partial(_rwkv_kernel, chunk=chunk, n_chunks=n_chunks, nseq=nseq, ntile=ntile)
    vec = pl.BlockSpec((1, W_A), const2)
    state_spec = pl.BlockSpec((nsq, H_A, HEAD_A, HEAD_A), lambda b, c: (b, 0, 0, 0))
    shift_spec = pl.BlockSpec((nsq, 1, N_A_COLS), lambda b, c: (b, 0, 0))
    ya, s_new, shift_new = pl.pallas_call(
        kern,
        grid=(batch // nsq, n_chunks),
        in_specs=[
            pl.BlockSpec((ntile, 1, rows, N_A_COLS), lambda b, c: (b, c, 0, 0)),
            shift_spec,
            state_spec,
            pl.BlockSpec((1, N_A_COLS), const2),
            pl.BlockSpec((LANES, 2 * W_A), const2),
            vec, vec, vec, vec, vec, vec, vec,
            pl.BlockSpec((LANES, LANES), const2),
        ],
        out_specs=[
            pl.BlockSpec((ntile, 1, rows, W_A), lambda b, c: (b, c, 0, 0)),
            state_spec,
            shift_spec,
        ],
        out_shape=[
            jax.ShapeDtypeStruct((batch // nseq, n_chunks, rows, W_A), BF16),
            jax.ShapeDtypeStruct((batch, H_A, HEAD_A, HEAD_A), F32),
            jax.ShapeDtypeStruct((batch, 1, N_A_COLS), F32),
        ],
        scratch_shapes=[
            pltpu.VMEM((ntile * rows, N_A_COLS), F32),
            pltpu.VMEM((nsq, 1, N_A_COLS), F32),
            pltpu.VMEM((nsq, N_PAIRS, LANES, LANES), F32),
        ],
        compiler_params=pltpu.CompilerParams(
            dimension_semantics=("arbitrary", "arbitrary"), vmem_limit_bytes=VMEM_LIMIT_BYTES),
        name="rwkv7_mix",
    )(z.reshape(batch // nseq, n_chunks, rows, N_IN), shift, s0, wts["shift_mix"], wts["wwa"], wts["w0"],
      wts["a0"], wts["k_k"], wts["k_a"], wts["r_k"], wts["gn_a_g"], wts["gn_a_b"], wts["ones_bd"])
    return ya.reshape(batch * seq, W_A), s_new, shift_new.reshape(batch, N_A_COLS)


def _ret_kernel(lg_ref, lgv_ref, zb_ref, cos_ref, sin_ref, s0_ref, gng_ref, gnb_ref,
                yb_ref, s_out_ref, *scratch, chunk, n_chunks, nseq, ntile):
    C = chunk
    G = nseq
    NT = ntile
    R = G * C
    log2c = int(math.log2(C))
    c = pl.program_id(1)
    if n_chunks == 1:
        s_in, s_new = s0_ref, s_out_ref
    else:
        (state_scr,) = scratch
        s_in = s_new = state_scr

        @pl.when(c == 0)
        def _():
            state_scr[...] = s0_ref[...]

    heads = range(H_B)
    tiles = range(NT)
    head_slices = [slice(h * DK_B, (h + 1) * DK_B) for h in heads]
    seq_rows = [slice(g * C, (g + 1) * C) for g in range(G)]
    join = lambda parts: parts[0] if len(parts) == 1 else jnp.concatenate(parts, axis=0)
    cos = jnp.concatenate([join([cos_ref[...]] * G)] * H_B, axis=1)
    sin = jnp.concatenate([join([sin_ref[...]] * G)] * H_B, axis=1)
    lgv = lgv_ref[...]
    tok = (lax.broadcasted_iota(jnp.int32, (R, W_B), 0) & (C - 1)).astype(F32)
    q_decay = jnp.exp((tok + 1.0) * lgv)
    k_decay = jnp.exp((C - 1.0 - tok) * lgv) * (DK_B ** -0.5)
    ri = lax.broadcasted_iota(jnp.int32, (R, R), 0)
    ci = lax.broadcasted_iota(jnp.int32, (R, R), 1)
    causal = ((ri >> log2c) == (ci >> log2c)) & (ci <= ri)
    diff_f = jnp.maximum(ri - ci, 0).astype(F32)
    dmask = [jnp.where(causal, jnp.exp(diff_f * lg_ref[h]), 0.0) for h in heads]
    s_decay = jnp.exp(C * lgv)

    def rot_half(x):
        return jnp.concatenate([pltpu.roll(x[:, s], DK_B // 2, axis=1) for s in head_slices], axis=1)

    qr_b, kr_b, q_dec, k_dec, v_b, gate = {}, {}, {}, {}, {}, {}
    for t in tiles:
        q = zb_ref[t, 0, :, 0:W_B]
        k = zb_ref[t, 0, :, W_B:2 * W_B]
        g = zb_ref[t, 0, :, 3 * W_B:4 * W_B]
        qr = q * cos + rot_half(q) * sin
        kr = k * cos + rot_half(k) * sin
        qr_b[t] = qr.astype(BF16)
        kr_b[t] = (kr * (DK_B ** -0.5)).astype(BF16)
        q_dec[t] = (qr * q_decay).astype(BF16)
        k_dec[t] = (kr * k_decay).astype(BF16)
        v_b[t] = zb_ref[t, 0, :, 2 * W_B:3 * W_B].astype(BF16)
        gate[t] = g * _sigmoid(g)

    ids = [(t, h) for t in tiles for h in heads]
    hs = {i: head_slices[i[1]] for i in ids}
    scores = {i: (_dot_nt(qr_b[i[0]][:, hs[i]], kr_b[i[0]][:, hs[i]]) * dmask[i[1]]).astype(BF16)
              for i in ids}
    s_old = {(g, i): s_in[i[0] * G + g, i[1]] for g in range(G) for i in ids}
    cross = {i: join([_dot(q_dec[i[0]][seq_rows[g], hs[i]], s_old[g, i].astype(BF16)) for g in range(G)])
             for i in ids}
    out = {i: _dot(scores[i], v_b[i[0]][:, hs[i]]) + cross[i] for i in ids}
    for g in range(G):
        for i in ids:
            s_new[i[0] * G + g, i[1]] = (s_old[g, i] * s_decay[:, hs[i]]
                                         + _dot_tn(k_dec[i[0]][seq_rows[g], hs[i]],
                                                   v_b[i[0]][seq_rows[g], hs[i]]))
    d = {i: out[i] - jnp.mean(out[i], axis=-1, keepdims=True) for i in ids}
    rstd = {i: lax.rsqrt(jnp.mean(d[i] * d[i], axis=-1, keepdims=True) + GN_EPS_B) for i in ids}
    for t in tiles:
        o_n = jnp.concatenate([d[t, h] * rstd[t, h] for h in heads], axis=1) * gng_ref[...] + gnb_ref[...]
        yb_ref[t, 0] = (o_n * gate[t]).astype(yb_ref.dtype)

    if n_chunks > 1:
        @pl.when(c == n_chunks - 1)
        def _():
            s_out_ref[...] = state_scr[...]


def _retention(z, cos, sin, s0, log_g, gn_g, gn_b, batch, seq, chunk, nseq, ntile):
    n_chunks = seq // chunk
    assert seq % chunk == 0 and batch % (nseq * ntile) == 0 and (nseq == 1 or n_chunks == 1)
    rows = nseq * chunk
    nsq = nseq * ntile
    kern = functools.partial(_ret_kernel, chunk=chunk, n_chunks=n_chunks, nseq=nseq, ntile=ntile)
    state_spec = pl.BlockSpec((nsq, H_B, DK_B, DK_B), lambda b, c: (b, 0, 0, 0))
    yb, s_new = pl.pallas_call(
        kern,
        grid=(batch // nsq, n_chunks),
        in_specs=[
            pl.BlockSpec(memory_space=pltpu.SMEM),
            pl.BlockSpec((1, W_B), lambda b, c: (0, 0)),
            pl.BlockSpec((ntile, 1, rows, N_A_COLS), lambda b, c: (b, c, 0, 1)),
            pl.BlockSpec((chunk, DK_B), lambda b, c: (c, 0)),
            pl.BlockSpec((chunk, DK_B), lambda b, c: (c, 0)),
            state_spec,
            pl.BlockSpec((1, W_B), lambda b, c: (0, 0)),
            pl.BlockSpec((1, W_B), lambda b, c: (0, 0)),
        ],
        out_specs=[
            pl.BlockSpec((ntile, 1, rows, W_B), lambda b, c: (b, c, 0, 0)),
            state_spec,
        ],
        out_shape=[
            jax.ShapeDtypeStruct((batch // nseq, n_chunks, rows, W_B), BF16),
            jax.ShapeDtypeStruct((batch, H_B, DK_B, DK_B), F32),
        ],
        scratch_shapes=[pltpu.VMEM((nsq, H_B, DK_B, DK_B), F32)] if n_chunks > 1 else [],
        compiler_params=pltpu.CompilerParams(
            dimension_semantics=("arbitrary", "arbitrary"), vmem_limit_bytes=VMEM_LIMIT_BYTES),
        name="retention",
    )(log_g, jnp.repeat(log_g, DK_B).reshape(1, W_B), z.reshape(batch // nseq, n_chunks, rows, N_IN),
      cos, sin, s0, gn_g, gn_b)
    return yb.reshape(batch * seq, W_B), s_new


def _out_kernel(ya_ref, yb_ref, w_ref, x_ref, g_ref, b_ref, y_ref):
    tm = x_ref.shape[0]
    sub = min(tm, OUT_SUB_ROWS)
    blocks = [slice(i, i + sub) for i in range(0, tm, sub)]
    hs = [_dot(ya_ref[rows, :], w_ref[0:W_A, :]) + _dot(yb_ref[rows, :], w_ref[W_A:W_A + W_B, :])
          for rows in blocks]
    for rows, h in zip(blocks, hs):
        t = ALPHA * x_ref[rows, :] + h
        mu = jnp.mean(t, axis=-1, keepdims=True)
        d = t - mu
        var = jnp.mean(d * d, axis=-1, keepdims=True)
        y_ref[rows, :] = d * lax.rsqrt(var + LN_EPS) * g_ref[...] + b_ref[...]


def _out_proj(ya, yb, w_out, x, ln_g, ln_b, tm):
    m = x.shape[0]
    assert m % tm == 0
    rows = lambda i: (i, 0)
    const = lambda i: (0, 0)
    return pl.pallas_call(
        _out_kernel,
        grid=(m // tm,),
        in_specs=[
            pl.BlockSpec((tm, W_A), rows),
            pl.BlockSpec((tm, W_B), rows),
            pl.BlockSpec((W_A + W_B, D_MODEL), const),
            pl.BlockSpec((tm, D_MODEL), rows),
            pl.BlockSpec((1, D_MODEL), const),
            pl.BlockSpec((1, D_MODEL), const),
        ],
        out_specs=pl.BlockSpec((tm, D_MODEL), rows),
        out_shape=jax.ShapeDtypeStruct((m, D_MODEL), F32),
        compiler_params=pltpu.CompilerParams(
            dimension_semantics=("arbitrary",), vmem_limit_bytes=VMEM_LIMIT_BYTES),
        name="out_proj_ln",
    )(ya, yb, w_out, x, ln_g, ln_b)


def _rope_tables(pos):
    half = DK_B // 2
    inv = ROPE_BASE ** (-jnp.arange(half, dtype=F32) / half)
    ang = pos.astype(F32)[:, None] * inv[None, :]
    cos = jnp.cos(ang)
    sin = jnp.sin(ang)
    return jnp.concatenate([cos, cos], axis=-1), jnp.concatenate([-sin, sin], axis=-1)


def _tile(m, candidates):
    for t in candidates:
        if m % t == 0:
            return t
    return m


def _trunk(x2d, batch, seq, pos0, s_rwkv, s_shift, s_ret, wts, c_rwkv, c_ret, bblk, need_y, ntile=1):
    m = batch * seq
    z = _matmul(x2d, wts["w_in"], _tile(m, (1024, 512, 256)), IN_PROJ_TN)
    ya, sa, shift = _rwkv(z, s_shift.reshape(batch, 1, N_A_COLS), s_rwkv, wts, batch, seq, c_rwkv,
                          bblk, ntile)
    cos, sin = _rope_tables(pos0 + jnp.arange(seq, dtype=jnp.int32))
    yb, sb = _retention(z, cos, sin, s_ret, wts["log_g"], wts["gn_b_g"], wts["gn_b_b"],
                        batch, seq, c_ret, bblk, ntile)
    y = None
    if need_y:
        y = _out_proj(ya, yb, wts["w_out"], x2d, wts["ln_g"], wts["ln_b"],
                      _tile(m, (512, 256)))
    return y, sa, shift, sb


def _weights(w_in, w_out, shift_mix, w0, w_up, a0, a_up, k_k, k_a, r_k, gn_a_g, gn_a_b, gn_b_g, gn_b_b,
             ln_g, ln_b):
    zw = jnp.zeros((LORA_W, W_A), F32)
    wwa = jnp.concatenate([jnp.concatenate([w_up[0], zw], axis=1),
                           jnp.concatenate([zw, a_up[0]], axis=1)], axis=0).astype(BF16)
    lane = jnp.arange(LANES)
    return {
        "w_in": w_in[0].astype(BF16),
        "w_out": w_out[0].astype(BF16),
        "shift_mix": shift_mix,
        "wwa": wwa,
        "w0": w0, "a0": a0, "k_k": k_k, "k_a": k_a, "r_k": r_k.reshape(1, W_A),
        "gn_a_g": gn_a_g, "gn_a_b": gn_a_b, "gn_b_g": gn_b_g, "gn_b_b": gn_b_b,
        "ln_g": ln_g, "ln_b": ln_b,
        "ones_bd": (lane[:, None] // HEAD_A == lane[None, :] // HEAD_A).astype(BF16),
        "log_g": jnp.log1p(-jnp.exp2(-5.0 - jnp.arange(H_B, dtype=F32))),
    }


def kernel(x_prompt, x_sample, state_rwkv, state_shift, state_ret, meta_tokens, w_in, w_out, shift_mix,
           w0, w_up, a0, a_up, k_k, k_a, r_k, gn_a_g, gn_a_b, gn_b_g, gn_b_b, ln_g, ln_b):
    bp, tp, _ = x_prompt.shape
    bs, ts, _ = x_sample.shape
    wts = _weights(w_in, w_out, shift_mix, w0, w_up, a0, a_up, k_k, k_a, r_k, gn_a_g, gn_a_b, gn_b_g,
                   gn_b_b, ln_g, ln_b)
    _, ra_m, sh_m, rb_m = _trunk(
        meta_tokens.astype(F32), 1, N_META, 0,
        jnp.zeros((1, H_A, HEAD_A, HEAD_A), F32), jnp.zeros((1, N_A_COLS), F32),
        jnp.zeros((1, H_B, DK_B, DK_B), F32), wts, N_META, N_META, 1, False)
    rep = lambda s: jnp.broadcast_to(s, (bp,) + s.shape[1:])
    y_p, ra_p, sh_p, rb_p = _trunk(
        x_prompt.reshape(bp * tp, D_MODEL), bp, tp, N_META, rep(ra_m), rep(sh_m), rep(rb_m),
        wts, 64, 128, 1, True, ntile=4)
    y_s, ra_s, sh_s, rb_s = _trunk(
        x_sample.reshape(bs * ts, D_MODEL), bs, ts, PAST_LEN, state_rwkv[0], state_shift[0], state_ret[0],
        wts, ts, ts, 8, True, ntile=2)
    return (y_p.reshape(bp, tp, D_MODEL), y_s.reshape(bs, ts, D_MODEL),
            ra_p[None], sh_p[None], rb_p[None], ra_s[None], sh_s[None], rb_s[None])
```

```python
import functools
import math

import jax
import jax.numpy as jnp
from jax import lax
from jax.experimental import pallas as pl
from jax.experimental.pallas import tpu as pltpu

D_MODEL = 2048
N_META = 16
W_A = 1024
HEAD_A = 64
H_A = W_A // HEAD_A
LORA_W = 64
LORA_A = 64
W_B = D_MODEL - W_A
H_B = 8
DK_B = W_B // H_B
N_A_COLS = 4 * W_A + LORA_W + LORA_A
N_B_COLS = 4 * W_B
N_IN = N_A_COLS + N_B_COLS
N_TAIL = 128
N_MAIN = N_IN - N_TAIL
IN_PROJ_TN = 2048
OUT_SUB_ROWS = 128
DEPTH = 1
PAST_LEN = 16384
ALPHA = (2.0 * DEPTH) ** 0.25
ROPE_BASE = 10000.0
LN_EPS = 1e-5
GN_EPS_A = 64e-5
GN_EPS_B = 1e-5
EXP_NEG_HALF = math.exp(-0.5)

LANES = 128
N_PAIRS = W_A // LANES
VMEM_LIMIT_BYTES = 56 * 1024 * 1024

F32 = jnp.float32
BF16 = jnp.bfloat16


def _dot(a, b):
    return jnp.dot(a, b, preferred_element_type=F32)


def _dot_nt(a, b):
    return lax.dot_general(a, b, (((1,), (1,)), ((), ())), preferred_element_type=F32)


def _dot_tn(a, b):
    return lax.dot_general(a, b, (((0,), (0,)), ((), ())), preferred_element_type=F32)


def _split_dot(x, w_bf16):
    hi = x.astype(BF16)
    lo = (x - hi.astype(F32)).astype(BF16)
    return _dot(hi, w_bf16) + _dot(lo, w_bf16)


def _sigmoid(x):
    return 1.0 / (1.0 + jnp.exp(-x))


def _batch_loop(n, body):
    if n == 1:
        body(0)
    else:
        lax.fori_loop(0, n, lambda i, carry: (body(i), carry)[1], 0)


def _in_proj_kernel(x_ref, w_ref, w_tail_ref, o_ref, o_tail_ref):
    xb = x_ref[...].astype(BF16)
    o_ref[...] = _dot(xb, w_ref[...])

    @pl.when(pl.program_id(1) == pl.num_programs(1) - 1)
    def _():
        o_tail_ref[...] = _dot(xb, w_tail_ref[...])


def _in_proj(x, w, tm):
    m, k = x.shape
    assert m % tm == 0 and w.shape[1] == N_IN
    return pl.pallas_call(
        _in_proj_kernel,
        grid=(m // tm, N_MAIN // IN_PROJ_TN),
        in_specs=[pl.BlockSpec((tm, k), lambda i, j: (i, 0)),
                  pl.BlockSpec((k, IN_PROJ_TN), lambda i, j: (0, j)),
                  pl.BlockSpec((k, N_TAIL), lambda i, j: (0, N_MAIN // N_TAIL))],
        out_specs=[pl.BlockSpec((tm, IN_PROJ_TN), lambda i, j: (i, j)),
                   pl.BlockSpec((tm, N_TAIL), lambda i, j: (i, 0))],
        out_shape=[jax.ShapeDtypeStruct((m, N_MAIN), F32),
                   jax.ShapeDtypeStruct((m, N_TAIL), F32)],
        compiler_params=pltpu.CompilerParams(
            dimension_semantics=("arbitrary", "arbitrary"), vmem_limit_bytes=VMEM_LIMIT_BYTES),
        name="in_proj",
    )(x, w, w)


def _rwkv_kernel(za_ref, shift_ref, s0_ref, mix_ref, wwa_ref, w0_ref, a0_ref, kk_ref, ka_ref,
                 rk_ref, gng_ref, gnb_ref, ones_ref,
                 ya_ref, s_out_ref, shift_out_ref,
                 zs_scr, prev_scr, state_scr, *, chunk, n_chunks, nseq, ntile):
    C = chunk
    G = nseq
    NT = ntile
    R = G * C
    R2 = 2 * R
    RT = NT * R
    NS = NT * G
    log2c = int(math.log2(C))
    c = pl.program_id(1)

    @pl.when(c == 0)
    def _():
        prev_scr[...] = shift_ref[...]
        state_scr[...] = jnp.zeros(state_scr.shape, F32)
        for p in range(N_PAIRS):
            state_scr[:, p, 0:HEAD_A, 0:HEAD_A] = s0_ref[:, 2 * p]
            state_scr[:, p, HEAD_A:LANES, HEAD_A:LANES] = s0_ref[:, 2 * p + 1]

    ones_bd = ones_ref[...]
    head0 = {n: lax.broadcasted_iota(jnp.int32, (n, LANES), 1) < HEAD_A for n in {R, C}}
    ri =lax.broadcasted_iota(jnp.int32, (RT, RT), 0)
    ci = lax.broadcasted_iota(jnp.int32, (RT, RT), 1)
    tri_incl = (((ri >> log2c) == (ci >> log2c)) & (ci <= ri)).astype(BF16)
    i2 = lax.broadcasted_iota(jnp.int32, (R, R2), 0)
    j2 = lax.broadcasted_iota(jnp.int32, (R, R2), 1) & (R - 1)
    same_seq = (i2 >> log2c) == (j2 >> log2c)
    strict = same_seq & (j2 < i2)
    incl = same_seq & (j2 <= i2)
    eye = (i2 == j2).astype(F32)
    zero = jnp.zeros((R, R2), F32)
    col_head0 = lax.broadcasted_iota(jnp.int32, (R, R2), 1) < R

    pairs = range(N_PAIRS)
    tiles = range(NT)
    pair_slices = [slice(p * LANES, (p + 1) * LANES) for p in pairs]
    tile_rows = [slice(t * R, (t + 1) * R) for t in tiles]
    seq_rows = [slice(g * C, (g + 1) * C) for g in range(G)]
    rows2 = lambda xs: jnp.concatenate(xs, axis=0)

    def stack(x, first_head=None):
        m = head0[x.shape[0]] if first_head is None else first_head
        x = x.astype(BF16)
        z = jnp.zeros_like(x)
        return rows2([jnp.where(m, x, z), jnp.where(m, z, x)])

    def join(parts):
        return parts[0] if len(parts) == 1 else jnp.concatenate(parts, axis=0)

    def seg_sum(x, split):
        tall = rows2([x[:, s] for s in pair_slices])
        tall = _split_dot(tall, ones_bd) if split else _dot(tall.astype(BF16), ones_bd)
        return jnp.concatenate([tall[p * RT:(p + 1) * RT] for p in pairs], axis=1)

    za = join([za_ref[t, 0] for t in tiles])
    first = (lax.broadcasted_iota(jnp.int32, za.shape, 0) & (C - 1)) == 0
    carried = join([jnp.broadcast_to(prev_scr[s], (C, N_A_COLS)) for s in range(NS)])
    prev = jnp.where(first, carried, pltpu.roll(za, 1, axis=0))
    for s in range(NS):
        last = (s % G + 1) * C - 1
        prev_scr[s] = za_ref[s // G, 0, last:last + 1, :]
    zs_scr[...] = za + (prev - za) * mix_ref[...]

    zwx = zs_scr[:, 4 * W_A:4 * W_A + LANES]
    lora_in = jnp.where(lax.broadcasted_iota(jnp.int32, (RT, LANES), 1) < LORA_W, jnp.tanh(zwx), zwx)
    lora = _dot(lora_in.astype(BF16), wwa_ref[...])

    r = zs_scr[:, 0:W_A]
    k = zs_scr[:, W_A:2 * W_A]
    v = zs_scr[:, 2 * W_A:3 * W_A]
    g_gate = zs_scr[:, 3 * W_A:4 * W_A]
    logw = -EXP_NEG_HALF * _sigmoid(w0_ref[...] + lora[:, 0:W_A])
    a = _sigmoid(a0_ref[...] + lora[:, W_A:2 * W_A])
    l_hi = logw.astype(BF16)
    rem = logw - l_hi.astype(F32)
    l_mid = rem.astype(BF16)
    l_lo = (rem - l_mid.astype(F32)).astype(BF16)
    cum = _dot(tri_incl, l_hi) + _dot(tri_incl, l_mid) + _dot(tri_incl, l_lo)
    cum_last = [cum[(s + 1) * C - 1:(s + 1) * C] for s in range(NS)]
    cum_end = cum_last[0] if NS == 1 else jnp.concatenate(
        [jnp.broadcast_to(x, (C, W_A)) for x in cum_last], axis=0)
    kkr = k * kk_ref[...]
    kk = kkr * jnp.minimum(lax.rsqrt(seg_sum(kkr * kkr, False)), 1e12)
    kp = k * (1.0 + (a - 1.0) * ka_ref[...])
    b = kk * a
    inv_p = jnp.exp(-cum)
    p_end = jnp.exp(cum_end - cum)
    decay_end = [jnp.exp(x) for x in cum_last]
    at = -kk * jnp.exp(cum - logw)
    rt = r * jnp.exp(cum)
    bt = b * inv_p
    kt = kp * inv_p
    bh = b * p_end
    kh = kp * p_end

    def recurrence(ids):
        part = lambda x, key: x[tile_rows[key[0]], pair_slices[key[1]]]
        at_p = {i: part(at, i) for i in ids}
        rt_p = {i: part(rt, i) for i in ids}
        v_p = {i: part(v, i) for i in ids}
        bh_p = {i: part(bh, i) for i in ids}
        kh_p = {i: part(kh, i) for i in ids}
        ar = {i: rows2([at_p[i], rt_p[i]]).astype(BF16) for i in ids}
        b_s = {i: stack(part(bt, i)) for i in ids}
        k_s = {i: stack(part(kt, i)) for i in ids}
        v_s = {i: stack(v_p[i]) for i in ids}
        if R2 % LANES == 0:
            sc = {i: _dot_nt(ar[i], rows2([b_s[i], k_s[i]])) for i in ids}
            ab, ak = {i: sc[i][:R, :R2] for i in ids}, {i: sc[i][:R, R2:] for i in ids}
            rb, rk = {i: sc[i][R:, :R2] for i in ids}, {i: sc[i][R:, R2:] for i in ids}
        else:
            ab_rb = {i: _dot_nt(ar[i], b_s[i]) for i in ids}
            ak_rk = {i: _dot_nt(ar[i], k_s[i]) for i in ids}
            ab, rb = {i: ab_rb[i][:R] for i in ids}, {i: ab_rb[i][R:] for i in ids}
            ak, rk = {i: ak_rk[i][:R] for i in ids}, {i: ak_rk[i][R:] for i in ids}
        n_pow = {i: jnp.where(strict, ab[i], zero) for i in ids}
        l_ak = {i: jnp.where(strict, ak[i], zero).astype(BF16) for i in ids}
        m_rb = {i: jnp.where(incl, rb[i], zero).astype(BF16) for i in ids}
        m_rk = {i: jnp.where(incl, rk[i], zero).astype(BF16) for i in ids}
        t_inv = {i: eye + n_pow[i] for i in ids}
        n_pow = {i: _dot(n_pow[i].astype(BF16), stack(n_pow[i], col_head0)) for i in ids}
        for level in range(log2c - 1):
            n_st = {i: stack(n_pow[i], col_head0) for i in ids}
            if level < log2c - 2:
                both = {i: _dot(rows2([t_inv[i], n_pow[i]]).astype(BF16), n_st[i]) for i in ids}
                t_inv = {i: t_inv[i] + both[i][:R] for i in ids}
                n_pow = {i: both[i][R:] for i in ids}
            else:
                t_inv = {i: t_inv[i] + _dot(t_inv[i].astype(BF16), n_st[i]) for i in ids}
        s_old = {(g, i): state_scr[i[0] * G + g, i[1]] for g in range(G) for i in ids}
        s_old_b = {key: val.astype(BF16) for key, val in s_old.items()}
        yo = {(g, i): _dot_nt(rows2([at_p[i][seq_rows[g]], rt_p[i][seq_rows[g]]]).astype(BF16),
                              s_old_b[g, i]) for g in range(G) for i in ids}
        y_state = {i: join([yo[g, i][:C] for g in range(G)]) for i in ids}
        o_state = {i: join([yo[g, i][C:] for g in range(G)]) for i in ids}
        lv = {i: _dot(rows2([l_ak[i], m_rk[i]]), v_s[i]) for i in ids}
        y = {i: y_state[i] + lv[i][:R] for i in ids}
        u = {i: _dot(t_inv[i].astype(BF16), stack(y[i])) for i in ids}
        u_s = {i: stack(u[i]) for i in ids}
        o_p = {i: o_state[i] + _dot(m_rb[i], u_s[i]) + lv[i][R:] for i in ids}
        for g in range(G):
            rows = seq_rows[g]
            for i in ids:
                seq = i[0] * G + g
                if G == 1:
                    uv = rows2([u_s[i], v_s[i]])
                else:
                    uv = rows2([stack(u[i][rows]), stack(v_p[i][rows])])
                bk = rows2([stack(bh_p[i][rows]), stack(kh_p[i][rows])])
                state_scr[seq, i[1]] = (s_old[g, i] * decay_end[seq][:, pair_slices[i[1]]]
                                        + _dot_tn(uv, bk))
        return o_p

    o_p = recurrence([(t, p) for t in tiles for p in pairs])
    out = join([jnp.concatenate([o_p[t, p] for p in pairs], axis=1) for t in tiles])

    inv_n = 1.0 / HEAD_A
    d = out - seg_sum(out, True) * inv_n
    var = seg_sum(d * d, False) * inv_n
    o_n = d * lax.rsqrt(var + GN_EPS_A) * gng_ref[...] + gnb_ref[...]
    bonus = seg_sum(r * kp * rk_ref[...], True) * v
    ya = (o_n + bonus) * (g_gate * _sigmoid(g_gate))
    for t in tiles:
        ya_ref[t, 0] = ya[tile_rows[t]].astype(ya_ref.dtype)

    @pl.when(c == n_chunks - 1)
    def _():
        shift_out_ref[...] = prev_scr[...]
        for p in range(N_PAIRS):
            s_out_ref[:, 2 * p] = state_scr[:, p, 0:HEAD_A, 0:HEAD_A]
            s_out_ref[:, 2 * p + 1] = state_scr[:, p, HEAD_A:LANES, HEAD_A:LANES]


def _rwkv(z, shift, s0, wts, batch, seq, chunk, nseq, ntile):
    n_chunks = seq // chunk
    assert seq % chunk == 0 and batch % (nseq * ntile) == 0 and (nseq == 1 or n_chunks == 1)
    rows = nseq * chunk
    nsq = nseq * ntile
    const2 = lambda b, c: (0, 0)
    kern = functools.partial(_rwkv_kernel, chunk=chunk, n_chunks=n_chunks, nseq=nseq, ntile=ntile)
    vec = pl.BlockSpec((1, W_A), const2)
    state_spec = pl.BlockSpec((nsq, H_A, HEAD_A, HEAD_A), lambda b, c: (b, 0, 0, 0))
    shift_spec = pl.BlockSpec((nsq, 1, N_A_COLS), lambda b, c: (b, 0, 0))
    ya, s_new, shift_new = pl.pallas_call(
        kern,
        grid=(batch // nsq, n_chunks),
        in_specs=[
            pl.BlockSpec((ntile, 1, rows, N_A_COLS), lambda b, c: (b, c, 0, 0)),
            shift_spec,
            state_spec,
            pl.BlockSpec((1, N_A_COLS), const2),
            pl.BlockSpec((LANES, 2 * W_A), const2),
            vec, vec, vec, vec, vec, vec, vec,
            pl.BlockSpec((LANES, LANES), const2),
        ],
        out_specs=[
            pl.BlockSpec((ntile, 1, rows, W_A), lambda b, c: (b, c, 0, 0)),
            state_spec,
            shift_spec,
        ],
        out_shape=[
            jax.ShapeDtypeStruct((batch // nseq, n_chunks, rows, W_A), BF16),
            jax.ShapeDtypeStruct((batch, H_A, HEAD_A, HEAD_A), F32),
            jax.ShapeDtypeStruct((batch, 1, N_A_COLS), F32),
        ],
        scratch_shapes=[
            pltpu.VMEM((ntile * rows, N_A_COLS), F32),
            pltpu.VMEM((nsq, 1, N_A_COLS), F32),
            pltpu.VMEM((nsq, N_PAIRS, LANES, LANES), F32),
        ],
        compiler_params=pltpu.CompilerParams(
            dimension_semantics=("arbitrary", "arbitrary"), vmem_limit_bytes=VMEM_LIMIT_BYTES),
        name="rwkv7_mix",
    )(z.reshape(batch // nseq, n_chunks, rows, N_MAIN), shift, s0, wts["shift_mix"], wts["wwa"], wts["w0"],
      wts["a0"], wts["k_k"], wts["k_a"], wts["r_k"], wts["gn_a_g"], wts["gn_a_b"], wts["ones_bd"])
    return ya.reshape(batch * seq, W_A), s_new, shift_new.reshape(batch, N_A_COLS)


def _ret_kernel(lg_ref, lgv_ref, zb_ref, zt_ref, cos_ref, sin_ref, s0_ref, gng_ref, gnb_ref,
                yb_ref, s_out_ref, *scratch, chunk, n_chunks, nseq, ntile):
    C = chunk
    G = nseq
    NT = ntile
    R = G * C
    log2c = int(math.log2(C))
    c = pl.program_id(1)
    if n_chunks == 1:
        s_in, s_new = s0_ref, s_out_ref
    else:
        (state_scr,) = scratch
        s_in = s_new = state_scr

        @pl.when(c == 0)
        def _():
            state_scr[...] = s0_ref[...]

    heads = range(H_B)
    tiles = range(NT)
    head_slices = [slice(h * DK_B, (h + 1) * DK_B) for h in heads]
    seq_rows = [slice(g * C, (g + 1) * C) for g in range(G)]
    join = lambda parts: parts[0] if len(parts) == 1 else jnp.concatenate(parts, axis=0)
    cos = jnp.concatenate([join([cos_ref[...]] * G)] * H_B, axis=1)
    sin = jnp.concatenate([join([sin_ref[...]] * G)] * H_B, axis=1)
    lgv = lgv_ref[...]
    tok = (lax.broadcasted_iota(jnp.int32, (R, W_B), 0) & (C - 1)).astype(F32)
    q_decay = jnp.exp((tok + 1.0) * lgv)
    k_decay = jnp.exp((C - 1.0 - tok) * lgv) * (DK_B ** -0.5)
    ri = lax.broadcasted_iota(jnp.int32, (R, R), 0)
    ci = lax.broadcasted_iota(jnp.int32, (R, R), 1)
    causal = ((ri >> log2c) == (ci >> log2c)) & (ci <= ri)
    diff_f = jnp.maximum(ri - ci, 0).astype(F32)
    dmask = [jnp.where(causal, jnp.exp(diff_f * lg_ref[h]), 0.0) for h in heads]
    s_decay = jnp.exp(C * lgv)

    def rot_half(x):
        return jnp.concatenate([pltpu.roll(x[:, s], DK_B // 2, axis=1) for s in head_slices], axis=1)

    qr_b, kr_b, q_dec, k_dec, v_b, gate = {}, {}, {}, {}, {}, {}
    for t in tiles:
        q = zb_ref[t, 0, :, 0:W_B]
        k = zb_ref[t, 0, :, W_B:2 * W_B]
        g = jnp.concatenate([zb_ref[t, 0, :, 3 * W_B:4 * W_B - N_TAIL], zt_ref[t, 0]], axis=1)
        qr = q * cos + rot_half(q) * sin
        kr = k * cos + rot_half(k) * sin
        qr_b[t] = qr.astype(BF16)
        kr_b[t] = (kr * (DK_B ** -0.5)).astype(BF16)
        q_dec[t] = (qr * q_decay).astype(BF16)
        k_dec[t] = (kr * k_decay).astype(BF16)
        v_b[t] = zb_ref[t, 0, :, 2 * W_B:3 * W_B].astype(BF16)
        gate[t] = g * _sigmoid(g)

    ids = [(t, h) for t in tiles for h in heads]
    hs = {i: head_slices[i[1]] for i in ids}
    scores = {i: (_dot_nt(qr_b[i[0]][:, hs[i]], kr_b[i[0]][:, hs[i]]) * dmask[i[1]]).astype(BF16)
              for i in ids}
    s_old = {(g, i): s_in[i[0] * G + g, i[1]] for g in range(G) for i in ids}
    cross = {i: join([_dot(q_dec[i[0]][seq_rows[g], hs[i]], s_old[g, i].astype(BF16)) for g in range(G)])
             for i in ids}
    out = {i: _dot(scores[i], v_b[i[0]][:, hs[i]]) + cross[i] for i in ids}
    for g in range(G):
        for i in ids:
            s_new[i[0] * G + g, i[1]] = (s_old[g, i] * s_decay[:, hs[i]]
                                         + _dot_tn(k_dec[i[0]][seq_rows[g], hs[i]],
                                                   v_b[i[0]][seq_rows[g], hs[i]]))
    d = {i: out[i] - jnp.mean(out[i], axis=-1, keepdims=True) for i in ids}
    rstd = {i: lax.rsqrt(jnp.mean(d[i] * d[i], axis=-1, keepdims=True) + GN_EPS_B) for i in ids}
    for t in tiles:
        o_n = jnp.concatenate([d[t, h] * rstd[t, h] for h in heads], axis=1) * gng_ref[...] + gnb_ref[...]
        yb_ref[t, 0] = (o_n * gate[t]).astype(yb_ref.dtype)

    if n_chunks > 1:
        @pl.when(c == n_chunks - 1)
        def _():
            s_out_ref[...] = state_scr[...]


def _retention(z, z_tail, cos, sin, s0, log_g, gn_g, gn_b, batch, seq, chunk, nseq, ntile):
    n_chunks = seq // chunk
    assert seq % chunk == 0 and batch % (nseq * ntile) == 0 and (nseq == 1 or n_chunks == 1)
    rows = nseq * chunk
    nsq = nseq * ntile
    kern = functools.partial(_ret_kernel, chunk=chunk, n_chunks=n_chunks, nseq=nseq, ntile=ntile)
    state_spec = pl.BlockSpec((nsq, H_B, DK_B, DK_B), lambda b, c: (b, 0, 0, 0))
    yb, s_new = pl.pallas_call(
        kern,
        grid=(batch // nsq, n_chunks),
        in_specs=[
            pl.BlockSpec(memory_space=pltpu.SMEM),
            pl.BlockSpec((1, W_B), lambda b, c: (0, 0)),
            pl.BlockSpec((ntile, 1, rows, N_A_COLS), lambda b, c: (b, c, 0, 1)),
            pl.BlockSpec((ntile, 1, rows, N_TAIL), lambda b, c: (b, c, 0, 0)),
            pl.BlockSpec((chunk, DK_B), lambda b, c: (c, 0)),
            pl.BlockSpec((chunk, DK_B), lambda b, c: (c, 0)),
            state_spec,
            pl.BlockSpec((1, W_B), lambda b, c: (0, 0)),
            pl.BlockSpec((1, W_B), lambda b, c: (0, 0)),
        ],
        out_specs=[
            pl.BlockSpec((ntile, 1, rows, W_B), lambda b, c: (b, c, 0, 0)),
            state_spec,
        ],
        out_shape=[
            jax.ShapeDtypeStruct((batch // nseq, n_chunks, rows, W_B), BF16),
            jax.ShapeDtypeStruct((batch, H_B, DK_B, DK_B), F32),
        ],
        scratch_shapes=[pltpu.VMEM((nsq, H_B, DK_B, DK_B), F32)] if n_chunks > 1 else [],
        compiler_params=pltpu.CompilerParams(
            dimension_semantics=("arbitrary", "arbitrary"), vmem_limit_bytes=VMEM_LIMIT_BYTES),
        name="retention",
    )(log_g, jnp.repeat(log_g, DK_B).reshape(1, W_B), z.reshape(batch // nseq, n_chunks, rows, N_MAIN),
      z_tail.reshape(batch // nseq, n_chunks, rows, N_TAIL), cos, sin, s0, gn_g, gn_b)
    return yb.reshape(batch * seq, W_B), s_new


def _out_kernel(ya_ref, yb_ref, w_ref, x_ref, g_ref, b_ref, y_ref):
    tm = x_ref.shape[0]
    sub = min(tm, OUT_SUB_ROWS)
    blocks = [slice(i, i + sub) for i in range(0, tm, sub)]
    hs = [_dot(ya_ref[rows, :], w_ref[0:W_A, :]) + _dot(yb_ref[rows, :], w_ref[W_A:W_A + W_B, :])
          for rows in blocks]
    for rows, h in zip(blocks, hs):
        t = ALPHA * x_ref[rows, :] + h
        mu = jnp.mean(t, axis=-1, keepdims=True)
        d = t - mu
        var = jnp.mean(d * d, axis=-1, keepdims=True)
        y_ref[rows, :] = d * lax.rsqrt(var + LN_EPS) * g_ref[...] + b_ref[...]


def _out_proj(ya, yb, w_out, x, ln_g, ln_b, tm):
    m = x.shape[0]
    assert m % tm == 0
    rows = lambda i: (i, 0)
    const = lambda i: (0, 0)
    return pl.pallas_call(
        _out_kernel,
        grid=(m // tm,),
        in_specs=[
            pl.BlockSpec((tm, W_A), rows),
            pl.BlockSpec((tm, W_B), rows),
            pl.BlockSpec((W_A + W_B, D_MODEL), const),
            pl.BlockSpec((tm, D_MODEL), rows),
            pl.BlockSpec((1, D_MODEL), const),
            pl.BlockSpec((1, D_MODEL), const),
        ],
        out_specs=pl.BlockSpec((tm, D_MODEL), rows),
        out_shape=jax.ShapeDtypeStruct((m, D_MODEL), F32),
        compiler_params=pltpu.CompilerParams(
            dimension_semantics=("arbitrary",), vmem_limit_bytes=VMEM_LIMIT_BYTES),
        name="out_proj_ln",
    )(ya, yb, w_out, x, ln_g, ln_b)


def _rope_tables(pos):
    half = DK_B // 2
    inv = ROPE_BASE ** (-jnp.arange(half, dtype=F32) / half)
    ang = pos.astype(F32)[:, None] * inv[None, :]
    cos = jnp.cos(ang)
    sin = jnp.sin(ang)
    return jnp.concatenate([cos, cos], axis=-1), jnp.concatenate([-sin, sin], axis=-1)


def _tile(m, candidates):
    for t in candidates:
        if m % t == 0:
            return t
    return m


def _trunk(x2d, batch, seq, pos0, s_rwkv, s_shift, s_ret, wts, c_rwkv, c_ret, bblk, need_y, ntile=1):
    m = batch * seq
    z, z_tail = _in_proj(x2d, wts["w_in"], _tile(m, (1024, 512, 256)))
    ya, sa, shift = _rwkv(z, s_shift.reshape(batch, 1, N_A_COLS), s_rwkv, wts, batch, seq, c_rwkv,
                          bblk, ntile)
    cos, sin = _rope_tables(pos0 + jnp.arange(seq, dtype=jnp.int32))
    yb, sb = _retention(z, z_tail, cos, sin, s_ret, wts["log_g"], wts["gn_b_g"], wts["gn_b_b"],
                        batch, seq, c_ret, bblk, ntile)
    y = None
    if need_y:
        y = _out_proj(ya, yb, wts["w_out"], x2d, wts["ln_g"], wts["ln_b"],
                      _tile(m, (512, 256)))
    return y, sa, shift, sb


def _weights(w_in, w_out, shift_mix, w0, w_up, a0, a_up, k_k, k_a, r_k, gn_a_g, gn_a_b, gn_b_g, gn_b_b,
             ln_g, ln_b):
    zw = jnp.zeros((LORA_W, W_A), F32)
    wwa = jnp.concatenate([jnp.concatenate([w_up[0], zw], axis=1),
                           jnp.concatenate([zw, a_up[0]], axis=1)], axis=0).astype(BF16)
    lane = jnp.arange(LANES)
    return {
        "w_in": w_in[0].astype(BF16),
        "w_out": w_out[0].astype(BF16),
        "shift_mix": shift_mix,
        "wwa": wwa,
        "w0": w0, "a0": a0, "k_k": k_k, "k_a": k_a, "r_k": r_k.reshape(1, W_A),
        "gn_a_g": gn_a_g, "gn_a_b": gn_a_b, "gn_b_g": gn_b_g, "gn_b_b": gn_b_b,
        "ln_g": ln_g, "ln_b": ln_b,
        "ones_bd": (lane[:, None] // HEAD_A == lane[None, :] // HEAD_A).astype(BF16),
        "log_g": jnp.log1p(-jnp.exp2(-5.0 - jnp.arange(H_B, dtype=F32))),
    }


def kernel(x_prompt, x_sample, state_rwkv, state_shift, state_ret, meta_tokens, w_in, w_out, shift_mix,
           w0, w_up, a0, a_up, k_k, k_a, r_k, gn_a_g, gn_a_b, gn_b_g, gn_b_b, ln_g, ln_b):
    bp, tp, _ = x_prompt.shape
    bs, ts, _ = x_sample.shape
    wts = _weights(w_in, w_out, shift_mix, w0, w_up, a0, a_up, k_k, k_a, r_k, gn_a_g, gn_a_b, gn_b_g,
                   gn_b_b, ln_g, ln_b)
    _, ra_m, sh_m, rb_m = _trunk(
        meta_tokens.astype(F32), 1, N_META, 0,
        jnp.zeros((1, H_A, HEAD_A, HEAD_A), F32), jnp.zeros((1, N_A_COLS), F32),
        jnp.zeros((1, H_B, DK_B, DK_B), F32), wts, N_META, N_META, 1, False)
    rep = lambda s: jnp.broadcast_to(s, (bp,) + s.shape[1:])
    y_p, ra_p, sh_p, rb_p = _trunk(
        x_prompt.reshape(bp * tp, D_MODEL), bp, tp, N_META, rep(ra_m), rep(sh_m), rep(rb_m),
        wts, 64, 128, 1, True, ntile=4)
    y_s, ra_s, sh_s, rb_s = _trunk(
        x_sample.reshape(bs * ts, D_MODEL), bs, ts, PAST_LEN, state_rwkv[0], state_shift[0], state_ret[0],
        wts, ts, ts, 8, True, ntile=2)
    return (y_p.reshape(bp, tp, D_MODEL), y_s.reshape(bs, ts, D_MODEL),
            ra_p[None], sh_p[None], rb_p[None], ra_s[None], sh_s[None], rb_s[None])
```

```python
import functools
import math

import jax
import jax.numpy as jnp
from jax import lax
from jax.experimental import pallas as pl
from jax.experimental.pallas import tpu as pltpu

D_MODEL = 2048
N_META = 16
W_A = 1024
HEAD_A = 64
H_A = W_A // HEAD_A
LORA_W = 64
LORA_A = 64
W_B = D_MODEL - W_A
H_B = 8
DK_B = W_B // H_B
N_A_COLS = 4 * W_A + LORA_W + LORA_A
N_B_COLS = 4 * W_B
N_IN = N_A_COLS + N_B_COLS
N_TAIL = 128
N_MAIN = N_IN - N_TAIL
IN_PROJ_TN = 2048
OUT_SUB_ROWS = 128
SAMPLE_SEQS_PER_TILE = 8
DEPTH = 1
PAST_LEN = 16384
ALPHA = (2.0 * DEPTH) ** 0.25
ROPE_BASE = 10000.0
LN_EPS = 1e-5
GN_EPS_A = 64e-5
GN_EPS_B = 1e-5
EXP_NEG_HALF = math.exp(-0.5)

LANES = 128
N_PAIRS = W_A // LANES
VMEM_LIMIT_BYTES = 56 * 1024 * 1024

F32 = jnp.float32
BF16 = jnp.bfloat16


def _dot(a, b):
    return jnp.dot(a, b, preferred_element_type=F32)


def _dot_nt(a, b):
    return lax.dot_general(a, b, (((1,), (1,)), ((), ())), preferred_element_type=F32)


def _dot_tn(a, b):
    return lax.dot_general(a, b, (((0,), (0,)), ((), ())), preferred_element_type=F32)


def _split_dot(x, w_bf16):
    hi = x.astype(BF16)
    lo = (x - hi.astype(F32)).astype(BF16)
    return _dot(hi, w_bf16) + _dot(lo, w_bf16)


def _sigmoid(x):
    return 1.0 / (1.0 + jnp.exp(-x))


def _batch_loop(n, body):
    if n == 1:
        body(0)
    else:
        lax.fori_loop(0, n, lambda i, carry: (body(i), carry)[1], 0)


def _in_proj_kernel(x_ref, w_ref, w_tail_ref, o_ref, o_tail_ref):
    xb = x_ref[...].astype(BF16)
    o_ref[...] = _dot(xb, w_ref[...])

    @pl.when(pl.program_id(1) == pl.num_programs(1) - 1)
    def _():
        o_tail_ref[...] = _dot(xb, w_tail_ref[...])


def _in_proj(x, w, tm):
    m, k = x.shape
    assert m % tm == 0 and w.shape[1] == N_IN
    return pl.pallas_call(
        _in_proj_kernel,
        grid=(m // tm, N_MAIN // IN_PROJ_TN),
        in_specs=[pl.BlockSpec((tm, k), lambda i, j: (i, 0)),
                  pl.BlockSpec((k, IN_PROJ_TN), lambda i, j: (0, j)),
                  pl.BlockSpec((k, N_TAIL), lambda i, j: (0, N_MAIN // N_TAIL))],
        out_specs=[pl.BlockSpec((tm, IN_PROJ_TN), lambda i, j: (i, j)),
                   pl.BlockSpec((tm, N_TAIL), lambda i, j: (i, 0))],
        out_shape=[jax.ShapeDtypeStruct((m, N_MAIN), F32),
                   jax.ShapeDtypeStruct((m, N_TAIL), F32)],
        compiler_params=pltpu.CompilerParams(
            dimension_semantics=("arbitrary", "arbitrary"), vmem_limit_bytes=VMEM_LIMIT_BYTES),
        name="in_proj",
    )(x, w, w)


def _rwkv_kernel(za_ref, shift_ref, s0_ref, mix_ref, wwa_ref, w0_ref, a0_ref, kk_ref, ka_ref,
                 rk_ref, gng_ref, gnb_ref, ones_ref,
                 ya_ref, s_out_ref, shift_out_ref,
                 zs_scr, prev_scr, state_scr, *, chunk, n_chunks, nseq, ntile):
    C = chunk
    G = nseq
    NT = ntile
    R = G * C
    R2 = 2 * R
    RT = NT * R
    NS = NT * G
    log2c = int(math.log2(C))
    c = pl.program_id(1)

    @pl.when(c == 0)
    def _():
        prev_scr[...] = shift_ref[...]
        state_scr[...] = jnp.zeros(state_scr.shape, F32)
        for p in range(N_PAIRS):
            state_scr[:, p, 0:HEAD_A, 0:HEAD_A] = s0_ref[:, 2 * p]
            state_scr[:, p, HEAD_A:LANES, HEAD_A:LANES] = s0_ref[:, 2 * p + 1]

    ones_bd = ones_ref[...]
    head0 = {n: lax.broadcasted_iota(jnp.int32, (n, LANES), 1) < HEAD_A for n in {R, C}}
    ri =lax.broadcasted_iota(jnp.int32, (RT, RT), 0)
    ci = lax.broadcasted_iota(jnp.int32, (RT, RT), 1)
    tri_incl = (((ri >> log2c) == (ci >> log2c)) & (ci <= ri)).astype(BF16)
    i2 = lax.broadcasted_iota(jnp.int32, (R, R2), 0)
    j2 = lax.broadcasted_iota(jnp.int32, (R, R2), 1) & (R - 1)
    same_seq = (i2 >> log2c) == (j2 >> log2c)
    strict = same_seq & (j2 < i2)
    incl = same_seq & (j2 <= i2)
    eye = (i2 == j2).astype(F32)
    zero = jnp.zeros((R, R2), F32)
    col_head0 = lax.broadcasted_iota(jnp.int32, (R, R2), 1) < R

    pairs = range(N_PAIRS)
    tiles = range(NT)
    pair_slices = [slice(p * LANES, (p + 1) * LANES) for p in pairs]
    tile_rows = [slice(t * R, (t + 1) * R) for t in tiles]
    seq_rows = [slice(g * C, (g + 1) * C) for g in range(G)]
    rows2 = lambda xs: jnp.concatenate(xs, axis=0)

    def stack(x, first_head=None):
        m = head0[x.shape[0]] if first_head is None else first_head
        x = x.astype(BF16)
        z = jnp.zeros_like(x)
        return rows2([jnp.where(m, x, z), jnp.where(m, z, x)])

    def join(parts):
        return parts[0] if len(parts) == 1 else jnp.concatenate(parts, axis=0)

    def seg_sum(x, split):
        tall = rows2([x[:, s] for s in pair_slices])
        tall = _split_dot(tall, ones_bd) if split else _dot(tall.astype(BF16), ones_bd)
        return jnp.concatenate([tall[p * RT:(p + 1) * RT] for p in pairs], axis=1)

    za = join([za_ref[t, 0] for t in tiles])
    first = (lax.broadcasted_iota(jnp.int32, za.shape, 0) & (C - 1)) == 0
    carried = join([jnp.broadcast_to(prev_scr[s], (C, N_A_COLS)) for s in range(NS)])
    prev = jnp.where(first, carried, pltpu.roll(za, 1, axis=0))
    for s in range(NS):
        last = (s % G + 1) * C - 1
        prev_scr[s] = za_ref[s // G, 0, last:last + 1, :]
    zs_scr[...] = za + (prev - za) * mix_ref[...]

    zwx = zs_scr[:, 4 * W_A:4 * W_A + LANES]
    lora_in = jnp.where(lax.broadcasted_iota(jnp.int32, (RT, LANES), 1) < LORA_W, jnp.tanh(zwx), zwx)
    lora = _dot(lora_in.astype(BF16), wwa_ref[...])

    r = zs_scr[:, 0:W_A]
    k = zs_scr[:, W_A:2 * W_A]
    v = zs_scr[:, 2 * W_A:3 * W_A]
    g_gate = zs_scr[:, 3 * W_A:4 * W_A]
    logw = -EXP_NEG_HALF * _sigmoid(w0_ref[...] + lora[:, 0:W_A])
    a = _sigmoid(a0_ref[...] + lora[:, W_A:2 * W_A])
    l_hi = logw.astype(BF16)
    rem = logw - l_hi.astype(F32)
    l_mid = rem.astype(BF16)
    l_lo = (rem - l_mid.astype(F32)).astype(BF16)
    cum = _dot(tri_incl, l_hi) + _dot(tri_incl, l_mid) + _dot(tri_incl, l_lo)
    cum_last = [cum[(s + 1) * C - 1:(s + 1) * C] for s in range(NS)]
    cum_end = cum_last[0] if NS == 1 else jnp.concatenate(
        [jnp.broadcast_to(x, (C, W_A)) for x in cum_last], axis=0)
    kkr = k * kk_ref[...]
    kk = kkr * jnp.minimum(lax.rsqrt(seg_sum(kkr * kkr, False)), 1e12)
    kp = k * (1.0 + (a - 1.0) * ka_ref[...])
    b = kk * a
    inv_p = jnp.exp(-cum)
    p_end = jnp.exp(cum_end - cum)
    decay_end = [jnp.exp(x) for x in cum_last]
    at = -kk * jnp.exp(cum - logw)
    rt = r * jnp.exp(cum)
    bt = b * inv_p
    kt = kp * inv_p
    bh = b * p_end
    kh = kp * p_end

    def recurrence(ids):
        part = lambda x, key: x[tile_rows[key[0]], pair_slices[key[1]]]
        at_p = {i: part(at, i) for i in ids}
        rt_p = {i: part(rt, i) for i in ids}
        v_p = {i: part(v, i) for i in ids}
        bh_p = {i: part(bh, i) for i in ids}
        kh_p = {i: part(kh, i) for i in ids}
        ar = {i: rows2([at_p[i], rt_p[i]]).astype(BF16) for i in ids}
        b_s = {i: stack(part(bt, i)) for i in ids}
        k_s = {i: stack(part(kt, i)) for i in ids}
        v_s = {i: stack(v_p[i]) for i in ids}
        if R2 % LANES == 0:
            sc = {i: _dot_nt(ar[i], rows2([b_s[i], k_s[i]])) for i in ids}
            ab, ak = {i: sc[i][:R, :R2] for i in ids}, {i: sc[i][:R, R2:] for i in ids}
            rb, rk = {i: sc[i][R:, :R2] for i in ids}, {i: sc[i][R:, R2:] for i in ids}
        else:
            ab_rb = {i: _dot_nt(ar[i], b_s[i]) for i in ids}
            ak_rk = {i: _dot_nt(ar[i], k_s[i]) for i in ids}
            ab, rb = {i: ab_rb[i][:R] for i in ids}, {i: ab_rb[i][R:] for i in ids}
            ak, rk = {i: ak_rk[i][:R] for i in ids}, {i: ak_rk[i][R:] for i in ids}
        n_pow = {i: jnp.where(strict, ab[i], zero) for i in ids}
        l_ak = {i: jnp.where(strict, ak[i], zero).astype(BF16) for i in ids}
        m_rb = {i: jnp.where(incl, rb[i], zero).astype(BF16) for i in ids}
        m_rk = {i: jnp.where(incl, rk[i], zero).astype(BF16) for i in ids}
        t_inv = {i: eye + n_pow[i] for i in ids}
        n_pow = {i: _dot(n_pow[i].astype(BF16), stack(n_pow[i], col_head0)) for i in ids}
        for level in range(log2c - 1):
            n_st = {i: stack(n_pow[i], col_head0) for i in ids}
            if level < log2c - 2:
                both = {i: _dot(rows2([t_inv[i], n_pow[i]]).astype(BF16), n_st[i]) for i in ids}
                t_inv = {i: t_inv[i] + both[i][:R] for i in ids}
                n_pow = {i: both[i][R:] for i in ids}
            else:
                t_inv = {i: t_inv[i] + _dot(t_inv[i].astype(BF16), n_st[i]) for i in ids}
        s_old = {(g, i): state_scr[i[0] * G + g, i[1]] for g in range(G) for i in ids}
        s_old_b = {key: val.astype(BF16) for key, val in s_old.items()}
        yo = {(g, i): _dot_nt(rows2([at_p[i][seq_rows[g]], rt_p[i][seq_rows[g]]]).astype(BF16),
                              s_old_b[g, i]) for g in range(G) for i in ids}
        y_state = {i: join([yo[g, i][:C] for g in range(G)]) for i in ids}
        o_state = {i: join([yo[g, i][C:] for g in range(G)]) for i in ids}
        lv = {i: _dot(rows2([l_ak[i], m_rk[i]]), v_s[i]) for i in ids}
        y = {i: y_state[i] + lv[i][:R] for i in ids}
        u = {i: _dot(t_inv[i].astype(BF16), stack(y[i])) for i in ids}
        u_s = {i: stack(u[i]) for i in ids}
        o_p = {i: o_state[i] + _dot(m_rb[i], u_s[i]) + lv[i][R:] for i in ids}
        for g in range(G):
            rows = seq_rows[g]
            for i in ids:
                seq = i[0] * G + g
                if G == 1:
                    uv = rows2([u_s[i], v_s[i]])
                else:
                    uv = rows2([stack(u[i][rows]), stack(v_p[i][rows])])
                bk = rows2([stack(bh_p[i][rows]), stack(kh_p[i][rows])])
                state_scr[seq, i[1]] = (s_old[g, i] * decay_end[seq][:, pair_slices[i[1]]]
                                        + _dot_tn(uv, bk))
        return o_p

    o_p = recurrence([(t, p) for t in tiles for p in pairs])
    out = join([jnp.concatenate([o_p[t, p] for p in pairs], axis=1) for t in tiles])

    inv_n = 1.0 / HEAD_A
    d = out - seg_sum(out, True) * inv_n
    var = seg_sum(d * d, False) * inv_n
    o_n = d * lax.rsqrt(var + GN_EPS_A) * gng_ref[...] + gnb_ref[...]
    bonus = seg_sum(r * kp * rk_ref[...], True) * v
    ya = (o_n + bonus) * (g_gate * _sigmoid(g_gate))
    for t in tiles:
        ya_ref[t, 0] = ya[tile_rows[t]].astype(ya_ref.dtype)

    @pl.when(c == n_chunks - 1)
    def _():
        shift_out_ref[...] = prev_scr[...]
        for p in range(N_PAIRS):
            s_out_ref[:, 2 * p] = state_scr[:, p, 0:HEAD_A, 0:HEAD_A]
            s_out_ref[:, 2 * p + 1] = state_scr[:, p, HEAD_A:LANES, HEAD_A:LANES]


def _rwkv(z, shift, s0, wts, batch, seq, chunk, nseq, ntile, tile0=0):
    n_chunks = seq // chunk
    assert seq % chunk == 0 and batch % (nseq * ntile) == 0 and (nseq == 1 or n_chunks == 1)
    rows = nseq * chunk
    nsq = nseq * ntile
    const2 = lambda b, c: (0, 0)
    kern = functools.partial(_rwkv_kernel, chunk=chunk, n_chunks=n_chunks, nseq=nseq, ntile=ntile)
    vec = pl.BlockSpec((1, W_A), const2)
    state_spec = pl.BlockSpec((nsq, H_A, HEAD_A, HEAD_A), lambda b, c: (b, 0, 0, 0))
    shift_spec = pl.BlockSpec((nsq, 1, N_A_COLS), lambda b, c: (b, 0, 0))
    ya, s_new, shift_new = pl.pallas_call(
        kern,
        grid=(batch // nsq, n_chunks),
        in_specs=[
            pl.BlockSpec((ntile, 1, rows, N_A_COLS), lambda b, c: (b + tile0, c, 0, 0)),
            shift_spec,
            state_spec,
            pl.BlockSpec((1, N_A_COLS), const2),
            pl.BlockSpec((LANES, 2 * W_A), const2),
            vec, vec, vec, vec, vec, vec, vec,
            pl.BlockSpec((LANES, LANES), const2),
        ],
        out_specs=[
            pl.BlockSpec((ntile, 1, rows, W_A), lambda b, c: (b, c, 0, 0)),
            state_spec,
            shift_spec,
        ],
        out_shape=[
            jax.ShapeDtypeStruct((batch // nseq, n_chunks, rows, W_A), BF16),
            jax.ShapeDtypeStruct((batch, H_A, HEAD_A, HEAD_A), F32),
            jax.ShapeDtypeStruct((batch, 1, N_A_COLS), F32),
        ],
        scratch_shapes=[
            pltpu.VMEM((ntile * rows, N_A_COLS), F32),
            pltpu.VMEM((nsq, 1, N_A_COLS), F32),
            pltpu.VMEM((nsq, N_PAIRS, LANES, LANES), F32),
        ],
        compiler_params=pltpu.CompilerParams(
            dimension_semantics=("arbitrary", "arbitrary"), vmem_limit_bytes=VMEM_LIMIT_BYTES),
        name="rwkv7_mix",
    )(z.reshape(-1, n_chunks, rows, N_MAIN), shift, s0, wts["shift_mix"], wts["wwa"], wts["w0"],
      wts["a0"], wts["k_k"], wts["k_a"], wts["r_k"], wts["gn_a_g"], wts["gn_a_b"], wts["ones_bd"])
    return ya.reshape(batch * seq, W_A), s_new, shift_new.reshape(batch, N_A_COLS)


def _ret_kernel(lg_ref, lgv_ref, zb_ref, zt_ref, cos_ref, sin_ref, s0_ref, gng_ref, gnb_ref,
                yb_ref, s_out_ref, *scratch, chunk, n_chunks, nseq, ntile):
    C = chunk
    G = nseq
    NT = ntile
    R = G * C
    log2c = int(math.log2(C))
    c = pl.program_id(1)
    if n_chunks == 1:
        s_in, s_new = s0_ref, s_out_ref
    else:
        (state_scr,) = scratch
        s_in = s_new = state_scr

        @pl.when(c == 0)
        def _():
            state_scr[...] = s0_ref[...]

    heads = range(H_B)
    tiles = range(NT)
    head_slices = [slice(h * DK_B, (h + 1) * DK_B) for h in heads]
    seq_rows = [slice(g * C, (g + 1) * C) for g in range(G)]
    join = lambda parts: parts[0] if len(parts) == 1 else jnp.concatenate(parts, axis=0)
    cos = jnp.concatenate([join([cos_ref[...]] * G)] * H_B, axis=1)
    sin = jnp.concatenate([join([sin_ref[...]] * G)] * H_B, axis=1)
    lgv = lgv_ref[...]
    tok = (lax.broadcasted_iota(jnp.int32, (R, W_B), 0) & (C - 1)).astype(F32)
    q_decay = jnp.exp((tok + 1.0) * lgv)
    k_decay = jnp.exp((C - 1.0 - tok) * lgv) * (DK_B ** -0.5)
    ri = lax.broadcasted_iota(jnp.int32, (R, R), 0)
    ci = lax.broadcasted_iota(jnp.int32, (R, R), 1)
    causal = ((ri >> log2c) == (ci >> log2c)) & (ci <= ri)
    diff_f = jnp.maximum(ri - ci, 0).astype(F32)
    dmask = [jnp.where(causal, jnp.exp(diff_f * lg_ref[h]), 0.0) for h in heads]
    s_decay = jnp.exp(C * lgv)

    def rot_half(x):
        return jnp.concatenate([pltpu.roll(x[:, s], DK_B // 2, axis=1) for s in head_slices], axis=1)

    qr_b, kr_b, q_dec, k_dec, v_b, gate = {}, {}, {}, {}, {}, {}
    for t in tiles:
        q = zb_ref[t, 0, :, 0:W_B]
        k = zb_ref[t, 0, :, W_B:2 * W_B]
        g = jnp.concatenate([zb_ref[t, 0, :, 3 * W_B:4 * W_B - N_TAIL], zt_ref[t, 0]], axis=1)
        qr = q * cos + rot_half(q) * sin
        kr = k * cos + rot_half(k) * sin
        qr_b[t] = qr.astype(BF16)
        kr_b[t] = (kr * (DK_B ** -0.5)).astype(BF16)
        q_dec[t] = (qr * q_decay).astype(BF16)
        k_dec[t] = (kr * k_decay).astype(BF16)
        v_b[t] = zb_ref[t, 0, :, 2 * W_B:3 * W_B].astype(BF16)
        gate[t] = g * _sigmoid(g)

    ids = [(t, h) for t in tiles for h in heads]
    hs = {i: head_slices[i[1]] for i in ids}
    scores = {i: (_dot_nt(qr_b[i[0]][:, hs[i]], kr_b[i[0]][:, hs[i]]) * dmask[i[1]]).astype(BF16)
              for i in ids}
    s_old = {(g, i): s_in[i[0] * G + g, i[1]] for g in range(G) for i in ids}
    cross = {i: join([_dot(q_dec[i[0]][seq_rows[g], hs[i]], s_old[g, i].astype(BF16)) for g in range(G)])
             for i in ids}
    out = {i: _dot(scores[i], v_b[i[0]][:, hs[i]]) + cross[i] for i in ids}
    for g in range(G):
        for i in ids:
            s_new[i[0] * G + g, i[1]] = (s_old[g, i] * s_decay[:, hs[i]]
                                         + _dot_tn(k_dec[i[0]][seq_rows[g], hs[i]],
                                                   v_b[i[0]][seq_rows[g], hs[i]]))
    d = {i: out[i] - jnp.mean(out[i], axis=-1, keepdims=True) for i in ids}
    rstd = {i: lax.rsqrt(jnp.mean(d[i] * d[i], axis=-1, keepdims=True) + GN_EPS_B) for i in ids}
    for t in tiles:
        o_n = jnp.concatenate([d[t, h] * rstd[t, h] for h in heads], axis=1) * gng_ref[...] + gnb_ref[...]
        yb_ref[t, 0] = (o_n * gate[t]).astype(yb_ref.dtype)

    if n_chunks > 1:
        @pl.when(c == n_chunks - 1)
        def _():
            s_out_ref[...] = state_scr[...]


def _retention(z, z_tail, cos, sin, s0, log_g, gn_g, gn_b, batch, seq, chunk, nseq, ntile, tile0=0):
    n_chunks = seq // chunk
    assert seq % chunk == 0 and batch % (nseq * ntile) == 0 and (nseq == 1 or n_chunks == 1)
    rows = nseq * chunk
    nsq = nseq * ntile
    kern = functools.partial(_ret_kernel, chunk=chunk, n_chunks=n_chunks, nseq=nseq, ntile=ntile)
    state_spec = pl.BlockSpec((nsq, H_B, DK_B, DK_B), lambda b, c: (b, 0, 0, 0))
    yb, s_new = pl.pallas_call(
        kern,
        grid=(batch // nsq, n_chunks),
        in_specs=[
            pl.BlockSpec(memory_space=pltpu.SMEM),
            pl.BlockSpec((1, W_B), lambda b, c: (0, 0)),
            pl.BlockSpec((ntile, 1, rows, N_A_COLS), lambda b, c: (b + tile0, c, 0, 1)),
            pl.BlockSpec((ntile, 1, rows, N_TAIL), lambda b, c: (b + tile0, c, 0, 0)),
            pl.BlockSpec((chunk, DK_B), lambda b, c: (c, 0)),
            pl.BlockSpec((chunk, DK_B), lambda b, c: (c, 0)),
            state_spec,
            pl.BlockSpec((1, W_B), lambda b, c: (0, 0)),
            pl.BlockSpec((1, W_B), lambda b, c: (0, 0)),
        ],
        out_specs=[
            pl.BlockSpec((ntile, 1, rows, W_B), lambda b, c: (b, c, 0, 0)),
            state_spec,
        ],
        out_shape=[
            jax.ShapeDtypeStruct((batch // nseq, n_chunks, rows, W_B), BF16),
            jax.ShapeDtypeStruct((batch, H_B, DK_B, DK_B), F32),
        ],
        scratch_shapes=[pltpu.VMEM((nsq, H_B, DK_B, DK_B), F32)] if n_chunks > 1 else [],
        compiler_params=pltpu.CompilerParams(
            dimension_semantics=("arbitrary", "arbitrary"), vmem_limit_bytes=VMEM_LIMIT_BYTES),
        name="retention",
    )(log_g, jnp.repeat(log_g, DK_B).reshape(1, W_B), z.reshape(-1, n_chunks, rows, N_MAIN),
      z_tail.reshape(-1, n_chunks, rows, N_TAIL), cos, sin, s0, gn_g, gn_b)
    return yb.reshape(batch * seq, W_B), s_new


def _out_kernel(ya_ref, yb_ref, w_ref, x_ref, g_ref, b_ref, y_ref):
    tm = x_ref.shape[0]
    sub = min(tm, OUT_SUB_ROWS)
    blocks = [slice(i, i + sub) for i in range(0, tm, sub)]
    hs = [_dot(ya_ref[rows, :], w_ref[0:W_A, :]) + _dot(yb_ref[rows, :], w_ref[W_A:W_A + W_B, :])
          for rows in blocks]
    for rows, h in zip(blocks, hs):
        t = ALPHA * x_ref[rows, :] + h
        mu = jnp.mean(t, axis=-1, keepdims=True)
        d = t - mu
        var = jnp.mean(d * d, axis=-1, keepdims=True)
        y_ref[rows, :] = d * lax.rsqrt(var + LN_EPS) * g_ref[...] + b_ref[...]


def _out_proj(ya, yb, w_out, x, ln_g, ln_b, tm):
    m = x.shape[0]
    assert m % tm == 0
    rows = lambda i: (i, 0)
    const = lambda i: (0, 0)
    return pl.pallas_call(
        _out_kernel,
        grid=(m // tm,),
        in_specs=[
            pl.BlockSpec((tm, W_A), rows),
            pl.BlockSpec((tm, W_B), rows),
            pl.BlockSpec((W_A + W_B, D_MODEL), const),
            pl.BlockSpec((tm, D_MODEL), rows),
            pl.BlockSpec((1, D_MODEL), const),
            pl.BlockSpec((1, D_MODEL), const),
        ],
        out_specs=pl.BlockSpec((tm, D_MODEL), rows),
        out_shape=jax.ShapeDtypeStruct((m, D_MODEL), F32),
        compiler_params=pltpu.CompilerParams(
            dimension_semantics=("arbitrary",), vmem_limit_bytes=VMEM_LIMIT_BYTES),
        name="out_proj_ln",
    )(ya, yb, w_out, x, ln_g, ln_b)


def _rope_tables(pos):
    half = DK_B // 2
    inv = ROPE_BASE ** (-jnp.arange(half, dtype=F32) / half)
    ang = pos.astype(F32)[:, None] * inv[None, :]
    cos = jnp.cos(ang)
    sin = jnp.sin(ang)
    return jnp.concatenate([cos, cos], axis=-1), jnp.concatenate([-sin, sin], axis=-1)


def _tile(m, candidates):
    for t in candidates:
        if m % t == 0:
            return t
    return m


def _trunk(x2d, batch, seq, pos0, s_rwkv, s_shift, s_ret, wts, c_rwkv, c_ret, bblk, need_y, ntile=1,
           proj=None, tile0=0):
    m = batch * seq
    z, z_tail = proj if proj is not None else _in_proj(x2d, wts["w_in"], _tile(m, (1024, 512, 256)))
    ya, sa, shift = _rwkv(z, s_shift.reshape(batch, 1, N_A_COLS), s_rwkv, wts, batch, seq, c_rwkv,
                          bblk, ntile, tile0)
    cos, sin = _rope_tables(pos0 + jnp.arange(seq, dtype=jnp.int32))
    yb, sb = _retention(z, z_tail, cos, sin, s_ret, wts["log_g"], wts["gn_b_g"], wts["gn_b_b"],
                        batch, seq, c_ret, bblk, ntile, tile0)
    y = None
    if need_y:
        y = _out_proj(ya, yb, wts["w_out"], x2d, wts["ln_g"], wts["ln_b"],
                      _tile(m, (512, 256)))
    return y, sa, shift, sb


def _weights(w_in, w_out, shift_mix, w0, w_up, a0, a_up, k_k, k_a, r_k, gn_a_g, gn_a_b, gn_b_g, gn_b_b,
             ln_g, ln_b):
    zw = jnp.zeros((LORA_W, W_A), F32)
    wwa = jnp.concatenate([jnp.concatenate([w_up[0], zw], axis=1),
                           jnp.concatenate([zw, a_up[0]], axis=1)], axis=0).astype(BF16)
    lane = jnp.arange(LANES)
    return {
        "w_in": w_in[0].astype(BF16),
        "w_out": w_out[0].astype(BF16),
        "shift_mix": shift_mix,
        "wwa": wwa,
        "w0": w0, "a0": a0, "k_k": k_k, "k_a": k_a, "r_k": r_k.reshape(1, W_A),
        "gn_a_g": gn_a_g, "gn_a_b": gn_a_b, "gn_b_g": gn_b_g, "gn_b_b": gn_b_b,
        "ln_g": ln_g, "ln_b": ln_b,
        "ones_bd": (lane[:, None] // HEAD_A == lane[None, :] // HEAD_A).astype(BF16),
        "log_g": jnp.log1p(-jnp.exp2(-5.0 - jnp.arange(H_B, dtype=F32))),
    }


def kernel(x_prompt, x_sample, state_rwkv, state_shift, state_ret, meta_tokens, w_in, w_out, shift_mix,
           w0, w_up, a0, a_up, k_k, k_a, r_k, gn_a_g, gn_a_b, gn_b_g, gn_b_b, ln_g, ln_b):
    bp, tp, _ = x_prompt.shape
    bs, ts, _ = x_sample.shape
    wts = _weights(w_in, w_out, shift_mix, w0, w_up, a0, a_up, k_k, k_a, r_k, gn_a_g, gn_a_b, gn_b_g,
                   gn_b_b, ln_g, ln_b)
    x_s = x_sample.reshape(bs * ts, D_MODEL)
    pad = jnp.zeros((SAMPLE_SEQS_PER_TILE * ts - N_META, D_MODEL), F32)
    x_sm = jnp.concatenate([x_s, meta_tokens.astype(F32), pad], axis=0)
    proj_sm = _in_proj(x_sm, wts["w_in"], x_sm.shape[0])
    _, ra_m, sh_m, rb_m = _trunk(
        None, 1, N_META, 0,
        jnp.zeros((1, H_A, HEAD_A, HEAD_A), F32), jnp.zeros((1, N_A_COLS), F32),
        jnp.zeros((1, H_B, DK_B, DK_B), F32), wts, N_META, N_META, 1, False,
        proj=proj_sm, tile0=(bs * ts) // N_META)
    rep = lambda s: jnp.broadcast_to(s, (bp,) + s.shape[1:])
    y_p, ra_p, sh_p, rb_p = _trunk(
        x_prompt.reshape(bp * tp, D_MODEL), bp, tp, N_META, rep(ra_m), rep(sh_m), rep(rb_m),
        wts, 64, 128, 1, True, ntile=4)
    y_s, ra_s, sh_s, rb_s = _trunk(
        x_s, bs, ts, PAST_LEN, state_rwkv[0], state_shift[0], state_ret[0],
        wts, ts, ts, SAMPLE_SEQS_PER_TILE, True, ntile=2, proj=proj_sm)
    return (y_p.reshape(bp, tp, D_MODEL), y_s.reshape(bs, ts, D_MODEL),
            ra_p[None], sh_p[None], rb_p[None], ra_s[None], sh_s[None], rb_s[None])
```

```python
import functools
import math

import jax
import jax.numpy as jnp
from jax import lax
from jax.experimental import pallas as pl
from jax.experimental.pallas import tpu as pltpu

D_MODEL = 2048
N_META = 16
W_A = 1024
HEAD_A = 64
H_A = W_A // HEAD_A
LORA_W = 64
LORA_A = 64
W_B = D_MODEL - W_A
H_B = 8
DK_B = W_B // H_B
N_A_COLS = 4 * W_A + LORA_W + LORA_A
N_B_COLS = 4 * W_B
N_IN = N_A_COLS + N_B_COLS
N_TAIL = 128
N_MAIN = N_IN - N_TAIL
IN_PROJ_TN = 2048
IN_PROJ_CAST_TN = 1024
OUT_SUB_ROWS = 128
SAMPLE_SEQS_PER_TILE = 8
DEPTH = 1
PAST_LEN = 16384
ALPHA = (2.0 * DEPTH) ** 0.25
ROPE_BASE = 10000.0
LN_EPS = 1e-5
GN_EPS_A = 64e-5
GN_EPS_B = 1e-5
EXP_NEG_HALF = math.exp(-0.5)

LANES = 128
N_PAIRS = W_A // LANES
VMEM_LIMIT_BYTES = 56 * 1024 * 1024

F32 = jnp.float32
BF16 = jnp.bfloat16


def _dot(a, b):
    return jnp.dot(a, b, preferred_element_type=F32)


def _dot_nt(a, b):
    return lax.dot_general(a, b, (((1,), (1,)), ((), ())), preferred_element_type=F32)


def _dot_tn(a, b):
    return lax.dot_general(a, b, (((0,), (0,)), ((), ())), preferred_element_type=F32)


def _split_dot(x, w_bf16):
    hi = x.astype(BF16)
    lo = (x - hi.astype(F32)).astype(BF16)
    return _dot(hi, w_bf16) + _dot(lo, w_bf16)


def _sigmoid(x):
    return 1.0 / (1.0 + jnp.exp(-x))


def _in_proj_kernel(x_ref, w_ref, w_tail_ref, o_ref, o_tail_ref):
    xb = x_ref[...].astype(BF16)
    o_ref[...] = _dot(xb, w_ref[...])

    @pl.when(pl.program_id(1) == pl.num_programs(1) - 1)
    def _():
        o_tail_ref[...] = _dot(xb, w_tail_ref[...])


def _in_proj_cast_kernel(x_ref, w_ref, w_tail_ref, o_ref, o_tail_ref, wb_ref, wb_tail_ref):
    xb = x_ref[...].astype(BF16)
    wb = w_ref[...].astype(BF16)
    wb_ref[...] = wb
    o_ref[...] = _dot(xb, wb)

    @pl.when(pl.program_id(1) == pl.num_programs(1) - 1)
    def _():
        wb_tail = w_tail_ref[...].astype(BF16)
        wb_tail_ref[...] = wb_tail
        o_tail_ref[...] = _dot(xb, wb_tail)


def _in_proj(x, w_main, w_tail, tm, tn):
    m, k = x.shape
    cast = w_main.dtype == F32
    assert m % tm == 0 and N_MAIN % tn == 0 and (not cast or m == tm)
    z_shapes = [jax.ShapeDtypeStruct((m, N_MAIN), F32), jax.ShapeDtypeStruct((m, N_TAIL), F32)]
    z_specs = [pl.BlockSpec((tm, tn), lambda i, j: (i, j)),
               pl.BlockSpec((tm, N_TAIL), lambda i, j: (i, 0))]
    w_shapes = [jax.ShapeDtypeStruct((k, N_MAIN), BF16), jax.ShapeDtypeStruct((k, N_TAIL), BF16)]
    w_specs = [pl.BlockSpec((k, tn), lambda i, j: (0, j)),
               pl.BlockSpec((k, N_TAIL), lambda i, j: (0, 0))]
    tail_in = pl.BlockSpec((k, N_TAIL), lambda i, j: (0, N_MAIN // N_TAIL if cast else 0))
    return pl.pallas_call(
        _in_proj_cast_kernel if cast else _in_proj_kernel,
        grid=(m // tm, N_MAIN // tn),
        in_specs=[pl.BlockSpec((tm, k), lambda i, j: (i, 0)), w_specs[0], tail_in],
        out_specs=z_specs + (w_specs if cast else []),
        out_shape=z_shapes + (w_shapes if cast else []),
        compiler_params=pltpu.CompilerParams(
            dimension_semantics=("arbitrary", "arbitrary"), vmem_limit_bytes=VMEM_LIMIT_BYTES),
        name="in_proj",
    )(x, w_main, w_tail)


def _rwkv_kernel(za_ref, shift_ref, s0_ref, mix_ref, wwa_ref, w0_ref, a0_ref, kk_ref, ka_ref,
                 rk_ref, gng_ref, gnb_ref, ones_ref,
                 ya_ref, s_out_ref, shift_out_ref,
                 zs_scr, prev_scr, state_scr, *, chunk, n_chunks, nseq, ntile):
    C = chunk
    G = nseq
    NT = ntile
    R = G * C
    R2 = 2 * R
    RT = NT * R
    NS = NT * G
    log2c = int(math.log2(C))
    c = pl.program_id(1)
    HALF_A = HEAD_A // 2
    half0 = lax.broadcasted_iota(jnp.int32, (HALF_A, LANES), 1) < HEAD_A

    @pl.when(c == 0)
    def _():
        prev_scr[...] = shift_ref[...]
        for s in range(NS):
            for p in range(N_PAIRS):
                x0 = s0_ref[s, 2 * p]
                x1 = s0_ref[s, 2 * p + 1]
                even = jnp.where(half0, x0, pltpu.roll(x1, HEAD_A, axis=1))
                odd = jnp.where(half0, pltpu.roll(x0, HEAD_A, axis=1), x1)
                zh = jnp.zeros_like(even)
                state_scr[s, p, pl.ds(0, HALF_A, stride=2), :] = jnp.where(half0, even, zh)
                state_scr[s, p, pl.ds(1, HALF_A, stride=2), :] = jnp.where(half0, odd, zh)
                state_scr[s, p, pl.ds(HEAD_A, HALF_A, stride=2), :] = jnp.where(half0, zh, even)
                state_scr[s, p, pl.ds(HEAD_A + 1, HALF_A, stride=2), :] = jnp.where(half0, zh, odd)

    ones_bd = ones_ref[...]
    head0 = {n: lax.broadcasted_iota(jnp.int32, (n, LANES), 1) < HEAD_A for n in {R, C}}
    ri =lax.broadcasted_iota(jnp.int32, (RT, RT), 0)
    ci = lax.broadcasted_iota(jnp.int32, (RT, RT), 1)
    tri_incl = (((ri >> log2c) == (ci >> log2c)) & (ci <= ri)).astype(BF16)
    i2 = lax.broadcasted_iota(jnp.int32, (R, R2), 0)
    j2 = lax.broadcasted_iota(jnp.int32, (R, R2), 1) & (R - 1)
    same_seq = (i2 >> log2c) == (j2 >> log2c)
    strict = same_seq & (j2 < i2)
    incl = same_seq & (j2 <= i2)
    eye = (i2 == j2).astype(F32)
    zero = jnp.zeros((R, R2), F32)
    col_head0 = lax.broadcasted_iota(jnp.int32, (R, R2), 1) < R

    pairs = range(N_PAIRS)
    tiles = range(NT)
    pair_slices = [slice(p * LANES, (p + 1) * LANES) for p in pairs]
    tile_rows = [slice(t * R, (t + 1) * R) for t in tiles]
    seq_rows = [slice(g * C, (g + 1) * C) for g in range(G)]
    rows2 = lambda xs: jnp.concatenate(xs, axis=0)

    def stack(x, first_head=None):
        m = head0[x.shape[0]] if first_head is None else first_head
        x = x.astype(BF16)
        z = jnp.zeros_like(x)
        return rows2([jnp.where(m, x, z), jnp.where(m, z, x)])

    def join(parts):
        return parts[0] if len(parts) == 1 else jnp.concatenate(parts, axis=0)

    def seg_sum(x, split):
        tall = rows2([x[:, s] for s in pair_slices])
        tall = _split_dot(tall, ones_bd) if split else _dot(tall.astype(BF16), ones_bd)
        return jnp.concatenate([tall[p * RT:(p + 1) * RT] for p in pairs], axis=1)

    za = join([za_ref[t, 0] for t in tiles])
    first = (lax.broadcasted_iota(jnp.int32, za.shape, 0) & (C - 1)) == 0
    carried = join([jnp.broadcast_to(prev_scr[s], (C, N_A_COLS)) for s in range(NS)])
    prev = jnp.where(first, carried, pltpu.roll(za, 1, axis=0))
    for s in range(NS):
        last = (s % G + 1) * C - 1
        prev_scr[s] = za_ref[s // G, 0, last:last + 1, :]
    zs_scr[...] = za + (prev - za) * mix_ref[...]

    zwx = zs_scr[:, 4 * W_A:4 * W_A + LANES]
    lora_in = jnp.where(lax.broadcasted_iota(jnp.int32, (RT, LANES), 1) < LORA_W, jnp.tanh(zwx), zwx)
    lora = _dot(lora_in.astype(BF16), wwa_ref[...])

    r = zs_scr[:, 0:W_A]
    k = zs_scr[:, W_A:2 * W_A]
    v = zs_scr[:, 2 * W_A:3 * W_A]
    g_gate = zs_scr[:, 3 * W_A:4 * W_A]
    logw = -EXP_NEG_HALF * _sigmoid(w0_ref[...] + lora[:, 0:W_A])
    a = _sigmoid(a0_ref[...] + lora[:, W_A:2 * W_A])
    l_hi = logw.astype(BF16)
    rem = logw - l_hi.astype(F32)
    l_mid = rem.astype(BF16)
    l_lo = (rem - l_mid.astype(F32)).astype(BF16)
    cum = _dot(tri_incl, l_hi) + _dot(tri_incl, l_mid) + _dot(tri_incl, l_lo)
    cum_last = [cum[(s + 1) * C - 1:(s + 1) * C] for s in range(NS)]
    cum_end = cum_last[0] if NS == 1 else jnp.concatenate(
        [jnp.broadcast_to(x, (C, W_A)) for x in cum_last], axis=0)
    kkr = k * kk_ref[...]
    kk = kkr * jnp.minimum(lax.rsqrt(seg_sum(kkr * kkr, False)), 1e12)
    kp = k * (1.0 + (a - 1.0) * ka_ref[...])
    b = kk * a
    inv_p = jnp.exp(-cum)
    p_end = jnp.exp(cum_end - cum)
    decay_end = [jnp.exp(x) for x in cum_last]
    at = -kk * jnp.exp(cum - logw)
    rt = r * jnp.exp(cum)
    bt = b * inv_p
    kt = kp * inv_p
    bh = b * p_end
    kh = kp * p_end

    def recurrence(ids):
        part = lambda x, key: x[tile_rows[key[0]], pair_slices[key[1]]]
        at_p = {i: part(at, i) for i in ids}
        rt_p = {i: part(rt, i) for i in ids}
        v_p = {i: part(v, i) for i in ids}
        bh_p = {i: part(bh, i) for i in ids}
        kh_p = {i: part(kh, i) for i in ids}
        ar = {i: rows2([at_p[i], rt_p[i]]).astype(BF16) for i in ids}
        b_s = {i: stack(part(bt, i)) for i in ids}
        k_s = {i: stack(part(kt, i)) for i in ids}
        v_s = {i: stack(v_p[i]) for i in ids}
        if R2 % LANES == 0:
            sc = {i: _dot_nt(ar[i], rows2([b_s[i], k_s[i]])) for i in ids}
            ab, ak = {i: sc[i][:R, :R2] for i in ids}, {i: sc[i][:R, R2:] for i in ids}
            rb, rk = {i: sc[i][R:, :R2] for i in ids}, {i: sc[i][R:, R2:] for i in ids}
        else:
            ab_rb = {i: _dot_nt(ar[i], b_s[i]) for i in ids}
            ak_rk = {i: _dot_nt(ar[i], k_s[i]) for i in ids}
            ab, rb = {i: ab_rb[i][:R] for i in ids}, {i: ab_rb[i][R:] for i in ids}
            ak, rk = {i: ak_rk[i][:R] for i in ids}, {i: ak_rk[i][R:] for i in ids}
        n_pow = {i: jnp.where(strict, ab[i], zero) for i in ids}
        l_ak = {i: jnp.where(strict, ak[i], zero).astype(BF16) for i in ids}
        m_rb = {i: jnp.where(incl, rb[i], zero).astype(BF16) for i in ids}
        m_rk = {i: jnp.where(incl, rk[i], zero).astype(BF16) for i in ids}
        t_inv = {i: eye + n_pow[i] for i in ids}
        n_pow = {i: _dot(n_pow[i].astype(BF16), stack(n_pow[i], col_head0)) for i in ids}
        for level in range(log2c - 1):
            n_st = {i: stack(n_pow[i], col_head0) for i in ids}
            if level < log2c - 2:
                both = {i: _dot(rows2([t_inv[i], n_pow[i]]).astype(BF16), n_st[i]) for i in ids}
                t_inv = {i: t_inv[i] + both[i][:R] for i in ids}
                n_pow = {i: both[i][R:] for i in ids}
            else:
                t_inv = {i: t_inv[i] + _dot(t_inv[i].astype(BF16), n_st[i]) for i in ids}
        s_old = {(g, i): state_scr[i[0] * G + g, i[1]] for g in range(G) for i in ids}
        s_old_b = {key: val.astype(BF16) for key, val in s_old.items()}
        yo = {(g, i): _dot_nt(rows2([at_p[i][seq_rows[g]], rt_p[i][seq_rows[g]]]).astype(BF16),
                              s_old_b[g, i]) for g in range(G) for i in ids}
        y_state = {i: join([yo[g, i][:C] for g in range(G)]) for i in ids}
        o_state = {i: join([yo[g, i][C:] for g in range(G)]) for i in ids}
        lv = {i: _dot(rows2([l_ak[i], m_rk[i]]), v_s[i]) for i in ids}
        y = {i: y_state[i] + lv[i][:R] for i in ids}
        u = {i: _dot(t_inv[i].astype(BF16), stack(y[i])) for i in ids}
        u_s = {i: stack(u[i]) for i in ids}
        o_p = {i: o_state[i] + _dot(m_rb[i], u_s[i]) + lv[i][R:] for i in ids}
        for g in range(G):
            rows = seq_rows[g]
            for i in ids:
                seq = i[0] * G + g
                if G == 1:
                    uv = rows2([u_s[i], v_s[i]])
                else:
                    uv = rows2([stack(u[i][rows]), stack(v_p[i][rows])])
                bk = rows2([stack(bh_p[i][rows]), stack(kh_p[i][rows])])
                state_scr[seq, i[1]] = (s_old[g, i] * decay_end[seq][:, pair_slices[i[1]]]
                                        + _dot_tn(uv, bk))
        return o_p

    o_p = recurrence([(t, p) for t in tiles for p in pairs])
    out = join([jnp.concatenate([o_p[t, p] for p in pairs], axis=1) for t in tiles])

    inv_n = 1.0 / HEAD_A
    d = out - seg_sum(out, True) * inv_n
    var = seg_sum(d * d, False) * inv_n
    o_n = d * lax.rsqrt(var + GN_EPS_A) * gng_ref[...] + gnb_ref[...]
    bonus = seg_sum(r * kp * rk_ref[...], True) * v
    ya = (o_n + bonus) * (g_gate * _sigmoid(g_gate))
    for t in tiles:
        ya_ref[t, 0] = ya[tile_rows[t]].astype(ya_ref.dtype)

    @pl.when(c == n_chunks - 1)
    def _():
        shift_out_ref[...] = prev_scr[...]
        for s in range(NS):
            for p in range(N_PAIRS):
                e0 = state_scr[s, p, pl.ds(0, HALF_A, stride=2), :]
                o0 = state_scr[s, p, pl.ds(1, HALF_A, stride=2), :]
                e1 = state_scr[s, p, pl.ds(HEAD_A, HALF_A, stride=2), :]
                o1 = state_scr[s, p, pl.ds(HEAD_A + 1, HALF_A, stride=2), :]
                s_out_ref[s, 2 * p] = jnp.where(half0, e0, pltpu.roll(o0, HEAD_A, axis=1))
                s_out_ref[s, 2 * p + 1] = jnp.where(half0, pltpu.roll(e1, HEAD_A, axis=1), o1)


def _rwkv(z, shift, s0, wts, batch, seq, chunk, nseq, ntile, tile0=0):
    n_chunks = seq // chunk
    assert seq % chunk == 0 and batch % (nseq * ntile) == 0 and (nseq == 1 or n_chunks == 1)
    rows = nseq * chunk
    nsq = nseq * ntile
    const2 = lambda b, c: (0, 0)
    kern = functools.partial(_rwkv_kernel, chunk=chunk, n_chunks=n_chunks, nseq=nseq, ntile=ntile)
    vec = pl.BlockSpec((1, W_A), const2)
    dense = (H_A, HEAD_A // 2, 2 * HEAD_A)
    state_spec = pl.BlockSpec((nsq,) + dense, lambda b, c: (b, 0, 0, 0))
    shift_spec = pl.BlockSpec((nsq, 1, N_A_COLS), lambda b, c: (b, 0, 0))
    ya, s_new, shift_new = pl.pallas_call(
        kern,
        grid=(batch // nsq, n_chunks),
        in_specs=[
            pl.BlockSpec((ntile, 1, rows, N_A_COLS), lambda b, c: (b + tile0, c, 0, 0)),
            shift_spec,
            state_spec,
            pl.BlockSpec((1, N_A_COLS), const2),
            pl.BlockSpec((LANES, 2 * W_A), const2),
            vec, vec, vec, vec, vec, vec, vec,
            pl.BlockSpec((LANES, LANES), const2),
        ],
        out_specs=[
            pl.BlockSpec((ntile, 1, rows, W_A), lambda b, c: (b, c, 0, 0)),
            state_spec,
            shift_spec,
        ],
        out_shape=[
            jax.ShapeDtypeStruct((batch // nseq, n_chunks, rows, W_A), BF16),
            jax.ShapeDtypeStruct((batch,) + dense, F32),
            jax.ShapeDtypeStruct((batch, 1, N_A_COLS), F32),
        ],
        scratch_shapes=[
            pltpu.VMEM((ntile * rows, N_A_COLS), F32),
            pltpu.VMEM((nsq, 1, N_A_COLS), F32),
            pltpu.VMEM((nsq, N_PAIRS, LANES, LANES), F32),
        ],
        compiler_params=pltpu.CompilerParams(
            dimension_semantics=("arbitrary", "arbitrary"), vmem_limit_bytes=VMEM_LIMIT_BYTES),
        name="rwkv7_mix",
    )(z.reshape(-1, n_chunks, rows, N_MAIN), shift, s0.reshape((batch,) + dense), wts["shift_mix"],
      wts["wwa"], wts["w0"],
      wts["a0"], wts["k_k"], wts["k_a"], wts["r_k"], wts["gn_a_g"], wts["gn_a_b"], wts["ones_bd"])
    return (ya.reshape(batch * seq, W_A), s_new.reshape(batch, H_A, HEAD_A, HEAD_A),
            shift_new.reshape(batch, N_A_COLS))


def _ret_kernel(lg_ref, lgv_ref, zb_ref, zt_ref, cos_ref, sin_ref, s0_ref, gng_ref, gnb_ref,
                yb_ref, s_out_ref, *scratch, chunk, n_chunks, nseq, ntile):
    C = chunk
    G = nseq
    NT = ntile
    R = G * C
    log2c = int(math.log2(C))
    c = pl.program_id(1)
    if n_chunks == 1:
        s_in, s_new = s0_ref, s_out_ref
    else:
        (state_scr,) = scratch
        s_in = s_new = state_scr

        @pl.when(c == 0)
        def _():
            state_scr[...] = s0_ref[...]

    heads = range(H_B)
    tiles = range(NT)
    head_slices = [slice(h * DK_B, (h + 1) * DK_B) for h in heads]
    seq_rows = [slice(g * C, (g + 1) * C) for g in range(G)]
    join = lambda parts: parts[0] if len(parts) == 1 else jnp.concatenate(parts, axis=0)
    cos = jnp.concatenate([join([cos_ref[...]] * G)] * H_B, axis=1)
    sin = jnp.concatenate([join([sin_ref[...]] * G)] * H_B, axis=1)
    lgv = lgv_ref[...]
    tok = (lax.broadcasted_iota(jnp.int32, (R, W_B), 0) & (C - 1)).astype(F32)
    q_decay = jnp.exp((tok + 1.0) * lgv)
    k_decay = jnp.exp((C - 1.0 - tok) * lgv) * (DK_B ** -0.5)
    ri = lax.broadcasted_iota(jnp.int32, (R, R), 0)
    ci = lax.broadcasted_iota(jnp.int32, (R, R), 1)
    causal = ((ri >> log2c) == (ci >> log2c)) & (ci <= ri)
    diff_f = jnp.maximum(ri - ci, 0).astype(F32)
    dmask = [jnp.where(causal, jnp.exp(diff_f * lg_ref[h]), 0.0) for h in heads]
    s_decay = jnp.exp(C * lgv)

    def rot_half(x):
        return jnp.concatenate([pltpu.roll(x[:, s], DK_B // 2, axis=1) for s in head_slices], axis=1)

    qr_b, kr_b, q_dec, k_dec, v_b, gate = {}, {}, {}, {}, {}, {}
    for t in tiles:
        q = zb_ref[t, 0, :, 0:W_B]
        k = zb_ref[t, 0, :, W_B:2 * W_B]
        g = jnp.concatenate([zb_ref[t, 0, :, 3 * W_B:4 * W_B - N_TAIL], zt_ref[t, 0]], axis=1)
        qr = q * cos + rot_half(q) * sin
        kr = k * cos + rot_half(k) * sin
        qr_b[t] = qr.astype(BF16)
        kr_b[t] = (kr * (DK_B ** -0.5)).astype(BF16)
        q_dec[t] = (qr * q_decay).astype(BF16)
        k_dec[t] = (kr * k_decay).astype(BF16)
        v_b[t] = zb_ref[t, 0, :, 2 * W_B:3 * W_B].astype(BF16)
        gate[t] = g * _sigmoid(g)

    ids = [(t, h) for t in tiles for h in heads]
    hs = {i: head_slices[i[1]] for i in ids}
    scores = {i: (_dot_nt(qr_b[i[0]][:, hs[i]], kr_b[i[0]][:, hs[i]]) * dmask[i[1]]).astype(BF16)
              for i in ids}
    s_old = {(g, i): s_in[i[0] * G + g, i[1]] for g in range(G) for i in ids}
    cross = {i: join([_dot(q_dec[i[0]][seq_rows[g], hs[i]], s_old[g, i].astype(BF16)) for g in range(G)])
             for i in ids}
    out = {i: _dot(scores[i], v_b[i[0]][:, hs[i]]) + cross[i] for i in ids}
    for g in range(G):
        for i in ids:
            s_new[i[0] * G + g, i[1]] = (s_old[g, i] * s_decay[:, hs[i]]
                                         + _dot_tn(k_dec[i[0]][seq_rows[g], hs[i]],
                                                   v_b[i[0]][seq_rows[g], hs[i]]))
    d = {i: out[i] - jnp.mean(out[i], axis=-1, keepdims=True) for i in ids}
    rstd = {i: lax.rsqrt(jnp.mean(d[i] * d[i], axis=-1, keepdims=True) + GN_EPS_B) for i in ids}
    for t in tiles:
        o_n = jnp.concatenate([d[t, h] * rstd[t, h] for h in heads], axis=1) * gng_ref[...] + gnb_ref[...]
        yb_ref[t, 0] = (o_n * gate[t]).astype(yb_ref.dtype)

    if n_chunks > 1:
        @pl.when(c == n_chunks - 1)
        def _():
            s_out_ref[...] = state_scr[...]


def _retention(z, z_tail, cos, sin, s0, log_g, gn_g, gn_b, batch, seq, chunk, nseq, ntile, tile0=0):
    n_chunks = seq // chunk
    assert seq % chunk == 0 and batch % (nseq * ntile) == 0 and (nseq == 1 or n_chunks == 1)
    rows = nseq * chunk
    nsq = nseq * ntile
    kern = functools.partial(_ret_kernel, chunk=chunk, n_chunks=n_chunks, nseq=nseq, ntile=ntile)
    state_spec = pl.BlockSpec((nsq, H_B, DK_B, DK_B), lambda b, c: (b, 0, 0, 0))
    yb, s_new = pl.pallas_call(
        kern,
        grid=(batch // nsq, n_chunks),
        in_specs=[
            pl.BlockSpec(memory_space=pltpu.SMEM),
            pl.BlockSpec((1, W_B), lambda b, c: (0, 0)),
            pl.BlockSpec((ntile, 1, rows, N_A_COLS), lambda b, c: (b + tile0, c, 0, 1)),
            pl.BlockSpec((ntile, 1, rows, N_TAIL), lambda b, c: (b + tile0, c, 0, 0)),
            pl.BlockSpec((chunk, DK_B), lambda b, c: (c, 0)),
            pl.BlockSpec((chunk, DK_B), lambda b, c: (c, 0)),
            state_spec,
            pl.BlockSpec((1, W_B), lambda b, c: (0, 0)),
            pl.BlockSpec((1, W_B), lambda b, c: (0, 0)),
        ],
        out_specs=[
            pl.BlockSpec((ntile, 1, rows, W_B), lambda b, c: (b, c, 0, 0)),
            state_spec,
        ],
        out_shape=[
            jax.ShapeDtypeStruct((batch // nseq, n_chunks, rows, W_B), BF16),
            jax.ShapeDtypeStruct((batch, H_B, DK_B, DK_B), F32),
        ],
        scratch_shapes=[pltpu.VMEM((nsq, H_B, DK_B, DK_B), F32)] if n_chunks > 1 else [],
        compiler_params=pltpu.CompilerParams(
            dimension_semantics=("arbitrary", "arbitrary"), vmem_limit_bytes=VMEM_LIMIT_BYTES),
        name="retention",
    )(log_g, jnp.repeat(log_g, DK_B).reshape(1, W_B), z.reshape(-1, n_chunks, rows, N_MAIN),
      z_tail.reshape(-1, n_chunks, rows, N_TAIL), cos, sin, s0, gn_g, gn_b)
    return yb.reshape(batch * seq, W_B), s_new


def _out_kernel(ya_ref, yb_ref, w_ref, x_ref, g_ref, b_ref, y_ref):
    tm = x_ref.shape[0]
    sub = min(tm, OUT_SUB_ROWS)
    blocks = [slice(i, i + sub) for i in range(0, tm, sub)]
    def finish(rows, h):
        t = ALPHA * x_ref[rows, :] + h
        mu = jnp.mean(t, axis=-1, keepdims=True)
        d = t - mu
        var = jnp.mean(d * d, axis=-1, keepdims=True)
        y_ref[rows, :] = d * lax.rsqrt(var + LN_EPS) * g_ref[...] + b_ref[...]

    pending = None
    for rows in blocks:
        h = _dot(ya_ref[rows, :], w_ref[0:W_A, :]) + _dot(yb_ref[rows, :], w_ref[W_A:W_A + W_B, :])
        if pending is not None:
            finish(*pending)
        pending = (rows, h)
    finish(*pending)


def _out_proj(ya, yb, w_out, x, ln_g, ln_b, tm):
    m = x.shape[0]
    assert m % tm == 0
    rows = lambda i: (i, 0)
    const = lambda i: (0, 0)
    return pl.pallas_call(
        _out_kernel,
        grid=(m // tm,),
        in_specs=[
            pl.BlockSpec((tm, W_A), rows),
            pl.BlockSpec((tm, W_B), rows),
            pl.BlockSpec((W_A + W_B, D_MODEL), const),
            pl.BlockSpec((tm, D_MODEL), rows),
            pl.BlockSpec((1, D_MODEL), const),
            pl.BlockSpec((1, D_MODEL), const),
        ],
        out_specs=pl.BlockSpec((tm, D_MODEL), rows),
        out_shape=jax.ShapeDtypeStruct((m, D_MODEL), F32),
        compiler_params=pltpu.CompilerParams(
            dimension_semantics=("arbitrary",), vmem_limit_bytes=VMEM_LIMIT_BYTES),
        name="out_proj_ln",
    )(ya, yb, w_out, x, ln_g, ln_b)


def _rope_tables(pos):
    half = DK_B // 2
    inv = ROPE_BASE ** (-jnp.arange(half, dtype=F32) / half)
    ang = pos.astype(F32)[:, None] * inv[None, :]
    cos = jnp.cos(ang)
    sin = jnp.sin(ang)
    return jnp.concatenate([cos, cos], axis=-1), jnp.concatenate([-sin, sin], axis=-1)


def _tile(m, candidates):
    for t in candidates:
        if m % t == 0:
            return t
    return m


def _trunk(x2d, batch, seq, pos0, s_rwkv, s_shift, s_ret, wts, c_rwkv, c_ret, bblk, need_y, ntile=1,
           proj=None, tile0=0):
    m = batch * seq
    z, z_tail = proj if proj is not None else _in_proj(
        x2d, wts["w_in_main"], wts["w_in_tail"], _tile(m, (1024, 512, 256)), IN_PROJ_TN)
    ya, sa, shift = _rwkv(z, s_shift.reshape(batch, 1, N_A_COLS), s_rwkv, wts, batch, seq, c_rwkv,
                          bblk, ntile, tile0)
    cos, sin = _rope_tables(pos0 + jnp.arange(seq, dtype=jnp.int32))
    yb, sb = _retention(z, z_tail, cos, sin, s_ret, wts["log_g"], wts["gn_b_g"], wts["gn_b_b"],
                        batch, seq, c_ret, bblk, ntile, tile0)
    y = None
    if need_y:
        y = _out_proj(ya, yb, wts["w_out"], x2d, wts["ln_g"], wts["ln_b"],
                      _tile(m, (512, 256)))
    return y, sa, shift, sb


def _weights(w_out, shift_mix, w0, w_up, a0, a_up, k_k, k_a, r_k, gn_a_g, gn_a_b, gn_b_g, gn_b_b,
             ln_g, ln_b):
    zw = jnp.zeros((LORA_W, W_A), F32)
    wwa = jnp.concatenate([jnp.concatenate([w_up[0], zw], axis=1),
                           jnp.concatenate([zw, a_up[0]], axis=1)], axis=0).astype(BF16)
    lane = jnp.arange(LANES)
    return {
        "w_out": w_out[0].astype(BF16),
        "shift_mix": shift_mix,
        "wwa": wwa,
        "w0": w0, "a0": a0, "k_k": k_k, "k_a": k_a, "r_k": r_k.reshape(1, W_A),
        "gn_a_g": gn_a_g, "gn_a_b": gn_a_b, "gn_b_g": gn_b_g, "gn_b_b": gn_b_b,
        "ln_g": ln_g, "ln_b": ln_b,
        "ones_bd": (lane[:, None] // HEAD_A == lane[None, :] // HEAD_A).astype(BF16),
        "log_g": jnp.log1p(-jnp.exp2(-5.0 - jnp.arange(H_B, dtype=F32))),
    }


def kernel(x_prompt, x_sample, state_rwkv, state_shift, state_ret, meta_tokens, w_in, w_out, shift_mix,
           w0, w_up, a0, a_up, k_k, k_a, r_k, gn_a_g, gn_a_b, gn_b_g, gn_b_b, ln_g, ln_b):
    bp, tp, _ = x_prompt.shape
    bs, ts, _ = x_sample.shape
    wts = _weights(w_out, shift_mix, w0, w_up, a0, a_up, k_k, k_a, r_k, gn_a_g, gn_a_b, gn_b_g,
                   gn_b_b, ln_g, ln_b)
    x_s = x_sample.reshape(bs * ts, D_MODEL)
    pad = jnp.zeros((SAMPLE_SEQS_PER_TILE * ts - N_META, D_MODEL), F32)
    x_sm = jnp.concatenate([x_s, meta_tokens.astype(F32), pad], axis=0)
    z_sm, zt_sm, wts["w_in_main"], wts["w_in_tail"] = _in_proj(
        x_sm, w_in[0], w_in[0], x_sm.shape[0], IN_PROJ_CAST_TN)
    proj_sm = (z_sm, zt_sm)
    _, ra_m, sh_m, rb_m = _trunk(
        None, 1, N_META, 0,
        jnp.zeros((1, H_A, HEAD_A, HEAD_A), F32), jnp.zeros((1, N_A_COLS), F32),
        jnp.zeros((1, H_B, DK_B, DK_B), F32), wts, N_META, N_META, 1, False,
        proj=proj_sm, tile0=(bs * ts) // N_META)
    rep = lambda s: jnp.broadcast_to(s, (bp,) + s.shape[1:])
    y_p, ra_p, sh_p, rb_p = _trunk(
        x_prompt.reshape(bp * tp, D_MODEL), bp, tp, N_META, rep(ra_m), rep(sh_m), rep(rb_m),
        wts, 64, 128, 1, True, ntile=4)
    y_s, ra_s, sh_s, rb_s = _trunk(
        x_s, bs, ts, PAST_LEN, state_rwkv[0], state_shift[0], state_ret[0],
        wts, ts, ts, SAMPLE_SEQS_PER_TILE, True, ntile=2, proj=proj_sm)
    return (y_p.reshape(bp, tp, D_MODEL), y_s.reshape(bs, ts, D_MODEL),
            ra_p[None], sh_p[None], rb_p[None], ra_s[None], sh_s[None], rb_s[None])
```

```python
import functools
import math

import jax
import jax.numpy as jnp
from jax import lax
from jax.experimental import pallas as pl
from jax.experimental.pallas import tpu as pltpu

D_MODEL = 2048
N_META = 16
W_A = 1024
HEAD_A = 64
H_A = W_A // HEAD_A
LORA_W = 64
LORA_A = 64
W_B = D_MODEL - W_A
H_B = 8
DK_B = W_B // H_B
N_A_COLS = 4 * W_A + LORA_W + LORA_A
N_B_COLS = 4 * W_B
N_IN = N_A_COLS + N_B_COLS
N_TAIL = 128
N_MAIN = N_IN - N_TAIL
IN_PROJ_TN = 2048
IN_PROJ_CAST_TN = 1024
OUT_SUB_ROWS = 128
SAMPLE_SEQS_PER_TILE = 8
DEPTH = 1
PAST_LEN = 16384
ALPHA = (2.0 * DEPTH) ** 0.25
ROPE_BASE = 10000.0
LN_EPS = 1e-5
GN_EPS_A = 64e-5
GN_EPS_B = 1e-5
EXP_NEG_HALF = math.exp(-0.5)

LANES = 128
N_PAIRS = W_A // LANES
VMEM_LIMIT_BYTES = 56 * 1024 * 1024

F32 = jnp.float32
BF16 = jnp.bfloat16


def _dot(a, b):
    return jnp.dot(a, b, preferred_element_type=F32)


def _dot_nt(a, b):
    return lax.dot_general(a, b, (((1,), (1,)), ((), ())), preferred_element_type=F32)


def _dot_tn(a, b):
    return lax.dot_general(a, b, (((0,), (0,)), ((), ())), preferred_element_type=F32)


def _split_dot(x, w_bf16):
    hi = x.astype(BF16)
    lo = (x - hi.astype(F32)).astype(BF16)
    return _dot(hi, w_bf16) + _dot(lo, w_bf16)


def _sigmoid(x):
    return 1.0 / (1.0 + jnp.exp(-x))


def _in_proj_kernel(x_ref, w_ref, w_tail_ref, o_ref, o_tail_ref):
    xb = x_ref[...].astype(BF16)
    o_ref[...] = _dot(xb, w_ref[...])

    @pl.when(pl.program_id(1) == pl.num_programs(1) - 1)
    def _():
        o_tail_ref[...] = _dot(xb, w_tail_ref[...])


def _in_proj_cast_kernel(x_ref, x2_ref, w_ref, w_tail_ref,
                         o_ref, o_tail_ref, o2_ref, o2_tail_ref, wb_ref, wb_tail_ref):
    xb = x_ref[...].astype(BF16)
    x2b = x2_ref[...].astype(BF16)
    wb = w_ref[...].astype(BF16)
    wb_ref[...] = wb
    o_ref[...] = _dot(xb, wb)
    o2_ref[...] = _dot(x2b, wb)

    @pl.when(pl.program_id(1) == pl.num_programs(1) - 1)
    def _():
        wb_tail = w_tail_ref[...].astype(BF16)
        wb_tail_ref[...] = wb_tail
        o_tail_ref[...] = _dot(xb, wb_tail)
        o2_tail_ref[...] = _dot(x2b, wb_tail)


def _z_specs(tm, tn):
    return [pl.BlockSpec((tm, tn), lambda i, j: (i, j)), pl.BlockSpec((tm, N_TAIL), lambda i, j: (i, 0))]


def _z_shapes(m):
    return [jax.ShapeDtypeStruct((m, N_MAIN), F32), jax.ShapeDtypeStruct((m, N_TAIL), F32)]


_IN_PROJ_PARAMS = pltpu.CompilerParams(
    dimension_semantics=("arbitrary", "arbitrary"), vmem_limit_bytes=VMEM_LIMIT_BYTES)


def _in_proj(x, w_main, w_tail, tm, tn):
    m, k = x.shape
    assert m % tm == 0 and N_MAIN % tn == 0
    return pl.pallas_call(
        _in_proj_kernel,
        grid=(m // tm, N_MAIN // tn),
        in_specs=[pl.BlockSpec((tm, k), lambda i, j: (i, 0)),
                  pl.BlockSpec((k, tn), lambda i, j: (0, j)),
                  pl.BlockSpec((k, N_TAIL), lambda i, j: (0, 0))],
        out_specs=_z_specs(tm, tn),
        out_shape=_z_shapes(m),
        compiler_params=_IN_PROJ_PARAMS,
        name="in_proj",
    )(x, w_main, w_tail)


def _in_proj_cast(x, x2, w, tn):
    (m, k), m2 = x.shape, x2.shape[0]
    assert N_MAIN % tn == 0 and w.shape == (k, N_IN)
    z2_specs = [pl.BlockSpec((m2, tn), lambda i, j: (0, j)), pl.BlockSpec((m2, N_TAIL), lambda i, j: (0, 0))]
    return pl.pallas_call(
        _in_proj_cast_kernel,
        grid=(1, N_MAIN // tn),
        in_specs=[pl.BlockSpec((m, k), lambda i, j: (0, 0)),
                  pl.BlockSpec((m2, k), lambda i, j: (0, 0)),
                  pl.BlockSpec((k, tn), lambda i, j: (0, j)),
                  pl.BlockSpec((k, N_TAIL), lambda i, j: (0, N_MAIN // N_TAIL))],
        out_specs=_z_specs(m, tn) + z2_specs + [pl.BlockSpec((k, tn), lambda i, j: (0, j)),
                                                pl.BlockSpec((k, N_TAIL), lambda i, j: (0, 0))],
        out_shape=_z_shapes(m) + _z_shapes(m2) + [jax.ShapeDtypeStruct((k, N_MAIN), BF16),
                                                  jax.ShapeDtypeStruct((k, N_TAIL), BF16)],
        compiler_params=_IN_PROJ_PARAMS,
        name="in_proj_cast",
    )(x, x2, w, w)


def _rwkv_kernel(za_ref, shift_ref, s0_ref, mix_ref, wwa_ref, w0_ref, a0_ref, kk_ref, ka_ref,
                 rk_ref, gng_ref, gnb_ref, ones_ref,
                 ya_ref, s_out_ref, shift_out_ref,
                 zs_scr, prev_scr, state_scr, *, chunk, n_chunks, nseq, ntile):
    C = chunk
    G = nseq
    NT = ntile
    R = G * C
    R2 = 2 * R
    RT = NT * R
    NS = NT * G
    log2c = int(math.log2(C))
    c = pl.program_id(1)
    HALF_A = HEAD_A // 2
    half0 = lax.broadcasted_iota(jnp.int32, (HALF_A, LANES), 1) < HEAD_A

    @pl.when(c == 0)
    def _():
        prev_scr[...] = shift_ref[...]
        for s in range(NS):
            for p in range(N_PAIRS):
                x0 = s0_ref[s, 2 * p]
                x1 = s0_ref[s, 2 * p + 1]
                even = jnp.where(half0, x0, pltpu.roll(x1, HEAD_A, axis=1))
                odd = jnp.where(half0, pltpu.roll(x0, HEAD_A, axis=1), x1)
                zh = jnp.zeros_like(even)
                state_scr[s, p, pl.ds(0, HALF_A, stride=2), :] = jnp.where(half0, even, zh)
                state_scr[s, p, pl.ds(1, HALF_A, stride=2), :] = jnp.where(half0, odd, zh)
                state_scr[s, p, pl.ds(HEAD_A, HALF_A, stride=2), :] = jnp.where(half0, zh, even)
                state_scr[s, p, pl.ds(HEAD_A + 1, HALF_A, stride=2), :] = jnp.where(half0, zh, odd)

    ones_bd = ones_ref[...]
    head0 = {n: lax.broadcasted_iota(jnp.int32, (n, LANES), 1) < HEAD_A for n in {R, C}}
    ri =lax.broadcasted_iota(jnp.int32, (RT, RT), 0)
    ci = lax.broadcasted_iota(jnp.int32, (RT, RT), 1)
    tri_incl = (((ri >> log2c) == (ci >> log2c)) & (ci <= ri)).astype(BF16)
    i2 = lax.broadcasted_iota(jnp.int32, (R, R2), 0)
    j2 = lax.broadcasted_iota(jnp.int32, (R, R2), 1) & (R - 1)
    same_seq = (i2 >> log2c) == (j2 >> log2c)
    strict = same_seq & (j2 < i2)
    incl = same_seq & (j2 <= i2)
    eye = (i2 == j2).astype(F32)
    zero = jnp.zeros((R, R2), F32)
    col_head0 = lax.broadcasted_iota(jnp.int32, (R, R2), 1) < R

    pairs = range(N_PAIRS)
    tiles = range(NT)
    pair_slices = [slice(p * LANES, (p + 1) * LANES) for p in pairs]
    tile_rows = [slice(t * R, (t + 1) * R) for t in tiles]
    seq_rows = [slice(g * C, (g + 1) * C) for g in range(G)]
    rows2 = lambda xs: jnp.concatenate(xs, axis=0)

    def stack(x, first_head=None):
        m = head0[x.shape[0]] if first_head is None else first_head
        x = x.astype(BF16)
        z = jnp.zeros_like(x)
        return rows2([jnp.where(m, x, z), jnp.where(m, z, x)])

    def join(parts):
        return parts[0] if len(parts) == 1 else jnp.concatenate(parts, axis=0)

    def seg_sum(x, split):
        tall = rows2([x[:, s] for s in pair_slices])
        tall = _split_dot(tall, ones_bd) if split else _dot(tall.astype(BF16), ones_bd)
        return jnp.concatenate([tall[p * RT:(p + 1) * RT] for p in pairs], axis=1)

    za = join([za_ref[t, 0] for t in tiles])
    first = (lax.broadcasted_iota(jnp.int32, za.shape, 0) & (C - 1)) == 0
    carried = join([jnp.broadcast_to(prev_scr[s], (C, N_A_COLS)) for s in range(NS)])
    prev = jnp.where(first, carried, pltpu.roll(za, 1, axis=0))
    for s in range(NS):
        last = (s % G + 1) * C - 1
        prev_scr[s] = za_ref[s // G, 0, last:last + 1, :]
    zs_scr[...] = za + (prev - za) * mix_ref[...]

    zwx = zs_scr[:, 4 * W_A:4 * W_A + LANES]
    lora_in = jnp.where(lax.broadcasted_iota(jnp.int32, (RT, LANES), 1) < LORA_W, jnp.tanh(zwx), zwx)
    lora = _dot(lora_in.astype(BF16), wwa_ref[...])

    r = zs_scr[:, 0:W_A]
    k = zs_scr[:, W_A:2 * W_A]
    v = zs_scr[:, 2 * W_A:3 * W_A]
    g_gate = zs_scr[:, 3 * W_A:4 * W_A]
    logw = -EXP_NEG_HALF * _sigmoid(w0_ref[...] + lora[:, 0:W_A])
    a = _sigmoid(a0_ref[...] + lora[:, W_A:2 * W_A])
    l_hi = logw.astype(BF16)
    rem = logw - l_hi.astype(F32)
    l_mid = rem.astype(BF16)
    l_lo = (rem - l_mid.astype(F32)).astype(BF16)
    cum = _dot(tri_incl, l_hi) + _dot(tri_incl, l_mid) + _dot(tri_incl, l_lo)
    cum_last = [cum[(s + 1) * C - 1:(s + 1) * C] for s in range(NS)]
    cum_end = cum_last[0] if NS == 1 else jnp.concatenate(
        [jnp.broadcast_to(x, (C, W_A)) for x in cum_last], axis=0)
    kkr = k * kk_ref[...]
    kk = kkr * jnp.minimum(lax.rsqrt(seg_sum(kkr * kkr, False)), 1e12)
    kp = k * (1.0 + (a - 1.0) * ka_ref[...])
    b = kk * a
    inv_p = jnp.exp(-cum)
    p_end = jnp.exp(cum_end - cum)
    decay_end = [jnp.exp(x) for x in cum_last]
    at = -kk * jnp.exp(cum - logw)
    rt = r * jnp.exp(cum)
    bt = b * inv_p
    kt = kp * inv_p
    bh = b * p_end
    kh = kp * p_end

    def recurrence(ids):
        part = lambda x, key: x[tile_rows[key[0]], pair_slices[key[1]]]
        at_p = {i: part(at, i) for i in ids}
        rt_p = {i: part(rt, i) for i in ids}
        v_p = {i: part(v, i) for i in ids}
        bh_p = {i: part(bh, i) for i in ids}
        kh_p = {i: part(kh, i) for i in ids}
        ar = {i: rows2([at_p[i], rt_p[i]]).astype(BF16) for i in ids}
        b_s = {i: stack(part(bt, i)) for i in ids}
        k_s = {i: stack(part(kt, i)) for i in ids}
        v_s = {i: stack(v_p[i]) for i in ids}
        if R2 % LANES == 0:
            sc = {i: _dot_nt(ar[i], rows2([b_s[i], k_s[i]])) for i in ids}
            ab, ak = {i: sc[i][:R, :R2] for i in ids}, {i: sc[i][:R, R2:] for i in ids}
            rb, rk = {i: sc[i][R:, :R2] for i in ids}, {i: sc[i][R:, R2:] for i in ids}
        else:
            ab_rb = {i: _dot_nt(ar[i], b_s[i]) for i in ids}
            ak_rk = {i: _dot_nt(ar[i], k_s[i]) for i in ids}
            ab, rb = {i: ab_rb[i][:R] for i in ids}, {i: ab_rb[i][R:] for i in ids}
            ak, rk = {i: ak_rk[i][:R] for i in ids}, {i: ak_rk[i][R:] for i in ids}
        n_pow = {i: jnp.where(strict, ab[i], zero) for i in ids}
        l_ak = {i: jnp.where(strict, ak[i], zero).astype(BF16) for i in ids}
        m_rb = {i: jnp.where(incl, rb[i], zero).astype(BF16) for i in ids}
        m_rk = {i: jnp.where(incl, rk[i], zero).astype(BF16) for i in ids}
        t_inv = {i: eye + n_pow[i] for i in ids}
        n_pow = {i: _dot(n_pow[i].astype(BF16), stack(n_pow[i], col_head0)) for i in ids}
        for level in range(log2c - 1):
            n_st = {i: stack(n_pow[i], col_head0) for i in ids}
            if level < log2c - 2:
                both = {i: _dot(rows2([t_inv[i], n_pow[i]]).astype(BF16), n_st[i]) for i in ids}
                t_inv = {i: t_inv[i] + both[i][:R] for i in ids}
                n_pow = {i: both[i][R:] for i in ids}
            else:
                t_inv = {i: t_inv[i] + _dot(t_inv[i].astype(BF16), n_st[i]) for i in ids}
        s_old = {(g, i): state_scr[i[0] * G + g, i[1]] for g in range(G) for i in ids}
        s_old_b = {key: val.astype(BF16) for key, val in s_old.items()}
        yo = {(g, i): _dot_nt(rows2([at_p[i][seq_rows[g]], rt_p[i][seq_rows[g]]]).astype(BF16),
                              s_old_b[g, i]) for g in range(G) for i in ids}
        y_state = {i: join([yo[g, i][:C] for g in range(G)]) for i in ids}
        o_state = {i: join([yo[g, i][C:] for g in range(G)]) for i in ids}
        lv = {i: _dot(rows2([l_ak[i], m_rk[i]]), v_s[i]) for i in ids}
        y = {i: y_state[i] + lv[i][:R] for i in ids}
        u = {i: _dot(t_inv[i].astype(BF16), stack(y[i])) for i in ids}
        u_s = {i: stack(u[i]) for i in ids}
        o_p = {i: o_state[i] + _dot(m_rb[i], u_s[i]) + lv[i][R:] for i in ids}
        for g in range(G):
            rows = seq_rows[g]
            for i in ids:
                seq = i[0] * G + g
                if G == 1:
                    uv = rows2([u_s[i], v_s[i]])
                else:
                    uv = rows2([stack(u[i][rows]), stack(v_p[i][rows])])
                bk = rows2([stack(bh_p[i][rows]), stack(kh_p[i][rows])])
                state_scr[seq, i[1]] = (s_old[g, i] * decay_end[seq][:, pair_slices[i[1]]]
                                        + _dot_tn(uv, bk))
        return o_p

    o_p = recurrence([(t, p) for t in tiles for p in pairs])
    out = join([jnp.concatenate([o_p[t, p] for p in pairs], axis=1) for t in tiles])

    inv_n = 1.0 / HEAD_A
    d = out - seg_sum(out, True) * inv_n
    var = seg_sum(d * d, False) * inv_n
    o_n = d * lax.rsqrt(var + GN_EPS_A) * gng_ref[...] + gnb_ref[...]
    bonus = seg_sum(r * kp * rk_ref[...], True) * v
    ya = (o_n + bonus) * (g_gate * _sigmoid(g_gate))
    for t in tiles:
        ya_ref[t, 0] = ya[tile_rows[t]].astype(ya_ref.dtype)

    @pl.when(c == n_chunks - 1)
    def _():
        shift_out_ref[...] = prev_scr[...]
        for s in range(NS):
            for p in range(N_PAIRS):
                e0 = state_scr[s, p, pl.ds(0, HALF_A, stride=2), :]
                o0 = state_scr[s, p, pl.ds(1, HALF_A, stride=2), :]
                e1 = state_scr[s, p, pl.ds(HEAD_A, HALF_A, stride=2), :]
                o1 = state_scr[s, p, pl.ds(HEAD_A + 1, HALF_A, stride=2), :]
                s_out_ref[s, 2 * p] = jnp.where(half0, e0, pltpu.roll(o0, HEAD_A, axis=1))
                s_out_ref[s, 2 * p + 1] = jnp.where(half0, pltpu.roll(e1, HEAD_A, axis=1), o1)


def _rwkv(z, shift, s0, wts, batch, seq, chunk, nseq, ntile):
    n_chunks = seq // chunk
    assert seq % chunk == 0 and batch % (nseq * ntile) == 0 and (nseq == 1 or n_chunks == 1)
    rows = nseq * chunk
    nsq = nseq * ntile
    const2 = lambda b, c: (0, 0)
    kern = functools.partial(_rwkv_kernel, chunk=chunk, n_chunks=n_chunks, nseq=nseq, ntile=ntile)
    vec = pl.BlockSpec((1, W_A), const2)
    dense = (H_A, HEAD_A // 2, 2 * HEAD_A)
    state_spec = pl.BlockSpec((nsq,) + dense, lambda b, c: (b, 0, 0, 0))
    shift_spec = pl.BlockSpec((nsq, 1, N_A_COLS), lambda b, c: (b, 0, 0))
    ya, s_new, shift_new = pl.pallas_call(
        kern,
        grid=(batch // nsq, n_chunks),
        in_specs=[
            pl.BlockSpec((ntile, 1, rows, N_A_COLS), lambda b, c: (b, c, 0, 0)),
            shift_spec,
            state_spec,
            pl.BlockSpec((1, N_A_COLS), const2),
            pl.BlockSpec((LANES, 2 * W_A), const2),
            vec, vec, vec, vec, vec, vec, vec,
            pl.BlockSpec((LANES, LANES), const2),
        ],
        out_specs=[
            pl.BlockSpec((ntile, 1, rows, W_A), lambda b, c: (b, c, 0, 0)),
            state_spec,
            shift_spec,
        ],
        out_shape=[
            jax.ShapeDtypeStruct((batch // nseq, n_chunks, rows, W_A), BF16),
            jax.ShapeDtypeStruct((batch,) + dense, F32),
            jax.ShapeDtypeStruct((batch, 1, N_A_COLS), F32),
        ],
        scratch_shapes=[
            pltpu.VMEM((ntile * rows, N_A_COLS), F32),
            pltpu.VMEM((nsq, 1, N_A_COLS), F32),
            pltpu.VMEM((nsq, N_PAIRS, LANES, LANES), F32),
        ],
        compiler_params=pltpu.CompilerParams(
            dimension_semantics=("arbitrary", "arbitrary"), vmem_limit_bytes=VMEM_LIMIT_BYTES),
        name="rwkv7_mix",
    )(z.reshape(-1, n_chunks, rows, N_MAIN), shift, s0.reshape((batch,) + dense), wts["shift_mix"],
      wts["wwa"], wts["w0"],
      wts["a0"], wts["k_k"], wts["k_a"], wts["r_k"], wts["gn_a_g"], wts["gn_a_b"], wts["ones_bd"])
    return (ya.reshape(batch * seq, W_A), s_new.reshape(batch, H_A, HEAD_A, HEAD_A),
            shift_new.reshape(batch, N_A_COLS))


def _ret_kernel(lg_ref, lgv_ref, zb_ref, zt_ref, cos_ref, sin_ref, s0_ref, gng_ref, gnb_ref,
                yb_ref, s_out_ref, *scratch, chunk, n_chunks, nseq, ntile):
    C = chunk
    G = nseq
    NT = ntile
    R = G * C
    log2c = int(math.log2(C))
    c = pl.program_id(1)
    if n_chunks == 1:
        s_in, s_new = s0_ref, s_out_ref
    else:
        (state_scr,) = scratch
        s_in = s_new = state_scr

        @pl.when(c == 0)
        def _():
            state_scr[...] = s0_ref[...]

    heads = range(H_B)
    tiles = range(NT)
    head_slices = [slice(h * DK_B, (h + 1) * DK_B) for h in heads]
    seq_rows = [slice(g * C, (g + 1) * C) for g in range(G)]
    join = lambda parts: parts[0] if len(parts) == 1 else jnp.concatenate(parts, axis=0)
    cos = jnp.concatenate([join([cos_ref[...]] * G)] * H_B, axis=1)
    sin = jnp.concatenate([join([sin_ref[...]] * G)] * H_B, axis=1)
    lgv = lgv_ref[...]
    tok = (lax.broadcasted_iota(jnp.int32, (R, W_B), 0) & (C - 1)).astype(F32)
    q_decay = jnp.exp((tok + 1.0) * lgv)
    k_decay = jnp.exp((C - 1.0 - tok) * lgv) * (DK_B ** -0.5)
    ri = lax.broadcasted_iota(jnp.int32, (R, R), 0)
    ci = lax.broadcasted_iota(jnp.int32, (R, R), 1)
    causal = ((ri >> log2c) == (ci >> log2c)) & (ci <= ri)
    diff_f = jnp.maximum(ri - ci, 0).astype(F32)
    dmask = [jnp.where(causal, jnp.exp(diff_f * lg_ref[h]), 0.0) for h in heads]
    s_decay = jnp.exp(C * lgv)

    def rot_half(x):
        return jnp.concatenate([pltpu.roll(x[:, s], DK_B // 2, axis=1) for s in head_slices], axis=1)

    qr_b, kr_b, q_dec, k_dec, v_b, gate = {}, {}, {}, {}, {}, {}
    for t in tiles:
        q = zb_ref[t, 0, :, 0:W_B]
        k = zb_ref[t, 0, :, W_B:2 * W_B]
        g = jnp.concatenate([zb_ref[t, 0, :, 3 * W_B:4 * W_B - N_TAIL], zt_ref[t, 0]], axis=1)
        qr = q * cos + rot_half(q) * sin
        kr = k * cos + rot_half(k) * sin
        qr_b[t] = qr.astype(BF16)
        kr_b[t] = (kr * (DK_B ** -0.5)).astype(BF16)
        q_dec[t] = (qr * q_decay).astype(BF16)
        k_dec[t] = (kr * k_decay).astype(BF16)
        v_b[t] = zb_ref[t, 0, :, 2 * W_B:3 * W_B].astype(BF16)
        gate[t] = g * _sigmoid(g)

    ids = [(t, h) for t in tiles for h in heads]
    hs = {i: head_slices[i[1]] for i in ids}
    scores = {i: (_dot_nt(qr_b[i[0]][:, hs[i]], kr_b[i[0]][:, hs[i]]) * dmask[i[1]]).astype(BF16)
              for i in ids}
    s_old = {(g, i): s_in[i[0] * G + g, i[1]] for g in range(G) for i in ids}
    cross = {i: join([_dot(q_dec[i[0]][seq_rows[g], hs[i]], s_old[g, i].astype(BF16)) for g in range(G)])
             for i in ids}
    out = {i: _dot(scores[i], v_b[i[0]][:, hs[i]]) + cross[i] for i in ids}
    for g in range(G):
        for i in ids:
            s_new[i[0] * G + g, i[1]] = (s_old[g, i] * s_decay[:, hs[i]]
                                         + _dot_tn(k_dec[i[0]][seq_rows[g], hs[i]],
                                                   v_b[i[0]][seq_rows[g], hs[i]]))
    d = {i: out[i] - jnp.mean(out[i], axis=-1, keepdims=True) for i in ids}
    rstd = {i: lax.rsqrt(jnp.mean(d[i] * d[i], axis=-1, keepdims=True) + GN_EPS_B) for i in ids}
    for t in tiles:
        o_n = jnp.concatenate([d[t, h] * rstd[t, h] for h in heads], axis=1) * gng_ref[...] + gnb_ref[...]
        yb_ref[t, 0] = (o_n * gate[t]).astype(yb_ref.dtype)

    if n_chunks > 1:
        @pl.when(c == n_chunks - 1)
        def _():
            s_out_ref[...] = state_scr[...]


def _retention(z, z_tail, cos, sin, s0, log_g, gn_g, gn_b, batch, seq, chunk, nseq, ntile):
    n_chunks = seq // chunk
    assert seq % chunk == 0 and batch % (nseq * ntile) == 0 and (nseq == 1 or n_chunks == 1)
    rows = nseq * chunk
    nsq = nseq * ntile
    kern = functools.partial(_ret_kernel, chunk=chunk, n_chunks=n_chunks, nseq=nseq, ntile=ntile)
    state_spec = pl.BlockSpec((nsq, H_B, DK_B, DK_B), lambda b, c: (b, 0, 0, 0))
    yb, s_new = pl.pallas_call(
        kern,
        grid=(batch // nsq, n_chunks),
        in_specs=[
            pl.BlockSpec(memory_space=pltpu.SMEM),
            pl.BlockSpec((1, W_B), lambda b, c: (0, 0)),
            pl.BlockSpec((ntile, 1, rows, N_A_COLS), lambda b, c: (b, c, 0, 1)),
            pl.BlockSpec((ntile, 1, rows, N_TAIL), lambda b, c: (b, c, 0, 0)),
            pl.BlockSpec((chunk, DK_B), lambda b, c: (c, 0)),
            pl.BlockSpec((chunk, DK_B), lambda b, c: (c, 0)),
            state_spec,
            pl.BlockSpec((1, W_B), lambda b, c: (0, 0)),
            pl.BlockSpec((1, W_B), lambda b, c: (0, 0)),
        ],
        out_specs=[
            pl.BlockSpec((ntile, 1, rows, W_B), lambda b, c: (b, c, 0, 0)),
            state_spec,
        ],
        out_shape=[
            jax.ShapeDtypeStruct((batch // nseq, n_chunks, rows, W_B), BF16),
            jax.ShapeDtypeStruct((batch, H_B, DK_B, DK_B), F32),
        ],
        scratch_shapes=[pltpu.VMEM((nsq, H_B, DK_B, DK_B), F32)] if n_chunks > 1 else [],
        compiler_params=pltpu.CompilerParams(
            dimension_semantics=("arbitrary", "arbitrary"), vmem_limit_bytes=VMEM_LIMIT_BYTES),
        name="retention",
    )(log_g, jnp.repeat(log_g, DK_B).reshape(1, W_B), z.reshape(-1, n_chunks, rows, N_MAIN),
      z_tail.reshape(-1, n_chunks, rows, N_TAIL), cos, sin, s0, gn_g, gn_b)
    return yb.reshape(batch * seq, W_B), s_new


def _out_kernel(ya_ref, yb_ref, w_ref, x_ref, g_ref, b_ref, y_ref):
    tm = x_ref.shape[0]
    sub = min(tm, OUT_SUB_ROWS)
    blocks = [slice(i, i + sub) for i in range(0, tm, sub)]
    def finish(rows, h):
        t = ALPHA * x_ref[rows, :] + h
        mu = jnp.mean(t, axis=-1, keepdims=True)
        d = t - mu
        var = jnp.mean(d * d, axis=-1, keepdims=True)
        y_ref[rows, :] = d * lax.rsqrt(var + LN_EPS) * g_ref[...] + b_ref[...]

    pending = None
    for rows in blocks:
        h = _dot(ya_ref[rows, :], w_ref[0:W_A, :]) + _dot(yb_ref[rows, :], w_ref[W_A:W_A + W_B, :])
        if pending is not None:
            finish(*pending)
        pending = (rows, h)
    finish(*pending)


def _out_proj(ya, yb, w_out, x, ln_g, ln_b, tm):
    m = x.shape[0]
    assert m % tm == 0
    rows = lambda i: (i, 0)
    const = lambda i: (0, 0)
    return pl.pallas_call(
        _out_kernel,
        grid=(m // tm,),
        in_specs=[
            pl.BlockSpec((tm, W_A), rows),
            pl.BlockSpec((tm, W_B), rows),
            pl.BlockSpec((W_A + W_B, D_MODEL), const),
            pl.BlockSpec((tm, D_MODEL), rows),
            pl.BlockSpec((1, D_MODEL), const),
            pl.BlockSpec((1, D_MODEL), const),
        ],
        out_specs=pl.BlockSpec((tm, D_MODEL), rows),
        out_shape=jax.ShapeDtypeStruct((m, D_MODEL), F32),
        compiler_params=pltpu.CompilerParams(
            dimension_semantics=("arbitrary",), vmem_limit_bytes=VMEM_LIMIT_BYTES),
        name="out_proj_ln",
    )(ya, yb, w_out, x, ln_g, ln_b)


def _rope_tables(pos):
    half = DK_B // 2
    inv = ROPE_BASE ** (-jnp.arange(half, dtype=F32) / half)
    ang = pos.astype(F32)[:, None] * inv[None, :]
    cos = jnp.cos(ang)
    sin = jnp.sin(ang)
    return jnp.concatenate([cos, cos], axis=-1), jnp.concatenate([-sin, sin], axis=-1)


def _tile(m, candidates):
    for t in candidates:
        if m % t == 0:
            return t
    return m


def _trunk(x2d, batch, seq, pos0, s_rwkv, s_shift, s_ret, wts, c_rwkv, c_ret, bblk, need_y, ntile=1,
           proj=None):
    m = batch * seq
    z, z_tail = proj if proj is not None else _in_proj(
        x2d, wts["w_in_main"], wts["w_in_tail"], _tile(m, (1024, 512, 256)), IN_PROJ_TN)
    ya, sa, shift = _rwkv(z, s_shift.reshape(batch, 1, N_A_COLS), s_rwkv, wts, batch, seq, c_rwkv,
                          bblk, ntile)
    cos, sin = _rope_tables(pos0 + jnp.arange(seq, dtype=jnp.int32))
    yb, sb = _retention(z, z_tail, cos, sin, s_ret, wts["log_g"], wts["gn_b_g"], wts["gn_b_b"],
                        batch, seq, c_ret, bblk, ntile)
    y = None
    if need_y:
        y = _out_proj(ya, yb, wts["w_out"], x2d, wts["ln_g"], wts["ln_b"],
                      _tile(m, (512, 256)))
    return y, sa, shift, sb


def _weights(w_out, shift_mix, w0, w_up, a0, a_up, k_k, k_a, r_k, gn_a_g, gn_a_b, gn_b_g, gn_b_b,
             ln_g, ln_b):
    zw = jnp.zeros((LORA_W, W_A), F32)
    wwa = jnp.concatenate([jnp.concatenate([w_up[0], zw], axis=1),
                           jnp.concatenate([zw, a_up[0]], axis=1)], axis=0).astype(BF16)
    lane = jnp.arange(LANES)
    return {
        "w_out": w_out[0].astype(BF16),
        "shift_mix": shift_mix,
        "wwa": wwa,
        "w0": w0, "a0": a0, "k_k": k_k, "k_a": k_a, "r_k": r_k.reshape(1, W_A),
        "gn_a_g": gn_a_g, "gn_a_b": gn_a_b, "gn_b_g": gn_b_g, "gn_b_b": gn_b_b,
        "ln_g": ln_g, "ln_b": ln_b,
        "ones_bd": (lane[:, None] // HEAD_A == lane[None, :] // HEAD_A).astype(BF16),
        "log_g": jnp.log1p(-jnp.exp2(-5.0 - jnp.arange(H_B, dtype=F32))),
    }


def kernel(x_prompt, x_sample, state_rwkv, state_shift, state_ret, meta_tokens, w_in, w_out, shift_mix,
           w0, w_up, a0, a_up, k_k, k_a, r_k, gn_a_g, gn_a_b, gn_b_g, gn_b_b, ln_g, ln_b):
    bp, tp, _ = x_prompt.shape
    bs, ts, _ = x_sample.shape
    wts = _weights(w_out, shift_mix, w0, w_up, a0, a_up, k_k, k_a, r_k, gn_a_g, gn_a_b, gn_b_g,
                   gn_b_b, ln_g, ln_b)
    x_s = x_sample.reshape(bs * ts, D_MODEL)
    z_s, zt_s, z_m, zt_m, wts["w_in_main"], wts["w_in_tail"] = _in_proj_cast(
        x_s, meta_tokens.astype(F32), w_in[0], IN_PROJ_CAST_TN)
    _, ra_m, sh_m, rb_m = _trunk(
        None, 1, N_META, 0,
        jnp.zeros((1, H_A, HEAD_A, HEAD_A), F32), jnp.zeros((1, N_A_COLS), F32),
        jnp.zeros((1, H_B, DK_B, DK_B), F32), wts, N_META, N_META, 1, False, proj=(z_m, zt_m))
    rep = lambda s: jnp.broadcast_to(s, (bp,) + s.shape[1:])
    y_p, ra_p, sh_p, rb_p = _trunk(
        x_prompt.reshape(bp * tp, D_MODEL), bp, tp, N_META, rep(ra_m), rep(sh_m), rep(rb_m),
        wts, 64, 128, 1, True, ntile=4)
    y_s, ra_s, sh_s, rb_s = _trunk(
        x_s, bs, ts, PAST_LEN, state_rwkv[0], state_shift[0], state_ret[0],
        wts, ts, ts, SAMPLE_SEQS_PER_TILE, True, ntile=2, proj=(z_s, zt_s))
    return (y_p.reshape(bp, tp, D_MODEL), y_s.reshape(bs, ts, D_MODEL),
            ra_p[None], sh_p[None], rb_p[None], ra_s[None], sh_s[None], rb_s[None])
```

```python
import functools
import math

import jax
import jax.numpy as jnp
from jax import lax
from jax.experimental import pallas as pl
from jax.experimental.pallas import tpu as pltpu

D_MODEL = 2048
N_META = 16
W_A = 1024
HEAD_A = 64
H_A = W_A // HEAD_A
LORA_W = 64
LORA_A = 64
W_B = D_MODEL - W_A
H_B = 8
DK_B = W_B // H_B
N_A_COLS = 4 * W_A + LORA_W + LORA_A
N_B_COLS = 4 * W_B
N_IN = N_A_COLS + N_B_COLS
N_TAIL = 128
N_MAIN = N_IN - N_TAIL
IN_PROJ_TN = 2048
IN_PROJ_CAST_TN = 1024
OUT_SUB_ROWS = 128
SAMPLE_SEQS_PER_TILE = 8
DEPTH = 1
PAST_LEN = 16384
ALPHA = (2.0 * DEPTH) ** 0.25
ROPE_BASE = 10000.0
LN_EPS = 1e-5
GN_EPS_A = 64e-5
GN_EPS_B = 1e-5
EXP_NEG_HALF = math.exp(-0.5)

LANES = 128
N_PAIRS = W_A // LANES
VMEM_LIMIT_BYTES = 56 * 1024 * 1024

F32 = jnp.float32
BF16 = jnp.bfloat16


def _dot(a, b):
    return jnp.dot(a, b, preferred_element_type=F32)


def _dot_nt(a, b):
    return lax.dot_general(a, b, (((1,), (1,)), ((), ())), preferred_element_type=F32)


def _dot_tn(a, b):
    return lax.dot_general(a, b, (((0,), (0,)), ((), ())), preferred_element_type=F32)


def _split_dot(x, w_bf16):
    hi = x.astype(BF16)
    lo = (x - hi.astype(F32)).astype(BF16)
    return _dot(hi, w_bf16) + _dot(lo, w_bf16)


def _split_dot_rhs(w_bf16, x):
    hi = x.astype(BF16)
    lo = (x - hi.astype(F32)).astype(BF16)
    return _dot(w_bf16, hi) + _dot(w_bf16, lo)


def _sigmoid(x):
    return 1.0 / (1.0 + jnp.exp(-x))


def _in_proj_kernel(x_ref, w_ref, w_tail_ref, o_ref, o_tail_ref):
    xb = x_ref[...].astype(BF16)
    o_ref[...] = _dot(xb, w_ref[...])

    @pl.when(pl.program_id(1) == pl.num_programs(1) - 1)
    def _():
        o_tail_ref[...] = _dot(xb, w_tail_ref[...])


def _in_proj_cast_kernel(x_ref, x2_ref, w_ref, w_tail_ref,
                         o_ref, o_tail_ref, o2_ref, o2_tail_ref, wb_ref, wb_tail_ref):
    xb = x_ref[...].astype(BF16)
    x2b = x2_ref[...].astype(BF16)
    wb = w_ref[...].astype(BF16)
    wb_ref[...] = wb
    o_ref[...] = _dot(xb, wb)
    o2_ref[...] = _dot(x2b, wb)

    @pl.when(pl.program_id(1) == pl.num_programs(1) - 1)
    def _():
        wb_tail = w_tail_ref[...].astype(BF16)
        wb_tail_ref[...] = wb_tail
        o_tail_ref[...] = _dot(xb, wb_tail)
        o2_tail_ref[...] = _dot(x2b, wb_tail)


def _z_specs(tm, tn):
    return [pl.BlockSpec((tm, tn), lambda i, j: (i, j)), pl.BlockSpec((tm, N_TAIL), lambda i, j: (i, 0))]


def _z_shapes(m):
    return [jax.ShapeDtypeStruct((m, N_MAIN), F32), jax.ShapeDtypeStruct((m, N_TAIL), F32)]


_IN_PROJ_PARAMS = pltpu.CompilerParams(
    dimension_semantics=("arbitrary", "arbitrary"), vmem_limit_bytes=VMEM_LIMIT_BYTES)


def _in_proj(x, w_main, w_tail, tm, tn):
    m, k = x.shape
    assert m % tm == 0 and N_MAIN % tn == 0
    return pl.pallas_call(
        _in_proj_kernel,
        grid=(m // tm, N_MAIN // tn),
        in_specs=[pl.BlockSpec((tm, k), lambda i, j: (i, 0)),
                  pl.BlockSpec((k, tn), lambda i, j: (0, j)),
                  pl.BlockSpec((k, N_TAIL), lambda i, j: (0, 0))],
        out_specs=_z_specs(tm, tn),
        out_shape=_z_shapes(m),
        compiler_params=_IN_PROJ_PARAMS,
        name="in_proj",
    )(x, w_main, w_tail)


def _in_proj_cast(x, x2, w, tn):
    (m, k), m2 = x.shape, x2.shape[0]
    assert N_MAIN % tn == 0 and w.shape == (k, N_IN)
    z2_specs = [pl.BlockSpec((m2, tn), lambda i, j: (0, j)), pl.BlockSpec((m2, N_TAIL), lambda i, j: (0, 0))]
    return pl.pallas_call(
        _in_proj_cast_kernel,
        grid=(1, N_MAIN // tn),
        in_specs=[pl.BlockSpec((m, k), lambda i, j: (0, 0)),
                  pl.BlockSpec((m2, k), lambda i, j: (0, 0)),
                  pl.BlockSpec((k, tn), lambda i, j: (0, j)),
                  pl.BlockSpec((k, N_TAIL), lambda i, j: (0, N_MAIN // N_TAIL))],
        out_specs=_z_specs(m, tn) + z2_specs + [pl.BlockSpec((k, tn), lambda i, j: (0, j)),
                                                pl.BlockSpec((k, N_TAIL), lambda i, j: (0, 0))],
        out_shape=_z_shapes(m) + _z_shapes(m2) + [jax.ShapeDtypeStruct((k, N_MAIN), BF16),
                                                  jax.ShapeDtypeStruct((k, N_TAIL), BF16)],
        compiler_params=_IN_PROJ_PARAMS,
        name="in_proj_cast",
    )(x, x2, w, w)


def _rwkv_kernel(za_ref, shift_ref, s0_ref, mix_ref, wwa_ref, w0_ref, a0_ref, kk_ref, ka_ref,
                 rk_ref, gng_ref, gnb_ref, ones_ref,
                 ya_ref, s_out_ref, shift_out_ref,
                 zs_scr, prev_scr, state_scr, *, chunk, n_chunks, nseq, ntile):
    C = chunk
    G = nseq
    NT = ntile
    R = G * C
    R2 = 2 * R
    RT = NT * R
    NS = NT * G
    log2c = int(math.log2(C))
    c = pl.program_id(1)
    HALF_A = HEAD_A // 2
    half0 = lax.broadcasted_iota(jnp.int32, (HALF_A, LANES), 1) < HEAD_A

    @pl.when(c == 0)
    def _():
        prev_scr[...] = shift_ref[...]
        for s in range(NS):
            for p in range(N_PAIRS):
                x0 = s0_ref[s, 2 * p]
                x1 = s0_ref[s, 2 * p + 1]
                even = jnp.where(half0, x0, pltpu.roll(x1, HEAD_A, axis=1))
                odd = jnp.where(half0, pltpu.roll(x0, HEAD_A, axis=1), x1)
                zh = jnp.zeros_like(even)
                state_scr[s, p, pl.ds(0, HALF_A, stride=2), :] = jnp.where(half0, even, zh)
                state_scr[s, p, pl.ds(1, HALF_A, stride=2), :] = jnp.where(half0, odd, zh)
                state_scr[s, p, pl.ds(HEAD_A, HALF_A, stride=2), :] = jnp.where(half0, zh, even)
                state_scr[s, p, pl.ds(HEAD_A + 1, HALF_A, stride=2), :] = jnp.where(half0, zh, odd)

    ones_bd = ones_ref[...]
    head0 = {n: lax.broadcasted_iota(jnp.int32, (n, LANES), 1) < HEAD_A for n in {R, C}}
    ri =lax.broadcasted_iota(jnp.int32, (RT, RT), 0)
    ci = lax.broadcasted_iota(jnp.int32, (RT, RT), 1)
    tri_incl = (((ri >> log2c) == (ci >> log2c)) & (ci <= ri)).astype(BF16)
    i2 = lax.broadcasted_iota(jnp.int32, (R, R2), 0)
    j2 = lax.broadcasted_iota(jnp.int32, (R, R2), 1) & (R - 1)
    same_seq = (i2 >> log2c) == (j2 >> log2c)
    strict = same_seq & (j2 < i2)
    incl = same_seq & (j2 <= i2)
    eye = (i2 == j2).astype(F32)
    zero = jnp.zeros((R, R2), F32)
    col_head0 = lax.broadcasted_iota(jnp.int32, (R, R2), 1) < R

    pairs = range(N_PAIRS)
    tiles = range(NT)
    pair_slices = [slice(p * LANES, (p + 1) * LANES) for p in pairs]
    tile_rows = [slice(t * R, (t + 1) * R) for t in tiles]
    seq_rows = [slice(g * C, (g + 1) * C) for g in range(G)]
    rows2 = lambda xs: jnp.concatenate(xs, axis=0)

    def stack(x, first_head=None):
        m = head0[x.shape[0]] if first_head is None else first_head
        x = x.astype(BF16)
        z = jnp.zeros_like(x)
        return rows2([jnp.where(m, x, z), jnp.where(m, z, x)])

    def join(parts):
        return parts[0] if len(parts) == 1 else jnp.concatenate(parts, axis=0)

    def seg_sum(x, split):
        tall = rows2([x[:, s] for s in pair_slices])
        tall = _split_dot(tall, ones_bd) if split else _dot(tall.astype(BF16), ones_bd)
        return jnp.concatenate([tall[p * RT:(p + 1) * RT] for p in pairs], axis=1)

    za = join([za_ref[t, 0] for t in tiles])
    first = (lax.broadcasted_iota(jnp.int32, za.shape, 0) & (C - 1)) == 0
    carried = join([jnp.broadcast_to(prev_scr[s], (C, N_A_COLS)) for s in range(NS)])
    prev = jnp.where(first, carried, pltpu.roll(za, 1, axis=0))
    for s in range(NS):
        last = (s % G + 1) * C - 1
        prev_scr[s] = za_ref[s // G, 0, last:last + 1, :]
    zs_scr[...] = za + (prev - za) * mix_ref[...]

    zwx = zs_scr[:, 4 * W_A:4 * W_A + LANES]
    lora_in = jnp.where(lax.broadcasted_iota(jnp.int32, (RT, LANES), 1) < LORA_W, jnp.tanh(zwx), zwx)
    lora = _dot(lora_in.astype(BF16), wwa_ref[...])

    r = zs_scr[:, 0:W_A]
    k = zs_scr[:, W_A:2 * W_A]
    v = zs_scr[:, 2 * W_A:3 * W_A]
    g_gate = zs_scr[:, 3 * W_A:4 * W_A]
    logw = -EXP_NEG_HALF * _sigmoid(w0_ref[...] + lora[:, 0:W_A])
    a = _sigmoid(a0_ref[...] + lora[:, W_A:2 * W_A])
    cum = _split_dot_rhs(tri_incl, logw)
    cum_last = [cum[(s + 1) * C - 1:(s + 1) * C] for s in range(NS)]
    cum_end = cum_last[0] if NS == 1 else jnp.concatenate(
        [jnp.broadcast_to(x, (C, W_A)) for x in cum_last], axis=0)
    kkr = k * kk_ref[...]
    kk = kkr * jnp.minimum(lax.rsqrt(seg_sum(kkr * kkr, False)), 1e12)
    kp = k * (1.0 + (a - 1.0) * ka_ref[...])
    b = kk * a
    inv_p = jnp.exp(-cum)
    p_end = jnp.exp(cum_end - cum)
    decay_end = [jnp.exp(x) for x in cum_last]
    at = -kk * jnp.exp(cum - logw)
    rt = r * jnp.exp(cum)
    bt = b * inv_p
    kt = kp * inv_p
    bh = b * p_end
    kh = kp * p_end

    def recurrence(ids):
        part = lambda x, key: x[tile_rows[key[0]], pair_slices[key[1]]]
        at_p = {i: part(at, i) for i in ids}
        rt_p = {i: part(rt, i) for i in ids}
        v_p = {i: part(v, i) for i in ids}
        bh_p = {i: part(bh, i) for i in ids}
        kh_p = {i: part(kh, i) for i in ids}
        ar = {i: rows2([at_p[i], rt_p[i]]).astype(BF16) for i in ids}
        b_s = {i: stack(part(bt, i)) for i in ids}
        k_s = {i: stack(part(kt, i)) for i in ids}
        v_s = {i: stack(v_p[i]) for i in ids}
        if R2 % LANES == 0:
            sc = {i: _dot_nt(ar[i], rows2([b_s[i], k_s[i]])) for i in ids}
            ab, ak = {i: sc[i][:R, :R2] for i in ids}, {i: sc[i][:R, R2:] for i in ids}
            rb, rk = {i: sc[i][R:, :R2] for i in ids}, {i: sc[i][R:, R2:] for i in ids}
        else:
            ab_rb = {i: _dot_nt(ar[i], b_s[i]) for i in ids}
            ak_rk = {i: _dot_nt(ar[i], k_s[i]) for i in ids}
            ab, rb = {i: ab_rb[i][:R] for i in ids}, {i: ab_rb[i][R:] for i in ids}
            ak, rk = {i: ak_rk[i][:R] for i in ids}, {i: ak_rk[i][R:] for i in ids}
        n_pow = {i: jnp.where(strict, ab[i], zero) for i in ids}
        l_ak = {i: jnp.where(strict, ak[i], zero).astype(BF16) for i in ids}
        m_rb = {i: jnp.where(incl, rb[i], zero).astype(BF16) for i in ids}
        m_rk = {i: jnp.where(incl, rk[i], zero).astype(BF16) for i in ids}
        t_inv = {i: eye + n_pow[i] for i in ids}
        n_pow = {i: _dot(n_pow[i].astype(BF16), stack(n_pow[i], col_head0)) for i in ids}
        for level in range(log2c - 1):
            n_st = {i: stack(n_pow[i], col_head0) for i in ids}
            if level < log2c - 2:
                both = {i: _dot(rows2([t_inv[i], n_pow[i]]).astype(BF16), n_st[i]) for i in ids}
                t_inv = {i: t_inv[i] + both[i][:R] for i in ids}
                n_pow = {i: both[i][R:] for i in ids}
            else:
                t_inv = {i: t_inv[i] + _dot(t_inv[i].astype(BF16), n_st[i]) for i in ids}
        s_old = {(g, i): state_scr[i[0] * G + g, i[1]] for g in range(G) for i in ids}
        s_old_b = {key: val.astype(BF16) for key, val in s_old.items()}
        yo = {(g, i): _dot_nt(rows2([at_p[i][seq_rows[g]], rt_p[i][seq_rows[g]]]).astype(BF16),
                              s_old_b[g, i]) for g in range(G) for i in ids}
        y_state = {i: join([yo[g, i][:C] for g in range(G)]) for i in ids}
        o_state = {i: join([yo[g, i][C:] for g in range(G)]) for i in ids}
        lv = {i: _dot(rows2([l_ak[i], m_rk[i]]), v_s[i]) for i in ids}
        y = {i: y_state[i] + lv[i][:R] for i in ids}
        u = {i: _dot(t_inv[i].astype(BF16), stack(y[i])) for i in ids}
        u_s = {i: stack(u[i]) for i in ids}
        o_p = {i: o_state[i] + _dot(m_rb[i], u_s[i]) + lv[i][R:] for i in ids}
        for g in range(G):
            rows = seq_rows[g]
            for i in ids:
                seq = i[0] * G + g
                if G == 1:
                    uv = rows2([u_s[i], v_s[i]])
                else:
                    uv = rows2([stack(u[i][rows]), stack(v_p[i][rows])])
                bk = rows2([stack(bh_p[i][rows]), stack(kh_p[i][rows])])
                state_scr[seq, i[1]] = (s_old[g, i] * decay_end[seq][:, pair_slices[i[1]]]
                                        + _dot_tn(uv, bk))
        return o_p

    o_p = recurrence([(t, p) for t in tiles for p in pairs])
    out = join([jnp.concatenate([o_p[t, p] for p in pairs], axis=1) for t in tiles])

    inv_n = 1.0 / HEAD_A
    d = out - seg_sum(out, True) * inv_n
    var = seg_sum(d * d, False) * inv_n
    o_n = d * lax.rsqrt(var + GN_EPS_A) * gng_ref[...] + gnb_ref[...]
    bonus = seg_sum(r * kp * rk_ref[...], False) * v
    ya = (o_n + bonus) * (g_gate * _sigmoid(g_gate))
    for t in tiles:
        ya_ref[t, 0] = ya[tile_rows[t]].astype(ya_ref.dtype)

    @pl.when(c == n_chunks - 1)
    def _():
        shift_out_ref[...] = prev_scr[...]
        for s in range(NS):
            for p in range(N_PAIRS):
                e0 = state_scr[s, p, pl.ds(0, HALF_A, stride=2), :]
                o0 = state_scr[s, p, pl.ds(1, HALF_A, stride=2), :]
                e1 = state_scr[s, p, pl.ds(HEAD_A, HALF_A, stride=2), :]
                o1 = state_scr[s, p, pl.ds(HEAD_A + 1, HALF_A, stride=2), :]
                s_out_ref[s, 2 * p] = jnp.where(half0, e0, pltpu.roll(o0, HEAD_A, axis=1))
                s_out_ref[s, 2 * p + 1] = jnp.where(half0, pltpu.roll(e1, HEAD_A, axis=1), o1)


def _rwkv(z, shift, s0, wts, batch, seq, chunk, nseq, ntile):
    n_chunks = seq // chunk
    assert seq % chunk == 0 and batch % (nseq * ntile) == 0 and (nseq == 1 or n_chunks == 1)
    rows = nseq * chunk
    nsq = nseq * ntile
    const2 = lambda b, c: (0, 0)
    kern = functools.partial(_rwkv_kernel, chunk=chunk, n_chunks=n_chunks, nseq=nseq, ntile=ntile)
    vec = pl.BlockSpec((1, W_A), const2)
    dense = (H_A, HEAD_A // 2, 2 * HEAD_A)
    state_spec = pl.BlockSpec((nsq,) + dense, lambda b, c: (b, 0, 0, 0))
    shift_spec = pl.BlockSpec((nsq, 1, N_A_COLS), lambda b, c: (b, 0, 0))
    ya, s_new, shift_new = pl.pallas_call(
        kern,
        grid=(batch // nsq, n_chunks),
        in_specs=[
            pl.BlockSpec((ntile, 1, rows, N_A_COLS), lambda b, c: (b, c, 0, 0)),
            shift_spec,
            state_spec,
            pl.BlockSpec((1, N_A_COLS), const2),
            pl.BlockSpec((LANES, 2 * W_A), const2),
            vec, vec, vec, vec, vec, vec, vec,
            pl.BlockSpec((LANES, LANES), const2),
        ],
        out_specs=[
            pl.BlockSpec((ntile, 1, rows, W_A), lambda b, c: (b, c, 0, 0)),
            state_spec,
            shift_spec,
        ],
        out_shape=[
            jax.ShapeDtypeStruct((batch // nseq, n_chunks, rows, W_A), BF16),
            jax.ShapeDtypeStruct((batch,) + dense, F32),
            jax.ShapeDtypeStruct((batch, 1, N_A_COLS), F32),
        ],
        scratch_shapes=[
            pltpu.VMEM((ntile * rows, N_A_COLS), F32),
            pltpu.VMEM((nsq, 1, N_A_COLS), F32),
            pltpu.VMEM((nsq, N_PAIRS, LANES, LANES), F32),
        ],
        compiler_params=pltpu.CompilerParams(
            dimension_semantics=("arbitrary", "arbitrary"), vmem_limit_bytes=VMEM_LIMIT_BYTES),
        name="rwkv7_mix",
    )(z.reshape(-1, n_chunks, rows, N_MAIN), shift, s0.reshape((batch,) + dense), wts["shift_mix"],
      wts["wwa"], wts["w0"],
      wts["a0"], wts["k_k"], wts["k_a"], wts["r_k"], wts["gn_a_g"], wts["gn_a_b"], wts["ones_bd"])
    return (ya.reshape(batch * seq, W_A), s_new.reshape(batch, H_A, HEAD_A, HEAD_A),
            shift_new.reshape(batch, N_A_COLS))


def _ret_kernel(lg_ref, lgv_ref, zb_ref, zt_ref, cos_ref, sin_ref, s0_ref, gng_ref, gnb_ref,
                yb_ref, s_out_ref, *scratch, chunk, n_chunks, nseq, ntile):
    C = chunk
    G = nseq
    NT = ntile
    R = G * C
    log2c = int(math.log2(C))
    c = pl.program_id(1)
    if n_chunks == 1:
        s_in, s_new = s0_ref, s_out_ref
    else:
        (state_scr,) = scratch
        s_in = s_new = state_scr

        @pl.when(c == 0)
        def _():
            state_scr[...] = s0_ref[...]

    heads = range(H_B)
    tiles = range(NT)
    head_slices = [slice(h * DK_B, (h + 1) * DK_B) for h in heads]
    seq_rows = [slice(g * C, (g + 1) * C) for g in range(G)]
    join = lambda parts: parts[0] if len(parts) == 1 else jnp.concatenate(parts, axis=0)
    cos = jnp.concatenate([join([cos_ref[...]] * G)] * H_B, axis=1)
    sin = jnp.concatenate([join([sin_ref[...]] * G)] * H_B, axis=1)
    lgv = lgv_ref[...]
    tok = (lax.broadcasted_iota(jnp.int32, (R, W_B), 0) & (C - 1)).astype(F32)
    q_decay = jnp.exp((tok + 1.0) * lgv)
    k_decay = jnp.exp((C - 1.0 - tok) * lgv) * (DK_B ** -0.5)
    ri = lax.broadcasted_iota(jnp.int32, (R, R), 0)
    ci = lax.broadcasted_iota(jnp.int32, (R, R), 1)
    causal = ((ri >> log2c) == (ci >> log2c)) & (ci <= ri)
    diff_f = jnp.maximum(ri - ci, 0).astype(F32)
    dmask = [jnp.where(causal, jnp.exp(diff_f * lg_ref[h]), 0.0) for h in heads]
    s_decay = jnp.exp(C * lgv)

    def rot_half(x):
        return jnp.concatenate([pltpu.roll(x[:, s], DK_B // 2, axis=1) for s in head_slices], axis=1)

    qr_b, kr_b, q_dec, k_dec, v_b, gate = {}, {}, {}, {}, {}, {}
    for t in tiles:
        q = zb_ref[t, 0, :, 0:W_B]
        k = zb_ref[t, 0, :, W_B:2 * W_B]
        g = jnp.concatenate([zb_ref[t, 0, :, 3 * W_B:4 * W_B - N_TAIL], zt_ref[t, 0]], axis=1)
        qr = q * cos + rot_half(q) * sin
        kr = k * cos + rot_half(k) * sin
        qr_b[t] = qr.astype(BF16)
        kr_b[t] = (kr * (DK_B ** -0.5)).astype(BF16)
        q_dec[t] = (qr * q_decay).astype(BF16)
        k_dec[t] = (kr * k_decay).astype(BF16)
        v_b[t] = zb_ref[t, 0, :, 2 * W_B:3 * W_B].astype(BF16)
        gate[t] = g * _sigmoid(g)

    ids = [(t, h) for t in tiles for h in heads]
    hs = {i: head_slices[i[1]] for i in ids}
    scores = {i: (_dot_nt(qr_b[i[0]][:, hs[i]], kr_b[i[0]][:, hs[i]]) * dmask[i[1]]).astype(BF16)
              for i in ids}
    s_old = {(g, i): s_in[i[0] * G + g, i[1]] for g in range(G) for i in ids}
    cross = {i: join([_dot(q_dec[i[0]][seq_rows[g], hs[i]], s_old[g, i].astype(BF16)) for g in range(G)])
             for i in ids}
    out = {i: _dot(scores[i], v_b[i[0]][:, hs[i]]) + cross[i] for i in ids}
    for g in range(G):
        for i in ids:
            s_new[i[0] * G + g, i[1]] = (s_old[g, i] * s_decay[:, hs[i]]
                                         + _dot_tn(k_dec[i[0]][seq_rows[g], hs[i]],
                                                   v_b[i[0]][seq_rows[g], hs[i]]))
    d = {i: out[i] - jnp.mean(out[i], axis=-1, keepdims=True) for i in ids}
    rstd = {i: lax.rsqrt(jnp.mean(d[i] * d[i], axis=-1, keepdims=True) + GN_EPS_B) for i in ids}
    for t in tiles:
        o_n = jnp.concatenate([d[t, h] * rstd[t, h] for h in heads], axis=1) * gng_ref[...] + gnb_ref[...]
        yb_ref[t, 0] = (o_n * gate[t]).astype(yb_ref.dtype)

    if n_chunks > 1:
        @pl.when(c == n_chunks - 1)
        def _():
            s_out_ref[...] = state_scr[...]


def _retention(z, z_tail, cos, sin, s0, log_g, gn_g, gn_b, batch, seq, chunk, nseq, ntile):
    n_chunks = seq // chunk
    assert seq % chunk == 0 and batch % (nseq * ntile) == 0 and (nseq == 1 or n_chunks == 1)
    rows = nseq * chunk
    nsq = nseq * ntile
    kern = functools.partial(_ret_kernel, chunk=chunk, n_chunks=n_chunks, nseq=nseq, ntile=ntile)
    state_spec = pl.BlockSpec((nsq, H_B, DK_B, DK_B), lambda b, c: (b, 0, 0, 0))
    yb, s_new = pl.pallas_call(
        kern,
        grid=(batch // nsq, n_chunks),
        in_specs=[
            pl.BlockSpec(memory_space=pltpu.SMEM),
            pl.BlockSpec((1, W_B), lambda b, c: (0, 0)),
            pl.BlockSpec((ntile, 1, rows, N_A_COLS), lambda b, c: (b, c, 0, 1)),
            pl.BlockSpec((ntile, 1, rows, N_TAIL), lambda b, c: (b, c, 0, 0)),
            pl.BlockSpec((chunk, DK_B), lambda b, c: (c, 0)),
            pl.BlockSpec((chunk, DK_B), lambda b, c: (c, 0)),
            state_spec,
            pl.BlockSpec((1, W_B), lambda b, c: (0, 0)),
            pl.BlockSpec((1, W_B), lambda b, c: (0, 0)),
        ],
        out_specs=[
            pl.BlockSpec((ntile, 1, rows, W_B), lambda b, c: (b, c, 0, 0)),
            state_spec,
        ],
        out_shape=[
            jax.ShapeDtypeStruct((batch // nseq, n_chunks, rows, W_B), BF16),
            jax.ShapeDtypeStruct((batch, H_B, DK_B, DK_B), F32),
        ],
        scratch_shapes=[pltpu.VMEM((nsq, H_B, DK_B, DK_B), F32)] if n_chunks > 1 else [],
        compiler_params=pltpu.CompilerParams(
            dimension_semantics=("arbitrary", "arbitrary"), vmem_limit_bytes=VMEM_LIMIT_BYTES),
        name="retention",
    )(log_g, jnp.repeat(log_g, DK_B).reshape(1, W_B), z.reshape(-1, n_chunks, rows, N_MAIN),
      z_tail.reshape(-1, n_chunks, rows, N_TAIL), cos, sin, s0, gn_g, gn_b)
    return yb.reshape(batch * seq, W_B), s_new


def _out_kernel(ya_ref, yb_ref, w_ref, x_ref, g_ref, b_ref, y_ref):
    tm = x_ref.shape[0]
    sub = min(tm, OUT_SUB_ROWS)
    blocks = [slice(i, i + sub) for i in range(0, tm, sub)]
    def finish(rows, h):
        t = ALPHA * x_ref[rows, :] + h
        mu = jnp.mean(t, axis=-1, keepdims=True)
        d = t - mu
        var = jnp.mean(d * d, axis=-1, keepdims=True)
        y_ref[rows, :] = d * lax.rsqrt(var + LN_EPS) * g_ref[...] + b_ref[...]

    pending = None
    for rows in blocks:
        h = _dot(ya_ref[rows, :], w_ref[0:W_A, :]) + _dot(yb_ref[rows, :], w_ref[W_A:W_A + W_B, :])
        if pending is not None:
            finish(*pending)
        pending = (rows, h)
    finish(*pending)


def _out_proj(ya, yb, w_out, x, ln_g, ln_b, tm):
    m = x.shape[0]
    assert m % tm == 0
    rows = lambda i: (i, 0)
    const = lambda i: (0, 0)
    return pl.pallas_call(
        _out_kernel,
        grid=(m // tm,),
        in_specs=[
            pl.BlockSpec((tm, W_A), rows),
            pl.BlockSpec((tm, W_B), rows),
            pl.BlockSpec((W_A + W_B, D_MODEL), const),
            pl.BlockSpec((tm, D_MODEL), rows),
            pl.BlockSpec((1, D_MODEL), const),
            pl.BlockSpec((1, D_MODEL), const),
        ],
        out_specs=pl.BlockSpec((tm, D_MODEL), rows),
        out_shape=jax.ShapeDtypeStruct((m, D_MODEL), F32),
        compiler_params=pltpu.CompilerParams(
            dimension_semantics=("arbitrary",), vmem_limit_bytes=VMEM_LIMIT_BYTES),
        name="out_proj_ln",
    )(ya, yb, w_out, x, ln_g, ln_b)


def _rope_tables(pos):
    half = DK_B // 2
    inv = ROPE_BASE ** (-jnp.arange(half, dtype=F32) / half)
    ang = pos.astype(F32)[:, None] * inv[None, :]
    cos = jnp.cos(ang)
    sin = jnp.sin(ang)
    return jnp.concatenate([cos, cos], axis=-1), jnp.concatenate([-sin, sin], axis=-1)


def _tile(m, candidates):
    for t in candidates:
        if m % t == 0:
            return t
    return m


def _trunk(x2d, batch, seq, pos0, s_rwkv, s_shift, s_ret, wts, c_rwkv, c_ret, bblk, need_y, ntile=1,
           proj=None):
    m = batch * seq
    z, z_tail = proj if proj is not None else _in_proj(
        x2d, wts["w_in_main"], wts["w_in_tail"], _tile(m, (1024, 512, 256)), IN_PROJ_TN)
    ya, sa, shift = _rwkv(z, s_shift.reshape(batch, 1, N_A_COLS), s_rwkv, wts, batch, seq, c_rwkv,
                          bblk, ntile)
    cos, sin = _rope_tables(pos0 + jnp.arange(seq, dtype=jnp.int32))
    yb, sb = _retention(z, z_tail, cos, sin, s_ret, wts["log_g"], wts["gn_b_g"], wts["gn_b_b"],
                        batch, seq, c_ret, bblk, ntile)
    y = None
    if need_y:
        y = _out_proj(ya, yb, wts["w_out"], x2d, wts["ln_g"], wts["ln_b"],
                      _tile(m, (512, 256)))
    return y, sa, shift, sb


def _weights(w_out, shift_mix, w0, w_up, a0, a_up, k_k, k_a, r_k, gn_a_g, gn_a_b, gn_b_g, gn_b_b,
             ln_g, ln_b):
    zw = jnp.zeros((LORA_W, W_A), F32)
    wwa = jnp.concatenate([jnp.concatenate([w_up[0], zw], axis=1),
                           jnp.concatenate([zw, a_up[0]], axis=1)], axis=0).astype(BF16)
    lane = jnp.arange(LANES)
    return {
        "w_out": w_out[0].astype(BF16),
        "shift_mix": shift_mix,
        "wwa": wwa,
        "w0": w0, "a0": a0, "k_k": k_k, "k_a": k_a, "r_k": r_k.reshape(1, W_A),
        "gn_a_g": gn_a_g, "gn_a_b": gn_a_b, "gn_b_g": gn_b_g, "gn_b_b": gn_b_b,
        "ln_g": ln_g, "ln_b": ln_b,
        "ones_bd": (lane[:, None] // HEAD_A == lane[None, :] // HEAD_A).astype(BF16),
        "log_g": jnp.log1p(-jnp.exp2(-5.0 - jnp.arange(H_B, dtype=F32))),
    }


def kernel(x_prompt, x_sample, state_rwkv, state_shift, state_ret, meta_tokens, w_in, w_out, shift_mix,
           w0, w_up, a0, a_up, k_k, k_a, r_k, gn_a_g, gn_a_b, gn_b_g, gn_b_b, ln_g, ln_b):
    bp, tp, _ = x_prompt.shape
    bs, ts, _ = x_sample.shape
    wts = _weights(w_out, shift_mix, w0, w_up, a0, a_up, k_k, k_a, r_k, gn_a_g, gn_a_b, gn_b_g,
                   gn_b_b, ln_g, ln_b)
    x_s = x_sample.reshape(bs * ts, D_MODEL)
    z_s, zt_s, z_m, zt_m, wts["w_in_main"], wts["w_in_tail"] = _in_proj_cast(
        x_s, meta_tokens.astype(F32), w_in[0], IN_PROJ_CAST_TN)
    _, ra_m, sh_m, rb_m = _trunk(
        None, 1, N_META, 0,
        jnp.zeros((1, H_A, HEAD_A, HEAD_A), F32), jnp.zeros((1, N_A_COLS), F32),
        jnp.zeros((1, H_B, DK_B, DK_B), F32), wts, N_META, N_META, 1, False, proj=(z_m, zt_m))
    rep = lambda s: jnp.broadcast_to(s, (bp,) + s.shape[1:])
    y_p, ra_p, sh_p, rb_p = _trunk(
        x_prompt.reshape(bp * tp, D_MODEL), bp, tp, N_META, rep(ra_m), rep(sh_m), rep(rb_m),
        wts, 64, 128, 1, True, ntile=4)
    y_s, ra_s, sh_s, rb_s = _trunk(
        x_s, bs, ts, PAST_LEN, state_rwkv[0], state_shift[0], state_ret[0],
        wts, ts, ts, SAMPLE_SEQS_PER_TILE, True, ntile=2, proj=(z_s, zt_s))
    return (y_p.reshape(bp, tp, D_MODEL), y_s.reshape(bs, ts, D_MODEL),
            ra_p[None], sh_p[None], rb_p[None], ra_s[None], sh_s[None], rb_s[None])
```

```python
import functools
import math

import jax
import jax.numpy as jnp
from jax import lax
from jax.experimental import pallas as pl
from jax.experimental.pallas import tpu as pltpu

D_MODEL = 2048
N_META = 16
W_A = 1024
HEAD_A = 64
H_A = W_A // HEAD_A
LORA_W = 64
LORA_A = 64
W_B = D_MODEL - W_A
H_B = 8
DK_B = W_B // H_B
N_A_COLS = 4 * W_A + LORA_W + LORA_A
N_B_COLS = 4 * W_B
N_IN = N_A_COLS + N_B_COLS
N_TAIL = 128
N_MAIN = N_IN - N_TAIL
IN_PROJ_TN = 2048
IN_PROJ_CAST_TN = 1024
OUT_SUB_ROWS = 128
SAMPLE_SEQS_PER_TILE = 8
DEPTH = 1
PAST_LEN = 16384
ALPHA = (2.0 * DEPTH) ** 0.25
ROPE_BASE = 10000.0
LN_EPS = 1e-5
GN_EPS_A = 64e-5
GN_EPS_B = 1e-5
EXP_NEG_HALF = math.exp(-0.5)

LANES = 128
N_PAIRS = W_A // LANES
VMEM_LIMIT_BYTES = 56 * 1024 * 1024

F32 = jnp.float32
BF16 = jnp.bfloat16


def _dot(a, b):
    return jnp.dot(a, b, preferred_element_type=F32)


def _dot_nt(a, b):
    return lax.dot_general(a, b, (((1,), (1,)), ((), ())), preferred_element_type=F32)


def _dot_tn(a, b):
    return lax.dot_general(a, b, (((0,), (0,)), ((), ())), preferred_element_type=F32)


def _split_dot(x, w_bf16):
    hi = x.astype(BF16)
    lo = (x - hi.astype(F32)).astype(BF16)
    return _dot(hi, w_bf16) + _dot(lo, w_bf16)


def _split_dot_rhs(w_bf16, x):
    hi = x.astype(BF16)
    lo = (x - hi.astype(F32)).astype(BF16)
    return _dot(w_bf16, hi) + _dot(w_bf16, lo)


def _sigmoid(x):
    return 0.5 * jnp.tanh(0.5 * x) + 0.5


def _in_proj_kernel(x_ref, w_ref, w_tail_ref, o_ref, o_tail_ref):
    xb = x_ref[...].astype(BF16)
    o_ref[...] = _dot(xb, w_ref[...])

    @pl.when(pl.program_id(1) == pl.num_programs(1) - 1)
    def _():
        o_tail_ref[...] = _dot(xb, w_tail_ref[...])


def _in_proj_cast_kernel(x_ref, x2_ref, w_ref, w_tail_ref,
                         o_ref, o_tail_ref, o2_ref, o2_tail_ref, wb_ref, wb_tail_ref):
    xb = x_ref[...].astype(BF16)
    x2b = x2_ref[...].astype(BF16)
    wb = w_ref[...].astype(BF16)
    wb_ref[...] = wb
    o_ref[...] = _dot(xb, wb)
    o2_ref[...] = _dot(x2b, wb)

    @pl.when(pl.program_id(1) == pl.num_programs(1) - 1)
    def _():
        wb_tail = w_tail_ref[...].astype(BF16)
        wb_tail_ref[...] = wb_tail
        o_tail_ref[...] = _dot(xb, wb_tail)
        o2_tail_ref[...] = _dot(x2b, wb_tail)


def _z_specs(tm, tn):
    return [pl.BlockSpec((tm, tn), lambda i, j: (i, j)), pl.BlockSpec((tm, N_TAIL), lambda i, j: (i, 0))]


def _z_shapes(m):
    return [jax.ShapeDtypeStruct((m, N_MAIN), F32), jax.ShapeDtypeStruct((m, N_TAIL), F32)]


_IN_PROJ_PARAMS = pltpu.CompilerParams(
    dimension_semantics=("arbitrary", "arbitrary"), vmem_limit_bytes=VMEM_LIMIT_BYTES)


def _in_proj(x, w_main, w_tail, tm, tn):
    m, k = x.shape
    assert m % tm == 0 and N_MAIN % tn == 0
    return pl.pallas_call(
        _in_proj_kernel,
        grid=(m // tm, N_MAIN // tn),
        in_specs=[pl.BlockSpec((tm, k), lambda i, j: (i, 0)),
                  pl.BlockSpec((k, tn), lambda i, j: (0, j)),
                  pl.BlockSpec((k, N_TAIL), lambda i, j: (0, 0))],
        out_specs=_z_specs(tm, tn),
        out_shape=_z_shapes(m),
        compiler_params=_IN_PROJ_PARAMS,
        name="in_proj",
    )(x, w_main, w_tail)


def _in_proj_cast(x, x2, w, tn):
    (m, k), m2 = x.shape, x2.shape[0]
    assert N_MAIN % tn == 0 and w.shape == (k, N_IN)
    z2_specs = [pl.BlockSpec((m2, tn), lambda i, j: (0, j)), pl.BlockSpec((m2, N_TAIL), lambda i, j: (0, 0))]
    return pl.pallas_call(
        _in_proj_cast_kernel,
        grid=(1, N_MAIN // tn),
        in_specs=[pl.BlockSpec((m, k), lambda i, j: (0, 0)),
                  pl.BlockSpec((m2, k), lambda i, j: (0, 0)),
                  pl.BlockSpec((k, tn), lambda i, j: (0, j)),
                  pl.BlockSpec((k, N_TAIL), lambda i, j: (0, N_MAIN // N_TAIL))],
        out_specs=_z_specs(m, tn) + z2_specs + [pl.BlockSpec((k, tn), lambda i, j: (0, j)),
                                                pl.BlockSpec((k, N_TAIL), lambda i, j: (0, 0))],
        out_shape=_z_shapes(m) + _z_shapes(m2) + [jax.ShapeDtypeStruct((k, N_MAIN), BF16),
                                                  jax.ShapeDtypeStruct((k, N_TAIL), BF16)],
        compiler_params=_IN_PROJ_PARAMS,
        name="in_proj_cast",
    )(x, x2, w, w)


def _rwkv_kernel(za_ref, shift_ref, s0_ref, mix_ref, wwa_ref, w0_ref, a0_ref, kk_ref, ka_ref,
                 rk_ref, gng_ref, gnb_ref, ones_ref,
                 ya_ref, s_out_ref, shift_out_ref,
                 zs_scr, prev_scr, state_scr, *, chunk, n_chunks, nseq, ntile):
    C = chunk
    G = nseq
    NT = ntile
    R = G * C
    R2 = 2 * R
    RT = NT * R
    NS = NT * G
    log2c = int(math.log2(C))
    c = pl.program_id(1)
    HALF_A = HEAD_A // 2
    half0 = lax.broadcasted_iota(jnp.int32, (HALF_A, LANES), 1) < HEAD_A

    @pl.when(c == 0)
    def _():
        prev_scr[...] = shift_ref[...]
        for s in range(NS):
            for p in range(N_PAIRS):
                x0 = s0_ref[s, 2 * p]
                x1 = s0_ref[s, 2 * p + 1]
                even = jnp.where(half0, x0, pltpu.roll(x1, HEAD_A, axis=1))
                odd = jnp.where(half0, pltpu.roll(x0, HEAD_A, axis=1), x1)
                zh = jnp.zeros_like(even)
                state_scr[s, p, pl.ds(0, HALF_A, stride=2), :] = jnp.where(half0, even, zh)
                state_scr[s, p, pl.ds(1, HALF_A, stride=2), :] = jnp.where(half0, odd, zh)
                state_scr[s, p, pl.ds(HEAD_A, HALF_A, stride=2), :] = jnp.where(half0, zh, even)
                state_scr[s, p, pl.ds(HEAD_A + 1, HALF_A, stride=2), :] = jnp.where(half0, zh, odd)

    ones_bd = ones_ref[...]
    head0 = {n: lax.broadcasted_iota(jnp.int32, (n, LANES), 1) < HEAD_A for n in {R, C}}
    ri =lax.broadcasted_iota(jnp.int32, (RT, RT), 0)
    ci = lax.broadcasted_iota(jnp.int32, (RT, RT), 1)
    tri_incl = (((ri >> log2c) == (ci >> log2c)) & (ci <= ri)).astype(BF16)
    i2 = lax.broadcasted_iota(jnp.int32, (R, R2), 0)
    j2 = lax.broadcasted_iota(jnp.int32, (R, R2), 1) & (R - 1)
    same_seq = (i2 >> log2c) == (j2 >> log2c)
    strict = same_seq & (j2 < i2)
    incl = same_seq & (j2 <= i2)
    eye = (i2 == j2).astype(F32)
    zero = jnp.zeros((R, R2), F32)
    col_head0 = lax.broadcasted_iota(jnp.int32, (R, R2), 1) < R

    pairs = range(N_PAIRS)
    tiles = range(NT)
    pair_slices = [slice(p * LANES, (p + 1) * LANES) for p in pairs]
    tile_rows = [slice(t * R, (t + 1) * R) for t in tiles]
    seq_rows = [slice(g * C, (g + 1) * C) for g in range(G)]
    rows2 = lambda xs: jnp.concatenate(xs, axis=0)

    def stack(x, first_head=None):
        m = head0[x.shape[0]] if first_head is None else first_head
        x = x.astype(BF16)
        z = jnp.zeros_like(x)
        return rows2([jnp.where(m, x, z), jnp.where(m, z, x)])

    def join(parts):
        return parts[0] if len(parts) == 1 else jnp.concatenate(parts, axis=0)

    def seg_sum(x, split):
        tall = rows2([x[:, s] for s in pair_slices])
        tall = _split_dot(tall, ones_bd) if split else _dot(tall.astype(BF16), ones_bd)
        return jnp.concatenate([tall[p * RT:(p + 1) * RT] for p in pairs], axis=1)

    za = join([za_ref[t, 0] for t in tiles])
    first = (lax.broadcasted_iota(jnp.int32, za.shape, 0) & (C - 1)) == 0
    carried = join([jnp.broadcast_to(prev_scr[s], (C, N_A_COLS)) for s in range(NS)])
    prev = jnp.where(first, carried, pltpu.roll(za, 1, axis=0))
    for s in range(NS):
        last = (s % G + 1) * C - 1
        prev_scr[s] = za_ref[s // G, 0, last:last + 1, :]
    lo = slice(4 * W_A, 4 * W_A + LANES)
    zwx = za[:, lo] + (prev[:, lo] - za[:, lo]) * mix_ref[:, lo]
    lora_in = jnp.where(lax.broadcasted_iota(jnp.int32, (RT, LANES), 1) < LORA_W, jnp.tanh(zwx), zwx)
    lora = _dot(lora_in.astype(BF16), wwa_ref[...])
    zs_scr[...] = za + (prev - za) * mix_ref[...]

    r = zs_scr[:, 0:W_A]
    k = zs_scr[:, W_A:2 * W_A]
    v = zs_scr[:, 2 * W_A:3 * W_A]
    g_gate = zs_scr[:, 3 * W_A:4 * W_A]
    logw = -EXP_NEG_HALF * _sigmoid(w0_ref[...] + lora[:, 0:W_A])
    a = _sigmoid(a0_ref[...] + lora[:, W_A:2 * W_A])
    cum = _split_dot_rhs(tri_incl, logw)
    cum_last = [cum[(s + 1) * C - 1:(s + 1) * C] for s in range(NS)]
    cum_end = cum_last[0] if NS == 1 else jnp.concatenate(
        [jnp.broadcast_to(x, (C, W_A)) for x in cum_last], axis=0)
    kkr = k * kk_ref[...]
    kk = kkr * jnp.minimum(lax.rsqrt(seg_sum(kkr * kkr, False)), 1e12)
    kp = k * (1.0 + (a - 1.0) * ka_ref[...])
    b = kk * a
    inv_p = jnp.exp(-cum)
    p_end = jnp.exp(cum_end - cum)
    decay_end = [jnp.exp(x) for x in cum_last]
    at = -kk * jnp.exp(cum - logw)
    rt = r * jnp.exp(cum)
    bt = b * inv_p
    kt = kp * inv_p
    bh = b * p_end
    kh = kp * p_end

    def recurrence(ids):
        part = lambda x, key: x[tile_rows[key[0]], pair_slices[key[1]]]
        at_p = {i: part(at, i) for i in ids}
        rt_p = {i: part(rt, i) for i in ids}
        v_p = {i: part(v, i) for i in ids}
        bh_p = {i: part(bh, i) for i in ids}
        kh_p = {i: part(kh, i) for i in ids}
        ar = {i: rows2([at_p[i], rt_p[i]]).astype(BF16) for i in ids}
        b_s = {i: stack(part(bt, i)) for i in ids}
        k_s = {i: stack(part(kt, i)) for i in ids}
        v_s = {i: stack(v_p[i]) for i in ids}
        if R2 % LANES == 0:
            sc = {i: _dot_nt(ar[i], rows2([b_s[i], k_s[i]])) for i in ids}
            ab, ak = {i: sc[i][:R, :R2] for i in ids}, {i: sc[i][:R, R2:] for i in ids}
            rb, rk = {i: sc[i][R:, :R2] for i in ids}, {i: sc[i][R:, R2:] for i in ids}
        else:
            ab_rb = {i: _dot_nt(ar[i], b_s[i]) for i in ids}
            ak_rk = {i: _dot_nt(ar[i], k_s[i]) for i in ids}
            ab, rb = {i: ab_rb[i][:R] for i in ids}, {i: ab_rb[i][R:] for i in ids}
            ak, rk = {i: ak_rk[i][:R] for i in ids}, {i: ak_rk[i][R:] for i in ids}
        n_pow = {i: jnp.where(strict, ab[i], zero) for i in ids}
        l_ak = {i: jnp.where(strict, ak[i], zero).astype(BF16) for i in ids}
        m_rb = {i: jnp.where(incl, rb[i], zero).astype(BF16) for i in ids}
        m_rk = {i: jnp.where(incl, rk[i], zero).astype(BF16) for i in ids}
        t_inv = {i: eye + n_pow[i] for i in ids}
        n_pow = {i: _dot(n_pow[i].astype(BF16), stack(n_pow[i], col_head0)) for i in ids}
        for level in range(log2c - 1):
            n_st = {i: stack(n_pow[i], col_head0) for i in ids}
            if level < log2c - 2:
                both = {i: _dot(rows2([t_inv[i], n_pow[i]]).astype(BF16), n_st[i]) for i in ids}
                t_inv = {i: t_inv[i] + both[i][:R] for i in ids}
                n_pow = {i: both[i][R:] for i in ids}
            else:
                t_inv = {i: t_inv[i] + _dot(t_inv[i].astype(BF16), n_st[i]) for i in ids}
        s_old = {(g, i): state_scr[i[0] * G + g, i[1]] for g in range(G) for i in ids}
        s_old_b = {key: val.astype(BF16) for key, val in s_old.items()}
        yo = {(g, i): _dot_nt(rows2([at_p[i][seq_rows[g]], rt_p[i][seq_rows[g]]]).astype(BF16),
                              s_old_b[g, i]) for g in range(G) for i in ids}
        y_state = {i: join([yo[g, i][:C] for g in range(G)]) for i in ids}
        o_state = {i: join([yo[g, i][C:] for g in range(G)]) for i in ids}
        lv = {i: _dot(rows2([l_ak[i], m_rk[i]]), v_s[i]) for i in ids}
        y = {i: y_state[i] + lv[i][:R] for i in ids}
        u = {i: _dot(t_inv[i].astype(BF16), stack(y[i])) for i in ids}
        u_s = {i: stack(u[i]) for i in ids}
        o_p = {i: o_state[i] + _dot(m_rb[i], u_s[i]) + lv[i][R:] for i in ids}
        for g in range(G):
            rows = seq_rows[g]
            for i in ids:
                seq = i[0] * G + g
                if G == 1:
                    uv = rows2([u_s[i], v_s[i]])
                else:
                    uv = rows2([stack(u[i][rows]), stack(v_p[i][rows])])
                bk = rows2([stack(bh_p[i][rows]), stack(kh_p[i][rows])])
                state_scr[seq, i[1]] = (s_old[g, i] * decay_end[seq][:, pair_slices[i[1]]]
                                        + _dot_tn(uv, bk))
        return o_p

    o_p = recurrence([(t, p) for t in tiles for p in pairs])
    out = join([jnp.concatenate([o_p[t, p] for p in pairs], axis=1) for t in tiles])

    inv_n = 1.0 / HEAD_A
    d = out - seg_sum(out, True) * inv_n
    var = seg_sum(d * d, False) * inv_n
    o_n = d * lax.rsqrt(var + GN_EPS_A) * gng_ref[...] + gnb_ref[...]
    bonus = seg_sum(r * kp * rk_ref[...], False) * v
    ya = (o_n + bonus) * (g_gate * _sigmoid(g_gate))
    for t in tiles:
        ya_ref[t, 0] = ya[tile_rows[t]].astype(ya_ref.dtype)

    @pl.when(c == n_chunks - 1)
    def _():
        shift_out_ref[...] = prev_scr[...]
        for s in range(NS):
            for p in range(N_PAIRS):
                e0 = state_scr[s, p, pl.ds(0, HALF_A, stride=2), :]
                o0 = state_scr[s, p, pl.ds(1, HALF_A, stride=2), :]
                e1 = state_scr[s, p, pl.ds(HEAD_A, HALF_A, stride=2), :]
                o1 = state_scr[s, p, pl.ds(HEAD_A + 1, HALF_A, stride=2), :]
                s_out_ref[s, 2 * p] = jnp.where(half0, e0, pltpu.roll(o0, HEAD_A, axis=1))
                s_out_ref[s, 2 * p + 1] = jnp.where(half0, pltpu.roll(e1, HEAD_A, axis=1), o1)


def _rwkv(z, shift, s0, wts, batch, seq, chunk, nseq, ntile):
    n_chunks = seq // chunk
    assert seq % chunk == 0 and batch % (nseq * ntile) == 0 and (nseq == 1 or n_chunks == 1)
    rows = nseq * chunk
    nsq = nseq * ntile
    const2 = lambda b, c: (0, 0)
    kern = functools.partial(_rwkv_kernel, chunk=chunk, n_chunks=n_chunks, nseq=nseq, ntile=ntile)
    vec = pl.BlockSpec((1, W_A), const2)
    dense = (H_A, HEAD_A // 2, 2 * HEAD_A)
    state_spec = pl.BlockSpec((nsq,) + dense, lambda b, c: (b, 0, 0, 0))
    shift_spec = pl.BlockSpec((nsq, 1, N_A_COLS), lambda b, c: (b, 0, 0))
    ya, s_new, shift_new = pl.pallas_call(
        kern,
        grid=(batch // nsq, n_chunks),
        in_specs=[
            pl.BlockSpec((ntile, 1, rows, N_A_COLS), lambda b, c: (b, c, 0, 0)),
            shift_spec,
            state_spec,
            pl.BlockSpec((1, N_A_COLS), const2),
            pl.BlockSpec((LANES, 2 * W_A), const2),
            vec, vec, vec, vec, vec, vec, vec,
            pl.BlockSpec((LANES, LANES), const2),
        ],
        out_specs=[
            pl.BlockSpec((ntile, 1, rows, W_A), lambda b, c: (b, c, 0, 0)),
            state_spec,
            shift_spec,
        ],
        out_shape=[
            jax.ShapeDtypeStruct((batch // nseq, n_chunks, rows, W_A), BF16),
            jax.ShapeDtypeStruct((batch,) + dense, F32),
            jax.ShapeDtypeStruct((batch, 1, N_A_COLS), F32),
        ],
        scratch_shapes=[
            pltpu.VMEM((ntile * rows, N_A_COLS), F32),
            pltpu.VMEM((nsq, 1, N_A_COLS), F32),
            pltpu.VMEM((nsq, N_PAIRS, LANES, LANES), F32),
        ],
        compiler_params=pltpu.CompilerParams(
            dimension_semantics=("arbitrary", "arbitrary"), vmem_limit_bytes=VMEM_LIMIT_BYTES),
        name="rwkv7_mix",
    )(z.reshape(-1, n_chunks, rows, N_MAIN), shift, s0.reshape((batch,) + dense), wts["shift_mix"],
      wts["wwa"], wts["w0"],
      wts["a0"], wts["k_k"], wts["k_a"], wts["r_k"], wts["gn_a_g"], wts["gn_a_b"], wts["ones_bd"])
    return (ya.reshape(batch * seq, W_A), s_new.reshape(batch, H_A, HEAD_A, HEAD_A),
            shift_new.reshape(batch, N_A_COLS))


def _ret_kernel(lg_ref, lgv_ref, zb_ref, zt_ref, cos_ref, sin_ref, s0_ref, gng_ref, gnb_ref,
                yb_ref, s_out_ref, *scratch, chunk, n_chunks, nseq, ntile):
    C = chunk
    G = nseq
    NT = ntile
    R = G * C
    log2c = int(math.log2(C))
    c = pl.program_id(1)
    if n_chunks == 1:
        s_in, s_new = s0_ref, s_out_ref
    else:
        (state_scr,) = scratch
        s_in = s_new = state_scr

        @pl.when(c == 0)
        def _():
            state_scr[...] = s0_ref[...]

    heads = range(H_B)
    tiles = range(NT)
    head_slices = [slice(h * DK_B, (h + 1) * DK_B) for h in heads]
    seq_rows = [slice(g * C, (g + 1) * C) for g in range(G)]
    join = lambda parts: parts[0] if len(parts) == 1 else jnp.concatenate(parts, axis=0)
    cos = jnp.concatenate([join([cos_ref[...]] * G)] * H_B, axis=1)
    sin = jnp.concatenate([join([sin_ref[...]] * G)] * H_B, axis=1)
    lgv = lgv_ref[...]
    tok = (lax.broadcasted_iota(jnp.int32, (R, W_B), 0) & (C - 1)).astype(F32)
    q_decay = jnp.exp((tok + 1.0) * lgv)
    k_decay = jnp.exp((C - 1.0 - tok) * lgv) * (DK_B ** -0.5)
    ri = lax.broadcasted_iota(jnp.int32, (R, R), 0)
    ci = lax.broadcasted_iota(jnp.int32, (R, R), 1)
    causal = ((ri >> log2c) == (ci >> log2c)) & (ci <= ri)
    diff_f = jnp.maximum(ri - ci, 0).astype(F32)
    dmask = [jnp.where(causal, jnp.exp(diff_f * lg_ref[h]), 0.0) for h in heads]
    s_decay = jnp.exp(C * lgv)

    def rot_half(x):
        return jnp.concatenate([pltpu.roll(x[:, s], DK_B // 2, axis=1) for s in head_slices], axis=1)

    qr_b, kr_b, q_dec, k_dec, v_b, gate = {}, {}, {}, {}, {}, {}
    for t in tiles:
        q = zb_ref[t, 0, :, 0:W_B]
        k = zb_ref[t, 0, :, W_B:2 * W_B]
        g = jnp.concatenate([zb_ref[t, 0, :, 3 * W_B:4 * W_B - N_TAIL], zt_ref[t, 0]], axis=1)
        qr = q * cos + rot_half(q) * sin
        kr = k * cos + rot_half(k) * sin
        qr_b[t] = qr.astype(BF16)
        kr_b[t] = (kr * (DK_B ** -0.5)).astype(BF16)
        q_dec[t] = (qr * q_decay).astype(BF16)
        k_dec[t] = (kr * k_decay).astype(BF16)
        v_b[t] = zb_ref[t, 0, :, 2 * W_B:3 * W_B].astype(BF16)
        gate[t] = g * _sigmoid(g)

    ids = [(t, h) for t in tiles for h in heads]
    hs = {i: head_slices[i[1]] for i in ids}
    scores = {i: (_dot_nt(qr_b[i[0]][:, hs[i]], kr_b[i[0]][:, hs[i]]) * dmask[i[1]]).astype(BF16)
              for i in ids}
    s_old = {(g, i): s_in[i[0] * G + g, i[1]] for g in range(G) for i in ids}
    cross = {i: join([_dot(q_dec[i[0]][seq_rows[g], hs[i]], s_old[g, i].astype(BF16)) for g in range(G)])
             for i in ids}
    out = {i: _dot(scores[i], v_b[i[0]][:, hs[i]]) + cross[i] for i in ids}
    for g in range(G):
        for i in ids:
            s_new[i[0] * G + g, i[1]] = (s_old[g, i] * s_decay[:, hs[i]]
                                         + _dot_tn(k_dec[i[0]][seq_rows[g], hs[i]],
                                                   v_b[i[0]][seq_rows[g], hs[i]]))
    d = {i: out[i] - jnp.mean(out[i], axis=-1, keepdims=True) for i in ids}
    rstd = {i: lax.rsqrt(jnp.mean(d[i] * d[i], axis=-1, keepdims=True) + GN_EPS_B) for i in ids}
    for t in tiles:
        o_n = jnp.concatenate([d[t, h] * rstd[t, h] for h in heads], axis=1) * gng_ref[...] + gnb_ref[...]
        yb_ref[t, 0] = (o_n * gate[t]).astype(yb_ref.dtype)

    if n_chunks > 1:
        @pl.when(c == n_chunks - 1)
        def _():
            s_out_ref[...] = state_scr[...]


def _retention(z, z_tail, cos, sin, s0, log_g, gn_g, gn_b, batch, seq, chunk, nseq, ntile):
    n_chunks = seq // chunk
    assert seq % chunk == 0 and batch % (nseq * ntile) == 0 and (nseq == 1 or n_chunks == 1)
    rows = nseq * chunk
    nsq = nseq * ntile
    kern = functools.partial(_ret_kernel, chunk=chunk, n_chunks=n_chunks, nseq=nseq, ntile=ntile)
    state_spec = pl.BlockSpec((nsq, H_B, DK_B, DK_B), lambda b, c: (b, 0, 0, 0))
    yb, s_new = pl.pallas_call(
        kern,
        grid=(batch // nsq, n_chunks),
        in_specs=[
            pl.BlockSpec(memory_space=pltpu.SMEM),
            pl.BlockSpec((1, W_B), lambda b, c: (0, 0)),
            pl.BlockSpec((ntile, 1, rows, N_A_COLS), lambda b, c: (b, c, 0, 1)),
            pl.BlockSpec((ntile, 1, rows, N_TAIL), lambda b, c: (b, c, 0, 0)),
            pl.BlockSpec((chunk, DK_B), lambda b, c: (c, 0)),
            pl.BlockSpec((chunk, DK_B), lambda b, c: (c, 0)),
            state_spec,
            pl.BlockSpec((1, W_B), lambda b, c: (0, 0)),
            pl.BlockSpec((1, W_B), lambda b, c: (0, 0)),
        ],
        out_specs=[
            pl.BlockSpec((ntile, 1, rows, W_B), lambda b, c: (b, c, 0, 0)),
            state_spec,
        ],
        out_shape=[
            jax.ShapeDtypeStruct((batch // nseq, n_chunks, rows, W_B), BF16),
            jax.ShapeDtypeStruct((batch, H_B, DK_B, DK_B), F32),
        ],
        scratch_shapes=[pltpu.VMEM((nsq, H_B, DK_B, DK_B), F32)] if n_chunks > 1 else [],
        compiler_params=pltpu.CompilerParams(
            dimension_semantics=("arbitrary", "arbitrary"), vmem_limit_bytes=VMEM_LIMIT_BYTES),
        name="retention",
    )(log_g, jnp.repeat(log_g, DK_B).reshape(1, W_B), z.reshape(-1, n_chunks, rows, N_MAIN),
      z_tail.reshape(-1, n_chunks, rows, N_TAIL), cos, sin, s0, gn_g, gn_b)
    return yb.reshape(batch * seq, W_B), s_new


def _out_kernel(ya_ref, yb_ref, w_ref, x_ref, g_ref, b_ref, y_ref):
    tm = x_ref.shape[0]
    sub = min(tm, OUT_SUB_ROWS)
    blocks = [slice(i, i + sub) for i in range(0, tm, sub)]
    def finish(rows, h):
        t = ALPHA * x_ref[rows, :] + h
        mu = jnp.mean(t, axis=-1, keepdims=True)
        d = t - mu
        var = jnp.mean(d * d, axis=-1, keepdims=True)
        y_ref[rows, :] = d * lax.rsqrt(var + LN_EPS) * g_ref[...] + b_ref[...]

    pending = None
    for rows in blocks:
        h = _dot(ya_ref[rows, :], w_ref[0:W_A, :]) + _dot(yb_ref[rows, :], w_ref[W_A:W_A + W_B, :])
        if pending is not None:
            finish(*pending)
        pending = (rows, h)
    finish(*pending)


def _out_proj(ya, yb, w_out, x, ln_g, ln_b, tm):
    m = x.shape[0]
    assert m % tm == 0
    rows = lambda i: (i, 0)
    const = lambda i: (0, 0)
    return pl.pallas_call(
        _out_kernel,
        grid=(m // tm,),
        in_specs=[
            pl.BlockSpec((tm, W_A), rows),
            pl.BlockSpec((tm, W_B), rows),
            pl.BlockSpec((W_A + W_B, D_MODEL), const),
            pl.BlockSpec((tm, D_MODEL), rows),
            pl.BlockSpec((1, D_MODEL), const),
            pl.BlockSpec((1, D_MODEL), const),
        ],
        out_specs=pl.BlockSpec((tm, D_MODEL), rows),
        out_shape=jax.ShapeDtypeStruct((m, D_MODEL), F32),
        compiler_params=pltpu.CompilerParams(
            dimension_semantics=("arbitrary",), vmem_limit_bytes=VMEM_LIMIT_BYTES),
        name="out_proj_ln",
    )(ya, yb, w_out, x, ln_g, ln_b)


def _rope_tables(pos):
    half = DK_B // 2
    inv = ROPE_BASE ** (-jnp.arange(half, dtype=F32) / half)
    ang = pos.astype(F32)[:, None] * inv[None, :]
    cos = jnp.cos(ang)
    sin = jnp.sin(ang)
    return jnp.concatenate([cos, cos], axis=-1), jnp.concatenate([-sin, sin], axis=-1)


def _tile(m, candidates):
    for t in candidates:
        if m % t == 0:
            return t
    return m


def _trunk(x2d, batch, seq, pos0, s_rwkv, s_shift, s_ret, wts, c_rwkv, c_ret, bblk, need_y, ntile=1,
           proj=None):
    m = batch * seq
    z, z_tail = proj if proj is not None else _in_proj(
        x2d, wts["w_in_main"], wts["w_in_tail"], _tile(m, (1024, 512, 256)), IN_PROJ_TN)
    ya, sa, shift = _rwkv(z, s_shift.reshape(batch, 1, N_A_COLS), s_rwkv, wts, batch, seq, c_rwkv,
                          bblk, ntile)
    cos, sin = _rope_tables(pos0 + jnp.arange(seq, dtype=jnp.int32))
    yb, sb = _retention(z, z_tail, cos, sin, s_ret, wts["log_g"], wts["gn_b_g"], wts["gn_b_b"],
                        batch, seq, c_ret, bblk, ntile)
    y = None
    if need_y:
        y = _out_proj(ya, yb, wts["w_out"], x2d, wts["ln_g"], wts["ln_b"],
                      _tile(m, (512, 256)))
    return y, sa, shift, sb


def _weights(w_out, shift_mix, w0, w_up, a0, a_up, k_k, k_a, r_k, gn_a_g, gn_a_b, gn_b_g, gn_b_b,
             ln_g, ln_b):
    zw = jnp.zeros((LORA_W, W_A), F32)
    wwa = jnp.concatenate([jnp.concatenate([w_up[0], zw], axis=1),
                           jnp.concatenate([zw, a_up[0]], axis=1)], axis=0).astype(BF16)
    lane = jnp.arange(LANES)
    return {
        "w_out": w_out[0].astype(BF16),
        "shift_mix": shift_mix,
        "wwa": wwa,
        "w0": w0, "a0": a0, "k_k": k_k, "k_a": k_a, "r_k": r_k.reshape(1, W_A),
        "gn_a_g": gn_a_g, "gn_a_b": gn_a_b, "gn_b_g": gn_b_g, "gn_b_b": gn_b_b,
        "ln_g": ln_g, "ln_b": ln_b,
        "ones_bd": (lane[:, None] // HEAD_A == lane[None, :] // HEAD_A).astype(BF16),
        "log_g": jnp.log1p(-jnp.exp2(-5.0 - jnp.arange(H_B, dtype=F32))),
    }


def kernel(x_prompt, x_sample, state_rwkv, state_shift, state_ret, meta_tokens, w_in, w_out, shift_mix,
           w0, w_up, a0, a_up, k_k, k_a, r_k, gn_a_g, gn_a_b, gn_b_g, gn_b_b, ln_g, ln_b):
    bp, tp, _ = x_prompt.shape
    bs, ts, _ = x_sample.shape
    wts = _weights(w_out, shift_mix, w0, w_up, a0, a_up, k_k, k_a, r_k, gn_a_g, gn_a_b, gn_b_g,
                   gn_b_b, ln_g, ln_b)
    x_s = x_sample.reshape(bs * ts, D_MODEL)
    z_s, zt_s, z_m, zt_m, wts["w_in_main"], wts["w_in_tail"] = _in_proj_cast(
        x_s, meta_tokens.astype(F32), w_in[0], IN_PROJ_CAST_TN)
    _, ra_m, sh_m, rb_m = _trunk(
        None, 1, N_META, 0,
        jnp.zeros((1, H_A, HEAD_A, HEAD_A), F32), jnp.zeros((1, N_A_COLS), F32),
        jnp.zeros((1, H_B, DK_B, DK_B), F32), wts, N_META, N_META, 1, False, proj=(z_m, zt_m))
    rep = lambda s: jnp.broadcast_to(s, (bp,) + s.shape[1:])
    y_p, ra_p, sh_p, rb_p = _trunk(
        x_prompt.reshape(bp * tp, D_MODEL), bp, tp, N_META, rep(ra_m), rep(sh_m), rep(rb_m),
        wts, 64, 128, 1, True, ntile=4)
    y_s, ra_s, sh_s, rb_s = _trunk(
        x_s, bs, ts, PAST_LEN, state_rwkv[0], state_shift[0], state_ret[0],
        wts, ts, ts, SAMPLE_SEQS_PER_TILE, True, ntile=2, proj=(z_s, zt_s))
    return (y_p.reshape(bp, tp, D_MODEL), y_s.reshape(bs, ts, D_MODEL),
            ra_p[None], sh_p[None], rb_p[None], ra_s[None], sh_s[None], rb_s[None])
```
